```python
import jax
import jax.numpy as jnp
from jax import lax
import numpy as np

D_MODEL = 1024
BATCH = 4
SEQ = 8192
DEPTH = 4

HEAD_DIM = 64
N_HEADS = D_MODEL // HEAD_DIM
SB_HEADS = N_HEADS // 2
NSA_HEADS = N_HEADS - SB_HEADS
NSA_KV_GROUPS = 2
FOX_HEADS = N_HEADS
CMP_LEN = 32
CMP_STRIDE = 16
SEL_LEN = 64
SEL_TOP = 16
WINDOW = 512
N_BRANCH = 3
Q_BLOCK = 128
ROPE_THETA = 500000.0
ROT_DIM = HEAD_DIM // 4
D_FF = 256 * ((8 * D_MODEL // 3 + 255) // 256)
CONV_WIDTH = 3
NORM_EPS = 1e-6
N_EVEN = (DEPTH + 1) // 2
N_ODD = DEPTH // 2
SB_W = SB_HEADS * HEAD_DIM
NSA_QW = NSA_HEADS * HEAD_DIM
NSA_KVW = NSA_KV_GROUPS * HEAD_DIM
EVEN_SPLITS = (SB_W, SB_W, SB_W, NSA_QW) + (NSA_KVW,) * 6 + (NSA_HEADS * N_BRANCH,)
EVEN_IN = sum(EVEN_SPLITS)
FOX_W = FOX_HEADS * HEAD_DIM
ODD_SPLITS = (FOX_W, FOX_W, FOX_W, FOX_HEADS)
ODD_IN = sum(ODD_SPLITS)

kernel_name = 'hybrid_stickbreak_nsa_fox_convffn'


def rmsnorm(x, g):
    x32 = x.astype(jnp.float32)
    y = x32 * lax.rsqrt(jnp.mean(x32 * x32, axis=-1, keepdims=True) + NORM_EPS)
    return y.astype(x.dtype) * g


def split_cols(h, sizes):
    offs = np.cumsum(sizes)[:-1].tolist()
    return jnp.split(h, offs, axis=-1)


def partial_rope(x, pos):
    half = ROT_DIM // 2
    inv_freq = ROPE_THETA ** (-(jnp.arange(half, dtype=jnp.float32) * 2.0 / ROT_DIM))
    ang = pos[:, None] * inv_freq[None, :]
    cos = jnp.cos(ang)[None, :, None, :].astype(x.dtype)
    sin = jnp.sin(ang)[None, :, None, :].astype(x.dtype)
    x1 = x[..., :half]
    x2 = x[..., half:ROT_DIM]
    return jnp.concatenate([x1 * cos - x2 * sin, x1 * sin + x2 * cos, x[..., ROT_DIM:]], axis=-1)


def masked_softmax(s, mask):
    s = jnp.where(mask, s, -jnp.inf)
    m = jnp.max(s, axis=-1, keepdims=True)
    m = jnp.where(jnp.isfinite(m), m, 0.0)
    e = jnp.where(mask, jnp.exp(s - m), 0.0)
    d = jnp.sum(e, axis=-1, keepdims=True)
    return e / jnp.where(d > 0, d, 1.0)


def sweep_query_blocks(fn, seq):
    out = lax.map(fn, jnp.arange(seq // Q_BLOCK))
    nb, bsz, q, nh, dh = out.shape
    return jnp.moveaxis(out, 0, 1).reshape(bsz, nb * q, nh, dh)


def stick_breaking_attention(q, k, v):
    bsz, seq, nh, dh = q.shape
    scale = dh ** -0.5
    kpos = jnp.arange(seq)

    def block(i):
        q0 = i * Q_BLOCK
        qb = lax.dynamic_slice_in_dim(q, q0, Q_BLOCK, axis=1)
        tpos = q0 + jnp.arange(Q_BLOCK)
        z = jnp.einsum('bqhd,bshd->bhqs', qb, k, preferred_element_type=jnp.float32) * scale
        mask = kpos[None, :] < tpos[:, None]
        log_keep = jnp.where(mask, jax.nn.log_sigmoid(-z), 0.0)
        after = lax.cumsum(log_keep, axis=3, reverse=True) - log_keep
        a = jnp.where(mask, jnp.exp(jax.nn.log_sigmoid(z) + after), 0.0)
        return jnp.einsum('bhqs,bshd->bqhd', a.astype(v.dtype), v)

    return sweep_query_blocks(block, seq)


def forgetting_attention(q, k, v, log_f):
    bsz, seq, nh, dh = q.shape
    scale = dh ** -0.5
    kpos = jnp.arange(seq)
    cum = jnp.transpose(jnp.cumsum(log_f, axis=1), (0, 2, 1))

    def block(i):
        q0 = i * Q_BLOCK
        qb = lax.dynamic_slice_in_dim(q, q0, Q_BLOCK, axis=1)
        cq = lax.dynamic_slice_in_dim(cum, q0, Q_BLOCK, axis=2)
        tpos = q0 + jnp.arange(Q_BLOCK)
        s = jnp.einsum('bqhd,bshd->bhqs', qb, k, preferred_element_type=jnp.float32) * scale
        s = s + (cq[..., :, None] - cum[..., None, :])
        p = masked_softmax(s, kpos[None, :] <= tpos[:, None])
        return jnp.einsum('bhqs,bshd->bqhd', p.astype(v.dtype), v)

    return sweep_query_blocks(block, seq)


def nsa_attention(q, k_cmp, v_cmp, k_slc, v_slc, k_win, v_win, gates, pos_k, pos_v, w_ck, w_cv):
    bsz, seq, nh, dh = q.shape
    ng = k_cmp.shape[2]
    rep = nh // ng
    scale = dh ** -0.5
    n_cmp = (seq - CMP_LEN) // CMP_STRIDE + 1
    c_start = jnp.arange(n_cmp) * CMP_STRIDE
    cidx = c_start[:, None] + jnp.arange(CMP_LEN)[None, :]

    def compress(t, pe, w):
        blocks = t[:, cidx] + pe[None, None, :, None, :]
        blocks = jnp.moveaxis(blocks, 3, 2).reshape(bsz, n_cmp, ng, CMP_LEN * dh)
        return blocks @ w

    cmp_end = c_start + CMP_LEN - 1
    kc = partial_rope(compress(k_cmp, pos_k, w_ck), cmp_end.astype(jnp.float32))
    vc = compress(v_cmp, pos_v, w_cv)
    n_sel = seq // SEL_LEN
    top = min(SEL_TOP, n_sel)
    ks = k_slc.reshape(bsz, n_sel, SEL_LEN, ng, dh).transpose(0, 3, 1, 2, 4)
    vs = v_slc.reshape(bsz, n_sel, SEL_LEN, ng, dh).transpose(0, 3, 1, 2, 4)
    sel_start = jnp.arange(n_sel) * SEL_LEN
    overlap = (c_start[:, None] < sel_start[None, :] + SEL_LEN) & (c_start[:, None] + CMP_LEN > sel_start[None, :])
    cmp_to_sel = overlap.astype(jnp.float32)
    blk = jnp.arange(n_sel)
    bi = jnp.arange(bsz)[:, None, None, None]
    gi = jnp.arange(ng)[None, :, None, None]
    kw = jnp.pad(k_win, ((0, 0), (WINDOW, 0), (0, 0), (0, 0)))
    vw = jnp.pad(v_win, ((0, 0), (WINDOW, 0), (0, 0), (0, 0)))

    def block(i):
        q0 = i * Q_BLOCK
        tpos = q0 + jnp.arange(Q_BLOCK)
        qb = lax.dynamic_slice_in_dim(q, q0, Q_BLOCK, axis=1).reshape(bsz, Q_BLOCK, ng, rep, dh)
        sc = jnp.einsum('bqgrd,bcgd->bgrqc', qb, kc, preferred_element_type=jnp.float32) * scale
        pc = masked_softmax(sc, cmp_end[None, :] <= tpos[:, None])
        o_cmp = jnp.einsum('bgrqc,bcgd->bqgrd', pc.astype(vc.dtype), vc)
        imp = jnp.einsum('bgrqc,cn->bgqn', pc, cmp_to_sel)
        cur = tpos // SEL_LEN
        causal = blk[None, :] <= cur[:, None]
        forced = (blk[None, :] == 0) | (blk[None, :] == cur[:, None]) | (blk[None, :] == cur[:, None] - 1)
        score = jnp.where(causal, jnp.where(forced, jnp.inf, imp), -jnp.inf)
        top_val, top_idx = lax.top_k(score, top)
        sel_ok = top_val > -jnp.inf
        kg = ks[bi, gi, top_idx]
        vg = vs[bi, gi, top_idx]
        kpos = top_idx[..., None] * SEL_LEN + jnp.arange(SEL_LEN)
        smask = sel_ok[..., None] & (kpos <= tpos[None, None, :, None, None])
        ss = jnp.einsum('bqgrd,bgqnld->bgrqnl', qb, kg, preferred_element_type=jnp.float32) * scale
        ps = masked_softmax(ss.reshape(bsz, ng, rep, Q_BLOCK, top * SEL_LEN),
                            smask.reshape(bsz, ng, 1, Q_BLOCK, top * SEL_LEN))
        ps = ps.reshape(bsz, ng, rep, Q_BLOCK, top, SEL_LEN)
        o_slc = jnp.einsum('bgrqnl,bgqnld->bqgrd', ps.astype(vg.dtype), vg)
        kwb = lax.dynamic_slice_in_dim(kw, q0, Q_BLOCK + WINDOW, axis=1)
        vwb = lax.dynamic_slice_in_dim(vw, q0, Q_BLOCK + WINDOW, axis=1)
        wpos = q0 - WINDOW + jnp.arange(Q_BLOCK + WINDOW)
        wmask = (wpos[None, :] >= 0) & (wpos[None, :] <= tpos[:, None]) & (wpos[None, :] > tpos[:, None] - WINDOW)
        sw = jnp.einsum('bqgrd,bkgd->bgrqk', qb, kwb, preferred_element_type=jnp.float32) * scale
        pw = masked_softmax(sw, wmask)
        o_win = jnp.einsum('bgrqk,bkgd->bqgrd', pw.astype(vwb.dtype), vwb)
        gb = lax.dynamic_slice_in_dim(gates, q0, Q_BLOCK, axis=1).reshape(bsz, Q_BLOCK, ng, rep, N_BRANCH)
        o = gb[..., 0:1] * o_cmp + gb[..., 1:2] * o_slc + gb[..., 2:3] * o_win
        return o.reshape(bsz, Q_BLOCK, nh, dh)

    return sweep_query_blocks(block, seq)


def even_mixer(h, w_in, pos_k, pos_v, w_ck, w_cv, w_out):
    bsz, seq, _ = h.shape
    sq, sk, sv, nq, kc, vc, ksl, vsl, kwn, vwn, g = split_cols(h @ w_in, EVEN_SPLITS)
    pos = jnp.arange(seq, dtype=jnp.float32)

    def heads(t):
        return t.reshape(bsz, seq, -1, HEAD_DIM)

    o_sb = stick_breaking_attention(heads(sq), heads(sk), heads(sv))
    o_nsa = nsa_attention(partial_rope(heads(nq), pos), heads(kc), heads(vc),
                          partial_rope(heads(ksl), pos), heads(vsl),
                          partial_rope(heads(kwn), pos), heads(vwn),
                          jax.nn.sigmoid(g.reshape(bsz, seq, NSA_HEADS, N_BRANCH)),
                          pos_k, pos_v, w_ck, w_cv)
    o = jnp.concatenate([o_sb.reshape(bsz, seq, SB_W), o_nsa.reshape(bsz, seq, NSA_QW)], axis=-1)
    return o @ w_out


def odd_mixer(h, w_in, b_f, w_out):
    bsz, seq, _ = h.shape
    q, k, v, f = split_cols(h @ w_in, ODD_SPLITS)
    log_f = jax.nn.log_sigmoid((f + b_f).astype(jnp.float32))
    o = forgetting_attention(q.reshape(bsz, seq, FOX_HEADS, HEAD_DIM),
                             k.reshape(bsz, seq, FOX_HEADS, HEAD_DIM),
                             v.reshape(bsz, seq, FOX_HEADS, HEAD_DIM), log_f)
    return o.reshape(bsz, seq, FOX_W) @ w_out


def conv_glu_ffn(h, w_in, conv_w, conv_b, w_out):
    a, b = jnp.split(h @ w_in, 2, axis=-1)
    taps = conv_w[:, None, :]
    a = lax.conv_general_dilated(a, taps, window_strides=(1,), padding=[(CONV_WIDTH - 1, 0)],
                                 dimension_numbers=('NWC', 'WIO', 'NWC'),
                                 feature_group_count=a.shape[-1]) + conv_b
    return (jax.nn.silu(a) * b) @ w_out


def setup_inputs(seed: int = 0) -> dict:
    key = jax.random.key(seed)
    k = jax.random.split(key, 17)

    def nrm(kk, shape, scale):
        return jax.random.normal(kk, shape, jnp.float32) * scale

    d = D_MODEL
    lc = CMP_LEN * HEAD_DIM
    return {
        'x': nrm(k[0], (BATCH, SEQ, d), 1.0),
        'attn_norm': 1.0 + nrm(k[1], (DEPTH, d), 0.02),
        'ffn_norm': 1.0 + nrm(k[2], (DEPTH, d), 0.02),
        'ev_w_in': nrm(k[3], (N_EVEN, d, EVEN_IN), d ** -0.5),
        'ev_cmp_pos_k': nrm(k[4], (N_EVEN, CMP_LEN, HEAD_DIM), 0.2),
        'ev_cmp_pos_v': nrm(k[5], (N_EVEN, CMP_LEN, HEAD_DIM), 0.2),
        'ev_cmp_w_k': nrm(k[6], (N_EVEN, lc, HEAD_DIM), lc ** -0.5),
        'ev_cmp_w_v': nrm(k[7], (N_EVEN, lc, HEAD_DIM), lc ** -0.5),
        'ev_w_out': nrm(k[8], (N_EVEN, SB_W + NSA_QW, d), (SB_W + NSA_QW) ** -0.5),
        'od_w_in': nrm(k[9], (N_ODD, d, ODD_IN), d ** -0.5),
        'od_b_f': 2.0 + nrm(k[10], (N_ODD, FOX_HEADS), 0.5),
        'od_w_out': nrm(k[11], (N_ODD, FOX_W, d), FOX_W ** -0.5),
        'ffn_w_in': nrm(k[12], (DEPTH, d, 2 * D_FF), d ** -0.5),
        'ffn_conv_w': nrm(k[13], (DEPTH, CONV_WIDTH, D_FF), CONV_WIDTH ** -0.5),
        'ffn_conv_b': nrm(k[14], (DEPTH, D_FF), 0.02),
        'ffn_w_out': nrm(k[15], (DEPTH, D_FF, d), D_FF ** -0.5),
        'final_norm': 1.0 + nrm(k[16], (d,), 0.02),
    }


def reference(x, attn_norm, ffn_norm, ev_w_in, ev_cmp_pos_k, ev_cmp_pos_v, ev_cmp_w_k, ev_cmp_w_v,
              ev_w_out, od_w_in, od_b_f, od_w_out, ffn_w_in, ffn_conv_w, ffn_conv_b, ffn_w_out,
              final_norm):
    for layer in range(DEPTH):
        h = rmsnorm(x, attn_norm[layer])
        if layer % 2 == 0:
            e = layer // 2
            x = x + even_mixer(h, ev_w_in[e], ev_cmp_pos_k[e], ev_cmp_pos_v[e],
                               ev_cmp_w_k[e], ev_cmp_w_v[e], ev_w_out[e])
        else:
            o = layer // 2
            x = x + odd_mixer(h, od_w_in[o], od_b_f[o], od_w_out[o])
        x = x + conv_glu_ffn(rmsnorm(x, ffn_norm[layer]), ffn_w_in[layer], ffn_conv_w[layer],
                             ffn_conv_b[layer], ffn_w_out[layer])
    return rmsnorm(x, final_norm)
```

```python
import functools
import math

import jax
import jax.numpy as jnp
from jax import lax
from jax.experimental import pallas as pl
from jax.experimental.pallas import tpu as pltpu

F32, BF16 = jnp.float32, jnp.bfloat16

D_MODEL = 1024
HEAD_DIM = 64
N_HEADS = D_MODEL // HEAD_DIM
SB_HEADS = N_HEADS // 2
NSA_HEADS = N_HEADS - SB_HEADS
NSA_KV_GROUPS = 2
NSA_REP = NSA_HEADS // NSA_KV_GROUPS
FOX_HEADS = N_HEADS
CMP_LEN = 32
CMP_STRIDE = 16
SEL_LEN = 64
SEL_TOP = 16
WINDOW = 512
N_BRANCH = 3
ROPE_THETA = 500000.0
ROT_DIM = HEAD_DIM // 4
D_FF = 2816
CONV_WIDTH = 3
NORM_EPS = 1e-6
SB_W = SB_HEADS * HEAD_DIM
NSA_QW = NSA_HEADS * HEAD_DIM
NSA_KVW = NSA_KV_GROUPS * HEAD_DIM
FOX_W = FOX_HEADS * HEAD_DIM

LANES = 128
SEL_LANES = 128
QK_SCALE = HEAD_DIM ** -0.5
LOG2E = math.log2(math.e)
SEL_NEG = -(2.0 ** 30)
VMEM_LIMIT = 56 * 2 ** 20

_NT = (((1,), (1,)), ((), ()))


def _params(sem):
    return pltpu.CompilerParams(dimension_semantics=sem, vmem_limit_bytes=VMEM_LIMIT)


def _dot(a, b):
    return jnp.dot(a, b, preferred_element_type=F32)


def _dot_nt(a, b):
    return lax.dot_general(a, b, _NT, preferred_element_type=F32)


def _rmsnorm(x, g):
    ms = jnp.mean(x * x, axis=-1, keepdims=True)
    return (x * lax.rsqrt(ms + NORM_EPS)) * g


def _rope(y, c, s1, s2):
    return y * c + pltpu.roll(y, LANES - ROT_DIM // 2, 1) * s1 + pltpu.roll(y, ROT_DIM // 2, 1) * s2


def _split3(x):
    hi = x.astype(BF16)
    r1 = x - hi.astype(F32)
    mid = r1.astype(BF16)
    lo = (r1 - mid.astype(F32)).astype(BF16)
    return hi, mid, lo


def _rope_tables(pos):
    half = ROT_DIM // 2
    inv_freq = ROPE_THETA ** (-(jnp.arange(half, dtype=F32) * 2.0 / ROT_DIM))
    ang = pos.astype(F32)[:, None] * inv_freq[None, :]
    cos, sin = jnp.cos(ang), jnp.sin(ang)
    n = pos.shape[0]
    one = jnp.ones((n, HEAD_DIM - ROT_DIM), F32)
    zero = jnp.zeros((n, HEAD_DIM - ROT_DIM), F32)
    z8 = jnp.zeros((n, half), F32)
    c = jnp.concatenate([cos, cos, one], -1)
    s1 = jnp.concatenate([-sin, z8, zero], -1)
    s2 = jnp.concatenate([z8, sin, zero], -1)
    two = lambda t: jnp.concatenate([t, t], -1)
    return two(c), two(s1), two(s2)


def _even_proj_kernel(x_ref, g_ref, w_ref, c_ref, s1_ref, s2_ref,
                      sbq_ref, sbk_ref, sbv_ref, nq_ref, kc_ref, vc_ref,
                      ksl_ref, vsl_ref, kwn_ref, vwn_ref, gate_ref):
    hb = _rmsnorm(x_ref[...], g_ref[...]).astype(BF16)
    c, s1, s2 = c_ref[...], s1_ref[...], s2_ref[...]

    def seg(a, n):
        return _dot(hb, w_ref[:, a:a + n])

    sbq_ref[...] = (seg(0, SB_W) * QK_SCALE).astype(BF16)
    sbk_ref[...] = seg(SB_W, SB_W).astype(BF16)
    sbv_ref[...] = seg(2 * SB_W, SB_W).astype(BF16)
    base = 3 * SB_W
    for j in range(NSA_QW // LANES):
        y = _rope(seg(base + j * LANES, LANES), c, s1, s2)
        nq_ref[:, j * LANES:(j + 1) * LANES] = (y * QK_SCALE).astype(BF16)
    base += NSA_QW
    kc_ref[...] = seg(base, LANES)
    vc_ref[...] = seg(base + LANES, LANES)
    ksl_ref[...] = _rope(seg(base + 2 * LANES, LANES), c, s1, s2).astype(BF16)
    vsl_ref[...] = seg(base + 3 * LANES, LANES).astype(BF16)
    kwn_ref[...] = _rope(seg(base + 4 * LANES, LANES), c, s1, s2).astype(BF16)
    vwn_ref[...] = seg(base + 5 * LANES, LANES).astype(BF16)
    gl = seg(base + 6 * LANES, 2 * LANES)
    gate_ref[...] = 1.0 / (1.0 + jnp.exp(-gl))


def _even_proj(xf, g, w, tabs, seq, tm):
    t, d = xf.shape
    nsb = seq // tm
    n = w.shape[1]
    row = lambda width: pl.BlockSpec((tm, width), lambda i: (i, 0))
    tab = pl.BlockSpec((tm, LANES), lambda i: (i % nsb, 0))
    out_shape = (
        [jax.ShapeDtypeStruct((t, SB_W), BF16)] * 3
        + [jax.ShapeDtypeStruct((t, NSA_QW), BF16)]
        + [jax.ShapeDtypeStruct((t, LANES), F32)] * 2
        + [jax.ShapeDtypeStruct((t, LANES), BF16)] * 4
        + [jax.ShapeDtypeStruct((t, 2 * LANES), F32)]
    )
    out_specs = [row(SB_W)] * 3 + [row(NSA_QW)] + [row(LANES)] * 6 + [row(2 * LANES)]
    return pl.pallas_call(
        _even_proj_kernel,
        grid=(t // tm,),
        in_specs=[row(d), pl.BlockSpec((1, d), lambda i: (0, 0)),
                  pl.BlockSpec((d, n), lambda i: (0, 0)), tab, tab, tab],
        out_specs=out_specs,
        out_shape=out_shape,
        compiler_params=_params(("parallel",)),
        name="even_proj",
    )(xf, g, w, *tabs)


def _prep_even_w(w):
    d = w.shape[0]
    main = w[:, :3 * SB_W + NSA_QW + 6 * NSA_KVW]
    gates = w[:, 3 * SB_W + NSA_QW + 6 * NSA_KVW:].reshape(d, NSA_KV_GROUPS, NSA_REP, N_BRANCH)
    gates = jnp.transpose(gates, (0, 1, 3, 2)).reshape(d, NSA_KV_GROUPS, N_BRANCH * NSA_REP)
    gates = jnp.pad(gates, ((0, 0), (0, 0), (0, LANES - N_BRANCH * NSA_REP)))
    return jnp.concatenate([main, gates.reshape(d, NSA_KV_GROUPS * LANES)], axis=1).astype(BF16)


def _sb_kernel(q_ref, k_ref, v_ref, o_ref, *, tile):
    i = pl.program_id(2)
    lane = lax.broadcasted_iota(jnp.int32, (1, LANES), 1)
    r = lax.broadcasted_iota(jnp.int32, (tile, tile), 0)
    c = lax.broadcasted_iota(jnp.int32, (tile, tile), 1)
    later = jnp.where(r > c, 1.0, 0.0).astype(BF16)
    diag = c < r
    q = q_ref[...]
    outs = []
    for h in range(2):
        hmask = (lane < HEAD_DIM) if h == 0 else (lane >= HEAD_DIM)
        qh = jnp.where(hmask, q, jnp.zeros_like(q))

        def step(jt, carry, masked, qh=qh):
            acc, keep_sum = carry
            k0 = pl.multiple_of(jt * tile, tile)
            z = _dot_nt(qh, k_ref[pl.ds(k0, tile), :])
            ls = jnp.minimum(z, 0.0) - jnp.log1p(jnp.exp(-jnp.abs(z)))
            lk = ls - z
            if masked:
                lk = jnp.where(diag, lk, 0.0)
            hi = lk.astype(BF16)
            lo = (lk - hi.astype(F32)).astype(BF16)
            after = _dot(hi, later) + _dot(lo, later)
            a = jnp.exp(ls + after + keep_sum)
            if masked:
                a = jnp.where(diag, a, 0.0)
            acc = acc + _dot(a.astype(BF16), v_ref[pl.ds(k0, tile), :])
            keep_sum = keep_sum + after[:, 0:1] + lk[:, 0:1]
            return acc, keep_sum

        carry = step(i, (jnp.zeros((tile, LANES), F32), jnp.zeros((tile, 1), F32)), True)
        acc, _ = lax.fori_loop(0, i, lambda n, cr: step(i - 1 - n, cr, False), carry)
        outs.append(acc)
    o_ref[...] = jnp.where(lane < HEAD_DIM, outs[0], outs[1]).astype(BF16)


def _sb_attention(q, k, v, batch, seq, tile):
    t, w = q.shape
    nq = seq // tile
    kv = pl.BlockSpec((seq, LANES), lambda b, p, i: (b, p))
    qo = pl.BlockSpec((tile, LANES), lambda b, p, i: (b * nq + i, p))
    return pl.pallas_call(
        functools.partial(_sb_kernel, tile=tile),
        grid=(batch, w // LANES, nq),
        in_specs=[qo, kv, kv],
        out_specs=qo,
        out_shape=jax.ShapeDtypeStruct((t, w), BF16),
        compiler_params=_params(("parallel", "parallel", "arbitrary")),
        name="sb_attention",
    )(q, k, v)


def _compress_kernel(xk_ref, xv_ref, pek_ref, pev_ref, wk_ref, wv_ref, c_ref, s1_ref, s2_ref,
                     kc_ref, vc_ref, *, nc):
    def comp(x_ref, pe_ref, w_ref):
        x = x_ref[0]
        top = _dot((x + pe_ref[0:1, :]).astype(BF16), w_ref[0])
        bot = _dot((x + pe_ref[1:2, :]).astype(BF16), w_ref[1])
        return top + pltpu.roll(bot, nc - 1, 0)

    kc = _rope(comp(xk_ref, pek_ref, wk_ref), c_ref[...], s1_ref[...], s2_ref[...])
    kc_ref[0] = kc.astype(BF16)
    vc_ref[0] = comp(xv_ref, pev_ref, wv_ref).astype(BF16)


def _prep_cmp(pe, w):
    half = CMP_LEN // 2
    w3 = w.reshape(CMP_LEN, HEAD_DIM, HEAD_DIM)
    eye = jnp.eye(NSA_KV_GROUPS, dtype=w.dtype)
    parts, pes = [], []
    for s in range(2):
        wh = jnp.einsum('ldo,gh->lgdho', w3[s * half:(s + 1) * half], eye)
        parts.append(wh.reshape(half * NSA_KVW, NSA_KVW))
        pes.append(jnp.broadcast_to(pe[s * half:(s + 1) * half, None, :],
                                    (half, NSA_KV_GROUPS, HEAD_DIM)).reshape(1, half * NSA_KVW))
    return jnp.concatenate(pes, 0), jnp.stack(parts).astype(BF16)


def _compress(kc_raw, vc_raw, pek, pev, wk, wv, ctabs, batch, seq):
    nc = seq // CMP_STRIDE
    cw = CMP_STRIDE * NSA_KVW
    xk = kc_raw.reshape(batch, nc, cw)
    xv = vc_raw.reshape(batch, nc, cw)
    xs = pl.BlockSpec((1, nc, cw), lambda b: (b, 0, 0))
    pes = pl.BlockSpec((2, cw), lambda b: (0, 0))
    ws = pl.BlockSpec((2, cw, NSA_KVW), lambda b: (0, 0, 0))
    tab = pl.BlockSpec((nc, LANES), lambda b: (0, 0))
    out = pl.BlockSpec((1, nc, LANES), lambda b: (b, 0, 0))
    return pl.pallas_call(
        functools.partial(_compress_kernel, nc=nc),
        grid=(batch,),
        in_specs=[xs, xs, pes, pes, ws, ws, tab, tab, tab],
        out_specs=[out, out],
        out_shape=[jax.ShapeDtypeStruct((batch, nc, LANES), BF16)] * 2,
        compiler_params=_params(("parallel",)),
        name="nsa_compress",
    )(xk, xv, pek, pev, wk, wv, *ctabs)


QB = 128


def _softmax_rows(s):
    m = jnp.max(s, axis=-1, keepdims=True)
    m = jnp.where(m > -jnp.inf, m, 0.0)
    e = jnp.exp(s - m)
    d = jnp.sum(e, axis=-1, keepdims=True)
    return e / jnp.where(d > 0, d, 1.0)


def _nsa_kernel(q_ref, kc_ref, vc_ref, ks_ref, vs_ref, kw_ref, vw_ref, g_ref, mt_ref, oh_ref,
                o_ref, score_ref, *, nc, tk, wlen):
    g = pl.program_id(1)
    i = pl.program_id(2)
    q0 = i * QB
    rows = NSA_REP * QB

    q2 = q_ref[...]
    pr = lax.broadcasted_iota(jnp.int32, (NSA_REP * HEAD_DIM, LANES), 0)
    pc = lax.broadcasted_iota(jnp.int32, (NSA_REP * HEAD_DIM, LANES), 1) - g * HEAD_DIM
    in_group = (pc >= 0) & (pc < HEAD_DIM)
    q4 = jnp.concatenate(
        [_dot(q2, jnp.where(in_group & (pr - hh * HEAD_DIM == pc), 1.0, 0.0).astype(BF16)).astype(BF16)
         for hh in range(NSA_REP)], axis=0)

    sc = _dot_nt(q4, kc_ref[0])
    cend = lax.broadcasted_iota(jnp.int32, (rows, nc), 1) * CMP_STRIDE + (CMP_LEN - 1)
    tpos_c = q0 + (lax.broadcasted_iota(jnp.int32, (rows, nc), 0) & (QB - 1))
    p_cmp = _softmax_rows(jnp.where(cend <= tpos_c, sc, -jnp.inf))
    o_cmp = _dot(p_cmp.astype(BF16), vc_ref[0])

    p_sum = p_cmp[0:QB] + p_cmp[QB:2 * QB] + p_cmp[2 * QB:3 * QB] + p_cmp[3 * QB:4 * QB]
    mt = mt_ref[...]
    imp_t = sum(_dot_nt(mt, part) for part in _split3(p_sum))
    nidx = lax.broadcasted_iota(jnp.int32, (SEL_LANES, QB), 0)
    cur = (q0 + lax.broadcasted_iota(jnp.int32, (SEL_LANES, QB), 1)) // SEL_LEN
    causal = nidx <= cur
    forced = (nidx == 0) | (nidx == cur) | (nidx == cur - 1)
    score = jnp.where(causal, jnp.where(forced, jnp.inf, imp_t), -jnp.inf)
    score_ref[...] = score

    def rank_step(m, cnt):
        rowm = score_ref[pl.ds(m, 1), :]
        ge = jnp.where(rowm >= score, 1.0, 0.0)
        gt = jnp.where(rowm > score, 1.0, 0.0)
        return cnt + jnp.where(nidx > m, ge, gt)

    n_causal = (q0 + QB - 1) // SEL_LEN + 1
    cnt = lax.fori_loop(0, n_causal, rank_step, jnp.zeros((SEL_LANES, QB), F32))
    bias_t = jnp.where(causal & (cnt < float(SEL_TOP)), 0.0, SEL_NEG)
    selb = jnp.transpose(bias_t).astype(BF16)
    qs = jnp.concatenate([q4, jnp.concatenate([selb] * NSA_REP, axis=0)], axis=1)

    def sel_step(jt, carry, masked):
        m, l, acc = carry
        k0 = pl.multiple_of(jt * tk, tk)
        kk = jnp.concatenate([ks_ref[pl.ds(k0, tk), :], oh_ref[pl.ds(k0, tk), :]], axis=1)
        s = _dot_nt(qs, kk)
        if masked:
            kpos = k0 + lax.broadcasted_iota(jnp.int32, (rows, tk), 1)
            tpos = q0 + (lax.broadcasted_iota(jnp.int32, (rows, tk), 0) & (QB - 1))
            s = jnp.where(kpos <= tpos, s, -jnp.inf)
        m_new = jnp.maximum(m, jnp.max(s, axis=-1, keepdims=True))
        alpha = jnp.exp(m - m_new)
        p = jnp.exp(s - m_new)
        l = alpha * l + jnp.sum(p, axis=-1, keepdims=True)
        acc = alpha * acc + _dot(p.astype(BF16), vs_ref[pl.ds(k0, tk), :])
        return m_new, l, acc

    n_full = q0 // tk
    init = (jnp.full((rows, 1), -jnp.inf, F32), jnp.zeros((rows, 1), F32), jnp.zeros((rows, LANES), F32))
    carry = lax.fori_loop(0, n_full, lambda jt, cr: sel_step(jt, cr, False), init)
    _, l_s, acc_s = sel_step(n_full, carry, True)
    o_slc = acc_s / l_s

    w0 = pl.multiple_of(jnp.maximum(q0 - WINDOW, 0), QB)
    sw = _dot_nt(q4, kw_ref[pl.ds(w0, wlen), :])
    kpos = w0 + lax.broadcasted_iota(jnp.int32, (rows, wlen), 1)
    tpos = q0 + (lax.broadcasted_iota(jnp.int32, (rows, wlen), 0) & (QB - 1))
    p_win = _softmax_rows(jnp.where((kpos <= tpos) & (kpos > tpos - WINDOW), sw, -jnp.inf))
    o_win = _dot(p_win.astype(BF16), vw_ref[pl.ds(w0, wlen), :])

    gates = g_ref[...]
    orow = lax.broadcasted_iota(jnp.int32, (LANES, NSA_REP * HEAD_DIM), 0) - g * HEAD_DIM
    ocol = lax.broadcasted_iota(jnp.int32, (LANES, NSA_REP * HEAD_DIM), 1)
    o_group = (orow >= 0) & (orow < HEAD_DIM)
    out = jnp.zeros((QB, NSA_REP * HEAD_DIM), F32)
    for hh in range(NSA_REP):
        sl = slice(hh * QB, (hh + 1) * QB)
        gate = lambda br: gates[:, br * NSA_REP + hh:br * NSA_REP + hh + 1]
        o_h = gate(0) * o_cmp[sl] + gate(1) * o_slc[sl] + gate(2) * o_win[sl]
        place = jnp.where(o_group & (ocol - hh * HEAD_DIM == orow), 1.0, 0.0).astype(BF16)
        out = out + _dot(o_h.astype(BF16), place)
    o_ref[...] = out.astype(BF16)


def _nsa_consts(seq):
    nc = seq // CMP_STRIDE
    c_start = jnp.arange(nc) * CMP_STRIDE
    sel_start = jnp.arange(SEL_LANES) * SEL_LEN
    real = (jnp.arange(SEL_LANES) < seq // SEL_LEN)[:, None] & (jnp.arange(nc) < (seq - CMP_LEN) // CMP_STRIDE + 1)[None, :]
    overlap = (c_start[None, :] < sel_start[:, None] + SEL_LEN) & (c_start[None, :] + CMP_LEN > sel_start[:, None])
    mt = (overlap & real).astype(BF16)
    oh = (jnp.arange(seq)[:, None] // SEL_LEN == jnp.arange(SEL_LANES)[None, :]).astype(BF16)
    return mt, oh


def _nsa_attention(nq, kc, vc, ksl, vsl, kwn, vwn, gates, mt, oh, batch, seq, tk):
    t = nq.shape[0]
    nqb = seq // QB
    nc = seq // CMP_STRIDE
    wlen = WINDOW + QB
    gw = NSA_REP * HEAD_DIM
    qspec = pl.BlockSpec((QB, gw), lambda b, g, i: (b * nqb + i, g))
    cspec = pl.BlockSpec((1, nc, LANES), lambda b, g, i: (b, 0, 0))
    kvspec = pl.BlockSpec((seq, LANES), lambda b, g, i: (b, 0))
    gspec = pl.BlockSpec((QB, LANES), lambda b, g, i: (b * nqb + i, g))
    return pl.pallas_call(
        functools.partial(_nsa_kernel, nc=nc, tk=tk, wlen=wlen),
        grid=(batch, NSA_KV_GROUPS, nqb),
        in_specs=[qspec, cspec, cspec, kvspec, kvspec, kvspec, kvspec, gspec,
                  pl.BlockSpec((SEL_LANES, nc), lambda b, g, i: (0, 0)),
                  pl.BlockSpec((seq, SEL_LANES), lambda b, g, i: (0, 0))],
        out_specs=qspec,
        out_shape=jax.ShapeDtypeStruct((t, NSA_QW), BF16),
        scratch_shapes=[pltpu.VMEM((SEL_LANES, QB), F32)],
        compiler_params=_params(("parallel", "parallel", "arbitrary")),
        name="nsa_attention",
    )(nq, kc, vc, ksl, vsl, kwn, vwn, gates, mt, oh)


def _odd_proj_kernel(x_ref, g_ref, w_ref, bf_ref, q_ref, k_ref, v_ref, cum_ref, carry_ref, *, tm, nsb):
    i = pl.program_id(0)
    hb = _rmsnorm(x_ref[...], g_ref[...]).astype(BF16)
    q_ref[...] = (_dot(hb, w_ref[:, 0:FOX_W]) * (QK_SCALE * LOG2E)).astype(BF16)
    k_ref[...] = _dot(hb, w_ref[:, FOX_W:2 * FOX_W]).astype(BF16)
    v_ref[...] = _dot(hb, w_ref[:, 2 * FOX_W:3 * FOX_W]).astype(BF16)
    f = _dot(hb, w_ref[:, 3 * FOX_W:3 * FOX_W + LANES]) + bf_ref[...]
    log_f = jnp.minimum(f, 0.0) - jnp.log1p(jnp.exp(-jnp.abs(f)))
    r = lax.broadcasted_iota(jnp.int32, (tm, tm), 0)
    c = lax.broadcasted_iota(jnp.int32, (tm, tm), 1)
    tri = jnp.where(c <= r, 1.0, 0.0).astype(BF16)
    local = sum(_dot(tri, part) for part in _split3(log_f))

    @pl.when(i % nsb == 0)
    def _():
        carry_ref[...] = jnp.zeros_like(carry_ref)

    cum = local + carry_ref[0:1, :]
    carry_ref[0:1, :] = cum[tm - 1:tm, :]
    cum_ref[...] = cum * LOG2E


def _odd_proj(xf, g, w, bf, seq, tm):
    t, d = xf.shape
    nsb = seq // tm
    n = w.shape[1]
    row = lambda width: pl.BlockSpec((tm, width), lambda i: (i, 0))
    return pl.pallas_call(
        functools.partial(_odd_proj_kernel, tm=tm, nsb=nsb),
        grid=(t // tm,),
        in_specs=[row(d), pl.BlockSpec((1, d), lambda i: (0, 0)),
                  pl.BlockSpec((d, n), lambda i: (0, 0)), pl.BlockSpec((1, LANES), lambda i: (0, 0))],
        out_specs=[row(FOX_W)] * 3 + [row(LANES)],
        out_shape=[jax.ShapeDtypeStruct((t, FOX_W), BF16)] * 3 + [jax.ShapeDtypeStruct((t, LANES), F32)],
        scratch_shapes=[pltpu.VMEM((8, LANES), F32)],
        compiler_params=_params(("arbitrary",)),
        name="odd_proj",
    )(xf, g, w, bf)


def _fox_kernel(q_ref, k_ref, v_ref, c_ref, o_ref, *, tq, tk):
    i = pl.program_id(2)
    q0 = pl.multiple_of(i * tq, tq)
    n_full = q0 // tk
    lane = lax.broadcasted_iota(jnp.int32, (1, LANES), 1)
    q = q_ref[...]
    outs = []
    for h in range(2):
        hmask = (lane < HEAD_DIM) if h == 0 else (lane >= HEAD_DIM)
        qh = jnp.where(hmask, q, jnp.zeros_like(q))
        c0 = c_ref[0, 0, h:h + 1, pl.ds(q0, tq)][:, 0:1]

        def step(jt, carry, masked, qh=qh, c0=c0, h=h):
            m, l, acc = carry
            k0 = pl.multiple_of(jt * tk, tk)
            s = _dot_nt(qh, k_ref[pl.ds(k0, tk), :]) + (c0 - c_ref[0, 0, h:h + 1, pl.ds(k0, tk)])
            if masked:
                kpos = k0 + lax.broadcasted_iota(jnp.int32, (tq, tk), 1)
                tpos = q0 + lax.broadcasted_iota(jnp.int32, (tq, tk), 0)
                s = jnp.where(kpos <= tpos, s, -jnp.inf)
            m_new = jnp.maximum(m, jnp.max(s, axis=-1, keepdims=True))
            alpha = jnp.exp2(m - m_new)
            p = jnp.exp2(s - m_new)
            l = alpha * l + jnp.sum(p, axis=-1, keepdims=True)
            acc = alpha * acc + _dot(p.astype(BF16), v_ref[pl.ds(k0, tk), :])
            return m_new, l, acc

        init = (jnp.full((tq, 1), -jnp.inf, F32), jnp.zeros((tq, 1), F32), jnp.zeros((tq, LANES), F32))
        carry = lax.fori_loop(0, n_full, lambda jt, cr: step(jt, cr, False), init)
        _, l, acc = step(n_full, carry, True)
        outs.append(acc / l)
    o_ref[...] = jnp.where(lane < HEAD_DIM, outs[0], outs[1]).astype(BF16)


def _fox_attention(q, k, v, cum_t, batch, seq, tq, tk):
    t, w = q.shape
    nq = seq // tq
    kv = pl.BlockSpec((seq, LANES), lambda b, p, i: (b, p))
    qo = pl.BlockSpec((tq, LANES), lambda b, p, i: (b * nq + i, p))
    return pl.pallas_call(
        functools.partial(_fox_kernel, tq=tq, tk=tk),
        grid=(batch, w // LANES, nq),
        in_specs=[qo, kv, kv, pl.BlockSpec((1, 1, 2, seq), lambda b, p, i: (b, p, 0, 0))],
        out_specs=qo,
        out_shape=jax.ShapeDtypeStruct((t, w), BF16),
        compiler_params=_params(("parallel", "parallel", "arbitrary")),
        name="fox_attention",
    )(q, k, v, cum_t)


def _out_proj_kernel(*refs, n_in):
    x_ref, o_refs, w_ref, y_ref = refs[0], refs[1:1 + n_in], refs[1 + n_in], refs[2 + n_in]
    y = x_ref[...]
    off = 0
    for o_ref in o_refs:
        width = o_ref.shape[1]
        y = y + _dot(o_ref[...], w_ref[off:off + width, :])
        off += width
    y_ref[...] = y


def _out_proj(xf, outs, w, tm):
    t, d = xf.shape
    row = lambda width: pl.BlockSpec((tm, width), lambda i: (i, 0))
    return pl.pallas_call(
        functools.partial(_out_proj_kernel, n_in=len(outs)),
        grid=(t // tm,),
        in_specs=[row(d)] + [row(o.shape[1]) for o in outs] + [pl.BlockSpec(w.shape, lambda i: (0, 0))],
        out_specs=row(d),
        out_shape=jax.ShapeDtypeStruct((t, d), F32),
        compiler_params=_params(("parallel",)),
        name="out_proj",
    )(xf, *outs, w)


FFN_CHUNK = 256
HALO = 8


def _ffn_kernel(*refs, tm, nsb, final):
    if final:
        x_ref, g_ref, win_ref, cw_ref, cb_ref, wout_ref, fn_ref, o_ref, a_scr = refs
    else:
        x_ref, g_ref, win_ref, cw_ref, cb_ref, wout_ref, o_ref, a_scr = refs
    i = pl.program_id(0)

    @pl.when(i % nsb == 0)
    def _():
        a_scr[0:HALO, :] = jnp.zeros((HALO, D_FF), F32)

    @pl.when(i % nsb != 0)
    def _():
        a_scr[0:HALO, :] = a_scr[tm:tm + HALO, :]

    x = x_ref[...]
    hb = _rmsnorm(x, g_ref[...]).astype(BF16)
    acc = jnp.zeros((tm, D_MODEL), F32)
    for c in range(D_FF // FFN_CHUNK):
        sl = slice(c * FFN_CHUNK, (c + 1) * FFN_CHUNK)
        a = _dot(hb, win_ref[:, sl])
        b = _dot(hb, win_ref[:, D_FF + c * FFN_CHUNK:D_FF + (c + 1) * FFN_CHUNK])
        a_scr[HALO:HALO + tm, sl] = a
        conv = (cw_ref[0:1, sl] * a_scr[HALO - 2:HALO - 2 + tm, sl]
                + cw_ref[1:2, sl] * a_scr[HALO - 1:HALO - 1 + tm, sl]
                + cw_ref[2:3, sl] * a + cb_ref[:, sl])
        gated = conv * (1.0 / (1.0 + jnp.exp(-conv))) * b
        acc = acc + _dot(gated.astype(BF16), wout_ref[sl, :])
    y = x + acc
    if final:
        y = _rmsnorm(y, fn_ref[...])
    o_ref[...] = y


def _ffn(xf, g, w_in, conv_w, conv_b, w_out, final_norm, seq, tm):
    t, d = xf.shape
    nsb = seq // tm
    row = pl.BlockSpec((tm, d), lambda i: (i, 0))
    const = lambda shape: pl.BlockSpec(shape, lambda i: (0, 0), pipeline_mode=pl.Buffered(1))
    small = lambda shape: pl.BlockSpec(shape, lambda i: (0, 0))
    final = final_norm is not None
    in_specs = [row, small((1, d)), const(w_in.shape), small(conv_w.shape), small((1, D_FF)), const(w_out.shape)]
    args = [xf, g, w_in, conv_w, conv_b, w_out]
    if final:
        in_specs.append(small((1, d)))
        args.append(final_norm)
    return pl.pallas_call(
        functools.partial(_ffn_kernel, tm=tm, nsb=nsb, final=final),
        grid=(t // tm,),
        in_specs=in_specs,
        out_specs=row,
        out_shape=jax.ShapeDtypeStruct((t, d), F32),
        scratch_shapes=[pltpu.VMEM((tm + HALO, D_FF), F32)],
        compiler_params=_params(("arbitrary",)),
        name="conv_glu_ffn",
    )(*args)


def kernel(x, attn_norm, ffn_norm, ev_w_in, ev_cmp_pos_k, ev_cmp_pos_v, ev_cmp_w_k, ev_cmp_w_v, ev_w_out,
           od_w_in, od_b_f, od_w_out, ffn_w_in, ffn_conv_w, ffn_conv_b, ffn_w_out, final_norm):
    batch, seq, d = x.shape
    t = batch * seq
    depth = attn_norm.shape[0]
    tm = min(512, seq)
    sb_tile = min(256, seq)
    fox_tq, fox_tk = min(256, seq), min(512, seq)
    sel_tk = min(512, seq)

    xf = x.reshape(t, d)
    tabs = _rope_tables(jnp.arange(seq))
    ctabs = _rope_tables(jnp.arange(seq // CMP_STRIDE) * CMP_STRIDE + (CMP_LEN - 1))
    mt, oh = _nsa_consts(seq)

    for layer in range(depth):
        g_attn = attn_norm[layer].reshape(1, d)
        if layer % 2 == 0:
            e = layer // 2
            (sbq, sbk, sbv, nq, kc_raw, vc_raw, ksl, vsl, kwn, vwn, gates) = _even_proj(
                xf, g_attn, _prep_even_w(ev_w_in[e]), tabs, seq, tm)
            pek, wk = _prep_cmp(ev_cmp_pos_k[e], ev_cmp_w_k[e])
            pev, wv = _prep_cmp(ev_cmp_pos_v[e], ev_cmp_w_v[e])
            kc, vc = _compress(kc_raw, vc_raw, pek, pev, wk, wv, ctabs, batch, seq)
            o_sb = _sb_attention(sbq, sbk, sbv, batch, seq, sb_tile)
            o_nsa = _nsa_attention(nq, kc, vc, ksl, vsl, kwn, vwn, gates, mt, oh, batch, seq, sel_tk)
            xf = _out_proj(xf, [o_sb, o_nsa], ev_w_out[e].astype(BF16), tm)
        else:
            o = layer // 2
            w = jnp.pad(od_w_in[o], ((0, 0), (0, LANES - FOX_HEADS))).astype(BF16)
            bf = jnp.pad(od_b_f[o], (0, LANES - FOX_HEADS)).reshape(1, LANES)
            q, k, v, cum = _odd_proj(xf, g_attn, w, bf, seq, tm)
            cum_t = jnp.transpose(cum[:, :FOX_HEADS].reshape(batch, seq, FOX_HEADS // 2, 2), (0, 2, 3, 1))
            o_fox = _fox_attention(q, k, v, cum_t, batch, seq, fox_tq, fox_tk)
            xf = _out_proj(xf, [o_fox], od_w_out[o].astype(BF16), tm)
        last = layer == depth - 1
        xf = _ffn(xf, ffn_norm[layer].reshape(1, d), ffn_w_in[layer].astype(BF16), ffn_conv_w[layer],
                  ffn_conv_b[layer].reshape(1, D_FF), ffn_w_out[layer].astype(BF16),
                  final_norm.reshape(1, d) if last else None, seq, tm)
    return xf.reshape(batch, seq, d)
```

```python
import functools
import math

import jax
import jax.numpy as jnp
from jax import lax
from jax.experimental import pallas as pl
from jax.experimental.pallas import tpu as pltpu

F32, BF16 = jnp.float32, jnp.bfloat16

D_MODEL = 1024
HEAD_DIM = 64
N_HEADS = D_MODEL // HEAD_DIM
SB_HEADS = N_HEADS // 2
NSA_HEADS = N_HEADS - SB_HEADS
NSA_KV_GROUPS = 2
NSA_REP = NSA_HEADS // NSA_KV_GROUPS
FOX_HEADS = N_HEADS
CMP_LEN = 32
CMP_STRIDE = 16
SEL_LEN = 64
SEL_TOP = 16
WINDOW = 512
N_BRANCH = 3
ROPE_THETA = 500000.0
ROT_DIM = HEAD_DIM // 4
D_FF = 2816
CONV_WIDTH = 3
NORM_EPS = 1e-6
SB_W = SB_HEADS * HEAD_DIM
NSA_QW = NSA_HEADS * HEAD_DIM
NSA_KVW = NSA_KV_GROUPS * HEAD_DIM
FOX_W = FOX_HEADS * HEAD_DIM

LANES = 128
SEL_LANES = 128
QK_SCALE = HEAD_DIM ** -0.5
LOG2E = math.log2(math.e)
SEL_NEG = -(2.0 ** 30)
VMEM_LIMIT = 56 * 2 ** 20

_NT = (((1,), (1,)), ((), ()))


def _params(sem):
    return pltpu.CompilerParams(dimension_semantics=sem, vmem_limit_bytes=VMEM_LIMIT)


def _dot(a, b):
    return jnp.dot(a, b, preferred_element_type=F32)


def _dot_nt(a, b):
    return lax.dot_general(a, b, _NT, preferred_element_type=F32)


def _rmsnorm(x, g):
    ms = jnp.mean(x * x, axis=-1, keepdims=True)
    return (x * lax.rsqrt(ms + NORM_EPS)) * g


def _rope(y, c, s1, s2):
    return y * c + pltpu.roll(y, LANES - ROT_DIM // 2, 1) * s1 + pltpu.roll(y, ROT_DIM // 2, 1) * s2


def _split3(x):
    hi = x.astype(BF16)
    r1 = x - hi.astype(F32)
    mid = r1.astype(BF16)
    lo = (r1 - mid.astype(F32)).astype(BF16)
    return hi, mid, lo


def _rope_tables(pos):
    half = ROT_DIM // 2
    inv_freq = ROPE_THETA ** (-(jnp.arange(half, dtype=F32) * 2.0 / ROT_DIM))
    ang = pos.astype(F32)[:, None] * inv_freq[None, :]
    cos, sin = jnp.cos(ang), jnp.sin(ang)
    n = pos.shape[0]
    one = jnp.ones((n, HEAD_DIM - ROT_DIM), F32)
    zero = jnp.zeros((n, HEAD_DIM - ROT_DIM), F32)
    z8 = jnp.zeros((n, half), F32)
    c = jnp.concatenate([cos, cos, one], -1)
    s1 = jnp.concatenate([-sin, z8, zero], -1)
    s2 = jnp.concatenate([z8, sin, zero], -1)
    two = lambda t: jnp.concatenate([t, t], -1)
    return two(c), two(s1), two(s2)


def _even_proj_kernel(x_ref, g_ref, w_ref, c_ref, s1_ref, s2_ref,
                      sbq_ref, sbk_ref, sbv_ref, nq_ref, kc_ref, vc_ref,
                      ksl_ref, vsl_ref, kwn_ref, vwn_ref, gate_ref):
    hb = _rmsnorm(x_ref[...], g_ref[...]).astype(BF16)
    c, s1, s2 = c_ref[...], s1_ref[...], s2_ref[...]

    def seg(a, n):
        return _dot(hb, w_ref[:, a:a + n])

    sbq_ref[...] = (seg(0, SB_W) * QK_SCALE).astype(BF16)
    sbk_ref[...] = seg(SB_W, SB_W).astype(BF16)
    sbv_ref[...] = seg(2 * SB_W, SB_W).astype(BF16)
    base = 3 * SB_W
    for j in range(NSA_QW // LANES):
        y = _rope(seg(base + j * LANES, LANES), c, s1, s2)
        nq_ref[:, j * LANES:(j + 1) * LANES] = (y * QK_SCALE).astype(BF16)
    base += NSA_QW
    kc_ref[...] = seg(base, LANES)
    vc_ref[...] = seg(base + LANES, LANES)
    ksl_ref[...] = _rope(seg(base + 2 * LANES, LANES), c, s1, s2).astype(BF16)
    vsl_ref[...] = seg(base + 3 * LANES, LANES).astype(BF16)
    kwn_ref[...] = _rope(seg(base + 4 * LANES, LANES), c, s1, s2).astype(BF16)
    vwn_ref[...] = seg(base + 5 * LANES, LANES).astype(BF16)
    gl = seg(base + 6 * LANES, 2 * LANES)
    gate_ref[...] = 1.0 / (1.0 + jnp.exp(-gl))


def _even_proj(xf, g, w, tabs, seq, tm):
    t, d = xf.shape
    nsb = seq // tm
    n = w.shape[1]
    row = lambda width: pl.BlockSpec((tm, width), lambda i: (i, 0))
    tab = pl.BlockSpec((tm, LANES), lambda i: (i % nsb, 0))
    out_shape = (
        [jax.ShapeDtypeStruct((t, SB_W), BF16)] * 3
        + [jax.ShapeDtypeStruct((t, NSA_QW), BF16)]
        + [jax.ShapeDtypeStruct((t, LANES), F32)] * 2
        + [jax.ShapeDtypeStruct((t, LANES), BF16)] * 4
        + [jax.ShapeDtypeStruct((t, 2 * LANES), F32)]
    )
    out_specs = [row(SB_W)] * 3 + [row(NSA_QW)] + [row(LANES)] * 6 + [row(2 * LANES)]
    return pl.pallas_call(
        _even_proj_kernel,
        grid=(t // tm,),
        in_specs=[row(d), pl.BlockSpec((1, d), lambda i: (0, 0)),
                  pl.BlockSpec((d, n), lambda i: (0, 0)), tab, tab, tab],
        out_specs=out_specs,
        out_shape=out_shape,
        compiler_params=_params(("parallel",)),
        name="even_proj",
    )(xf, g, w, *tabs)


def _prep_even_w(w):
    d = w.shape[0]
    main = w[:, :3 * SB_W + NSA_QW + 6 * NSA_KVW]
    gates = w[:, 3 * SB_W + NSA_QW + 6 * NSA_KVW:].reshape(d, NSA_KV_GROUPS, NSA_REP, N_BRANCH)
    gates = jnp.transpose(gates, (0, 1, 3, 2)).reshape(d, NSA_KV_GROUPS, N_BRANCH * NSA_REP)
    gates = jnp.pad(gates, ((0, 0), (0, 0), (0, LANES - N_BRANCH * NSA_REP)))
    return jnp.concatenate([main, gates.reshape(d, NSA_KV_GROUPS * LANES)], axis=1).astype(BF16)


SB_DEAD = -105.0


def _sb_kernel(q_ref, k_ref, v_ref, o_ref, *, tile):
    i = pl.program_id(2)
    lane = lax.broadcasted_iota(jnp.int32, (1, LANES), 1)
    r = lax.broadcasted_iota(jnp.int32, (tile, tile), 0)
    c = lax.broadcasted_iota(jnp.int32, (tile, tile), 1)
    later = jnp.where(r > c, 1.0, 0.0).astype(BF16)
    diag = c < r
    q = q_ref[...]
    qh = [jnp.where(lane < HEAD_DIM, q, jnp.zeros_like(q)), jnp.where(lane >= HEAD_DIM, q, jnp.zeros_like(q))]

    def step(jt, state, masked):
        k0 = pl.multiple_of(jt * tile, tile)
        kt = k_ref[pl.ds(k0, tile), :]
        vt = v_ref[pl.ds(k0, tile), :]
        out = []
        for h in range(2):
            acc, keep_sum = state[h]
            z = _dot_nt(qh[h], kt)
            ls = jnp.minimum(z, 0.0) - jnp.log1p(jnp.exp(-jnp.abs(z)))
            lk = ls - z
            if masked:
                lk = jnp.where(diag, lk, 0.0)
            hi = lk.astype(BF16)
            lo = (lk - hi.astype(F32)).astype(BF16)
            after = _dot(hi, later) + _dot(lo, later)
            a = jnp.exp(ls + after + keep_sum)
            if masked:
                a = jnp.where(diag, a, 0.0)
            acc = acc + _dot(a.astype(BF16), vt)
            keep_sum = keep_sum + after[:, 0:1] + lk[:, 0:1]
            out.append((acc, keep_sum))
        return tuple(out)

    def alive(state):
        return jnp.maximum(jnp.max(state[0][1]), jnp.max(state[1][1])) >= SB_DEAD

    zero = (jnp.zeros((tile, LANES), F32), jnp.zeros((tile, 1), F32))
    state = step(i, (zero, zero), True)

    def body(carry):
        jt, _, state = carry
        state = step(jt, state, False)
        return jt - 1, alive(state), state

    _, _, state = lax.while_loop(lambda cr: (cr[0] >= 0) & cr[1], body, (i - 1, alive(state), state))
    o_ref[...] = jnp.where(lane < HEAD_DIM, state[0][0], state[1][0]).astype(BF16)


def _sb_attention(q, k, v, batch, seq, tile):
    t, w = q.shape
    nq = seq // tile
    kv = pl.BlockSpec((seq, LANES), lambda b, p, i: (b, p))
    qo = pl.BlockSpec((tile, LANES), lambda b, p, i: (b * nq + i, p))
    return pl.pallas_call(
        functools.partial(_sb_kernel, tile=tile),
        grid=(batch, w // LANES, nq),
        in_specs=[qo, kv, kv],
        out_specs=qo,
        out_shape=jax.ShapeDtypeStruct((t, w), BF16),
        compiler_params=_params(("parallel", "parallel", "arbitrary")),
        name="sb_attention",
    )(q, k, v)


def _compress_kernel(xk_ref, xv_ref, pek_ref, pev_ref, wk_ref, wv_ref, c_ref, s1_ref, s2_ref,
                     kc_ref, vc_ref, *, nc):
    def comp(x_ref, pe_ref, w_ref):
        x = x_ref[0]
        top = _dot((x + pe_ref[0:1, :]).astype(BF16), w_ref[0])
        bot = _dot((x + pe_ref[1:2, :]).astype(BF16), w_ref[1])
        return top + pltpu.roll(bot, nc - 1, 0)

    kc = _rope(comp(xk_ref, pek_ref, wk_ref), c_ref[...], s1_ref[...], s2_ref[...])
    kc_ref[0] = kc.astype(BF16)
    vc_ref[0] = comp(xv_ref, pev_ref, wv_ref).astype(BF16)


def _prep_cmp(pe, w):
    half = CMP_LEN // 2
    w3 = w.reshape(CMP_LEN, HEAD_DIM, HEAD_DIM)
    eye = jnp.eye(NSA_KV_GROUPS, dtype=w.dtype)
    parts, pes = [], []
    for s in range(2):
        wh = jnp.einsum('ldo,gh->lgdho', w3[s * half:(s + 1) * half], eye)
        parts.append(wh.reshape(half * NSA_KVW, NSA_KVW))
        pes.append(jnp.broadcast_to(pe[s * half:(s + 1) * half, None, :],
                                    (half, NSA_KV_GROUPS, HEAD_DIM)).reshape(1, half * NSA_KVW))
    return jnp.concatenate(pes, 0), jnp.stack(parts).astype(BF16)


def _compress(kc_raw, vc_raw, pek, pev, wk, wv, ctabs, batch, seq):
    nc = seq // CMP_STRIDE
    cw = CMP_STRIDE * NSA_KVW
    xk = kc_raw.reshape(batch, nc, cw)
    xv = vc_raw.reshape(batch, nc, cw)
    xs = pl.BlockSpec((1, nc, cw), lambda b: (b, 0, 0))
    pes = pl.BlockSpec((2, cw), lambda b: (0, 0))
    ws = pl.BlockSpec((2, cw, NSA_KVW), lambda b: (0, 0, 0))
    tab = pl.BlockSpec((nc, LANES), lambda b: (0, 0))
    out = pl.BlockSpec((1, nc, LANES), lambda b: (b, 0, 0))
    return pl.pallas_call(
        functools.partial(_compress_kernel, nc=nc),
        grid=(batch,),
        in_specs=[xs, xs, pes, pes, ws, ws, tab, tab, tab],
        out_specs=[out, out],
        out_shape=[jax.ShapeDtypeStruct((batch, nc, LANES), BF16)] * 2,
        compiler_params=_params(("parallel",)),
        name="nsa_compress",
    )(xk, xv, pek, pev, wk, wv, *ctabs)


QB = 128


def _softmax_rows(s):
    m = jnp.max(s, axis=-1, keepdims=True)
    m = jnp.where(m > -jnp.inf, m, 0.0)
    e = jnp.exp(s - m)
    d = jnp.sum(e, axis=-1, keepdims=True)
    return e / jnp.where(d > 0, d, 1.0)


def _nsa_kernel(q_ref, kc_ref, vc_ref, ks_ref, vs_ref, kw_ref, vw_ref, g_ref, mt_ref, oh_ref,
                o_ref, score_ref, *, nc, tk, wlen, sel_rows):
    g = pl.program_id(1)
    i = pl.program_id(2)
    q0 = i * QB
    rows = NSA_REP * QB

    q2 = q_ref[...]
    pr = lax.broadcasted_iota(jnp.int32, (NSA_REP * HEAD_DIM, LANES), 0)
    pc = lax.broadcasted_iota(jnp.int32, (NSA_REP * HEAD_DIM, LANES), 1) - g * HEAD_DIM
    in_group = (pc >= 0) & (pc < HEAD_DIM)
    q4 = jnp.concatenate(
        [_dot(q2, jnp.where(in_group & (pr - hh * HEAD_DIM == pc), 1.0, 0.0).astype(BF16)).astype(BF16)
         for hh in range(NSA_REP)], axis=0)

    sc = _dot_nt(q4, kc_ref[0])
    cend = lax.broadcasted_iota(jnp.int32, (rows, nc), 1) * CMP_STRIDE + (CMP_LEN - 1)
    tpos_c = q0 + (lax.broadcasted_iota(jnp.int32, (rows, nc), 0) & (QB - 1))
    p_cmp = _softmax_rows(jnp.where(cend <= tpos_c, sc, -jnp.inf))
    o_cmp = _dot(p_cmp.astype(BF16), vc_ref[0])

    p_sum = p_cmp[0:QB] + p_cmp[QB:2 * QB] + p_cmp[2 * QB:3 * QB] + p_cmp[3 * QB:4 * QB]
    mt = mt_ref[...]
    imp_t = sum(_dot_nt(mt, part) for part in _split3(p_sum))
    nidx = lax.broadcasted_iota(jnp.int32, (SEL_LANES, QB), 0)
    cur = (q0 + lax.broadcasted_iota(jnp.int32, (SEL_LANES, QB), 1)) // SEL_LEN
    causal = nidx <= cur
    forced = (nidx == 0) | (nidx == cur) | (nidx == cur - 1)
    score = jnp.where(causal, jnp.where(forced, jnp.inf, imp_t), -jnp.inf)
    score_ref[...] = score

    def rank_step(m, cnt):
        rowm = score_ref[pl.ds(m, 1), :]
        ge = jnp.where(rowm >= score, 1.0, 0.0)
        gt = jnp.where(rowm > score, 1.0, 0.0)
        return cnt + jnp.where(nidx > m, ge, gt)

    n_causal = (q0 + QB - 1) // SEL_LEN + 1
    cnt = lax.fori_loop(0, n_causal, rank_step, jnp.zeros((SEL_LANES, QB), F32))
    bias_t = jnp.where(causal & (cnt < float(SEL_TOP)), 0.0, SEL_NEG)
    selb = jnp.transpose(bias_t).astype(BF16)
    qs = jnp.concatenate([q4, jnp.concatenate([selb] * NSA_REP, axis=0)], axis=1)

    lane = lax.broadcasted_iota(jnp.int32, (1, LANES), 1)
    own_lanes = (lane >= g * HEAD_DIM) & (lane < (g + 1) * HEAD_DIM)
    n_chain = rows // sel_rows
    q_chain = [qs[r * sel_rows:(r + 1) * sel_rows] for r in range(n_chain)]

    def sel_step(jt, state, masked):
        k0 = pl.multiple_of(jt * tk, tk)
        kk = jnp.concatenate([ks_ref[pl.ds(k0, tk), :], oh_ref[pl.ds(k0, tk), :]], axis=1)
        vt = vs_ref[pl.ds(k0, tk), :]
        vt = jnp.where(own_lanes, vt, jnp.ones_like(vt))
        out = []
        for r in range(n_chain):
            m, acc = state[r]
            s = _dot_nt(q_chain[r], kk)
            if masked:
                kpos = k0 + lax.broadcasted_iota(jnp.int32, (sel_rows, tk), 1)
                tpos = q0 + (lax.broadcasted_iota(jnp.int32, (sel_rows, tk), 0) & (QB - 1))
                s = jnp.where(kpos <= tpos, s, -jnp.inf)
            m_new = jnp.maximum(m, jnp.max(s, axis=-1, keepdims=True))
            p = jnp.exp(s - m_new)
            acc = jnp.exp(m - m_new) * acc + _dot(p.astype(BF16), vt)
            out.append((m_new, acc))
        return tuple(out)

    n_full = q0 // tk
    init = tuple((jnp.full((sel_rows, 1), -jnp.inf, F32), jnp.zeros((sel_rows, LANES), F32))
                 for _ in range(n_chain))
    state = lax.fori_loop(0, n_full, lambda jt, st: sel_step(jt, st, False), init)
    state = sel_step(n_full, state, True)
    acc_s = jnp.concatenate([st[1] for st in state], axis=0)
    l_s = jnp.where(g == 0, acc_s[:, HEAD_DIM:HEAD_DIM + 1], acc_s[:, 0:1])
    o_slc = acc_s / l_s

    w0 = pl.multiple_of(jnp.maximum(q0 - WINDOW, 0), QB)
    sw = _dot_nt(q4, kw_ref[pl.ds(w0, wlen), :])
    kpos = w0 + lax.broadcasted_iota(jnp.int32, (rows, wlen), 1)
    tpos = q0 + (lax.broadcasted_iota(jnp.int32, (rows, wlen), 0) & (QB - 1))
    p_win = _softmax_rows(jnp.where((kpos <= tpos) & (kpos > tpos - WINDOW), sw, -jnp.inf))
    o_win = _dot(p_win.astype(BF16), vw_ref[pl.ds(w0, wlen), :])

    gates = g_ref[...]
    orow = lax.broadcasted_iota(jnp.int32, (LANES, NSA_REP * HEAD_DIM), 0) - g * HEAD_DIM
    ocol = lax.broadcasted_iota(jnp.int32, (LANES, NSA_REP * HEAD_DIM), 1)
    o_group = (orow >= 0) & (orow < HEAD_DIM)
    out = jnp.zeros((QB, NSA_REP * HEAD_DIM), F32)
    for hh in range(NSA_REP):
        sl = slice(hh * QB, (hh + 1) * QB)
        gate = lambda br: gates[:, br * NSA_REP + hh:br * NSA_REP + hh + 1]
        o_h = gate(0) * o_cmp[sl] + gate(1) * o_slc[sl] + gate(2) * o_win[sl]
        place = jnp.where(o_group & (ocol - hh * HEAD_DIM == orow), 1.0, 0.0).astype(BF16)
        out = out + _dot(o_h.astype(BF16), place)
    o_ref[...] = out.astype(BF16)


def _nsa_consts(seq):
    nc = seq // CMP_STRIDE
    c_start = jnp.arange(nc) * CMP_STRIDE
    sel_start = jnp.arange(SEL_LANES) * SEL_LEN
    real = (jnp.arange(SEL_LANES) < seq // SEL_LEN)[:, None] & (jnp.arange(nc) < (seq - CMP_LEN) // CMP_STRIDE + 1)[None, :]
    overlap = (c_start[None, :] < sel_start[:, None] + SEL_LEN) & (c_start[None, :] + CMP_LEN > sel_start[:, None])
    mt = (overlap & real).astype(BF16)
    oh = (jnp.arange(seq)[:, None] // SEL_LEN == jnp.arange(SEL_LANES)[None, :]).astype(BF16)
    return mt, oh


def _nsa_attention(nq, kc, vc, ksl, vsl, kwn, vwn, gates, mt, oh, batch, seq, tk, sel_rows):
    t = nq.shape[0]
    nqb = seq // QB
    nc = seq // CMP_STRIDE
    wlen = WINDOW + QB
    gw = NSA_REP * HEAD_DIM
    qspec = pl.BlockSpec((QB, gw), lambda b, g, i: (b * nqb + i, g))
    cspec = pl.BlockSpec((1, nc, LANES), lambda b, g, i: (b, 0, 0))
    kvspec = pl.BlockSpec((seq, LANES), lambda b, g, i: (b, 0))
    gspec = pl.BlockSpec((QB, LANES), lambda b, g, i: (b * nqb + i, g))
    return pl.pallas_call(
        functools.partial(_nsa_kernel, nc=nc, tk=tk, wlen=wlen, sel_rows=sel_rows),
        grid=(batch, NSA_KV_GROUPS, nqb),
        in_specs=[qspec, cspec, cspec, kvspec, kvspec, kvspec, kvspec, gspec,
                  pl.BlockSpec((SEL_LANES, nc), lambda b, g, i: (0, 0)),
                  pl.BlockSpec((seq, SEL_LANES), lambda b, g, i: (0, 0))],
        out_specs=qspec,
        out_shape=jax.ShapeDtypeStruct((t, NSA_QW), BF16),
        scratch_shapes=[pltpu.VMEM((SEL_LANES, QB), F32)],
        compiler_params=_params(("parallel", "parallel", "arbitrary")),
        name="nsa_attention",
    )(nq, kc, vc, ksl, vsl, kwn, vwn, gates, mt, oh)


N_CUM_PARTS = 3


def _odd_proj_kernel(x_ref, g_ref, w_ref, bf_ref, place_ref, q_ref, k_ref, v_ref, carry_ref, *, tm, nsb):
    i = pl.program_id(0)
    hb = _rmsnorm(x_ref[...], g_ref[...]).astype(BF16)
    f = _dot(hb, w_ref[:, 3 * FOX_W:3 * FOX_W + LANES]) + bf_ref[...]
    log_f = jnp.minimum(f, 0.0) - jnp.log1p(jnp.exp(-jnp.abs(f)))
    r = lax.broadcasted_iota(jnp.int32, (tm, tm), 0)
    c = lax.broadcasted_iota(jnp.int32, (tm, tm), 1)
    tri = jnp.where(c <= r, 1.0, 0.0).astype(BF16)
    local = sum(_dot(tri, part) for part in _split3(log_f))

    @pl.when(i % nsb == 0)
    def _():
        carry_ref[...] = jnp.zeros_like(carry_ref)

    cum = local + carry_ref[0:1, :]
    carry_ref[0:1, :] = cum[tm - 1:tm, :]
    neg_parts = _split3(cum * (-LOG2E))

    lane = lax.broadcasted_iota(jnp.int32, (1, LANES), 1)
    low = lane < HEAD_DIM
    q_one = jnp.where((lane >= HEAD_DIM) & (lane < HEAD_DIM + N_CUM_PARTS), 1.0, 0.0)
    v_one = jnp.where(lane >= HEAD_DIM, 1.0, 0.0)
    for p in range(FOX_HEADS // 2):
        yq = _dot(hb, w_ref[:, p * LANES:(p + 1) * LANES]) * (QK_SCALE * LOG2E)
        yk = _dot(hb, w_ref[:, FOX_W + p * LANES:FOX_W + (p + 1) * LANES])
        yv = _dot(hb, w_ref[:, 2 * FOX_W + p * LANES:2 * FOX_W + (p + 1) * LANES])
        extras = sum(_dot(part, place_ref[n, :, 2 * p * LANES:(2 * p + 2) * LANES])
                     for n, part in enumerate(neg_parts))
        for h in range(2):
            hs = slice((2 * p + h) * LANES, (2 * p + h + 1) * LANES)
            head = (lambda y: y) if h == 0 else (lambda y: pltpu.roll(y, HEAD_DIM, 1))
            q_ref[:, hs] = (jnp.where(low, head(yq), 0.0) + q_one).astype(BF16)
            k_ref[:, hs] = (jnp.where(low, head(yk), 0.0) + extras[:, h * LANES:(h + 1) * LANES]).astype(BF16)
            v_ref[:, hs] = (jnp.where(low, head(yv), 0.0) + v_one).astype(BF16)


def _cum_placement():
    src = jnp.arange(LANES)[None, :, None]
    dst = jnp.arange(FOX_HEADS * LANES)[None, None, :]
    n = jnp.arange(N_CUM_PARTS)[:, None, None]
    return ((src < FOX_HEADS) & (dst == src * LANES + HEAD_DIM + n)).astype(BF16)


def _odd_proj(xf, g, w, bf, seq, tm):
    t, d = xf.shape
    nsb = seq // tm
    n = w.shape[1]
    wide = FOX_HEADS * LANES
    row = lambda width: pl.BlockSpec((tm, width), lambda i: (i, 0))
    return pl.pallas_call(
        functools.partial(_odd_proj_kernel, tm=tm, nsb=nsb),
        grid=(t // tm,),
        in_specs=[row(d), pl.BlockSpec((1, d), lambda i: (0, 0)),
                  pl.BlockSpec((d, n), lambda i: (0, 0)), pl.BlockSpec((1, LANES), lambda i: (0, 0)),
                  pl.BlockSpec((N_CUM_PARTS, LANES, wide), lambda i: (0, 0, 0))],
        out_specs=[row(wide)] * 3,
        out_shape=[jax.ShapeDtypeStruct((t, wide), BF16)] * 3,
        scratch_shapes=[pltpu.VMEM((8, LANES), F32)],
        compiler_params=_params(("arbitrary",)),
        name="odd_proj",
    )(xf, g, w, bf, _cum_placement())


FOX_DEAD = -160.0
NORM_SLACK = 1.001


def _fox_kernel(q_ref, k_ref, v_ref, o_ref, knorm_ref, *, tq, tk, sub, seq):
    i = pl.program_id(2)
    q0 = pl.multiple_of(i * tq, tq)
    n_full = q0 // tk
    nsub = tq // sub
    chains = [(h, r) for h in range(2) for r in range(nsub)]
    qs = [q_ref[r * sub:(r + 1) * sub, h * LANES:(h + 1) * LANES] for h, r in chains]
    lane = lax.broadcasted_iota(jnp.int32, (1, LANES), 1)
    feat = lane < HEAD_DIM
    cum_lanes = (lane >= HEAD_DIM) & (lane < HEAD_DIM + N_CUM_PARTS)

    def max_row_norm(x):
        xf = jnp.where(feat, x.astype(F32), 0.0)
        return jnp.sqrt(jnp.max(jnp.sum(xf * xf, axis=-1, keepdims=True)))

    @pl.when(i == 0)
    def _():
        for h in range(2):
            def chunk(n, best, h=h):
                rows = k_ref[pl.ds(pl.multiple_of(n * tk, tk), tk), h * LANES:(h + 1) * LANES]
                return jnp.maximum(best, max_row_norm(rows))
            knorm_ref[h] = lax.fori_loop(0, seq // tk, chunk, jnp.float32(0.0))

    qk_bound = [max_row_norm(q_ref[:, h * LANES:(h + 1) * LANES]) * knorm_ref[h] * NORM_SLACK for h in range(2)]

    def step(jt, state, masked):
        k0 = pl.multiple_of(jt * tk, tk)
        out = []
        for (h, r), q, (m, acc) in zip(chains, qs, state):
            hs = slice(h * LANES, (h + 1) * LANES)
            s = _dot_nt(q, k_ref[pl.ds(k0, tk), hs])
            if masked:
                kpos = k0 + lax.broadcasted_iota(jnp.int32, (sub, tk), 1)
                tpos = q0 + r * sub + lax.broadcasted_iota(jnp.int32, (sub, tk), 0)
                s = jnp.where(kpos <= tpos, s, -jnp.inf)
            m_new = jnp.maximum(m, jnp.max(s, axis=-1, keepdims=True))
            p = jnp.exp2(s - m_new)
            acc = jnp.exp2(m - m_new) * acc + _dot(p.astype(BF16), v_ref[pl.ds(k0, tk), hs])
            out.append((m_new, acc))
        return tuple(out)

    def alive(jt, state):
        k0 = pl.multiple_of(jt * tk, tk)
        go = None
        for h in range(2):
            first = k_ref[pl.ds(k0, 16), h * LANES:(h + 1) * LANES][0:1, :].astype(F32)
            neg_cum = jnp.sum(jnp.where(cum_lanes, first, 0.0))
            m_min = functools.reduce(jnp.minimum, [jnp.min(state[h * nsub + r][0]) for r in range(nsub)])
            live = qk_bound[h] + neg_cum - m_min >= FOX_DEAD
            go = live if go is None else (go | live)
        return go

    init = tuple((jnp.full((sub, 1), -jnp.inf, F32), jnp.zeros((sub, LANES), F32)) for _ in chains)
    state = step(n_full, init, True)

    def body(carry):
        jt, _, state = carry
        state = step(jt, state, False)
        return jt - 1, alive(jt, state), state

    _, _, state = lax.while_loop(lambda cr: (cr[0] >= 0) & cr[1], body, (n_full - 1, alive(n_full, state), state))
    heads = []
    for h in range(2):
        acc = jnp.concatenate([state[h * nsub + r][1] for r in range(nsub)], axis=0)
        heads.append(acc * (1.0 / acc[:, HEAD_DIM:HEAD_DIM + 1]))
    o_ref[...] = jnp.where(feat, heads[0], pltpu.roll(heads[1], HEAD_DIM, 1)).astype(BF16)


def _fox_attention(q, k, v, batch, seq, tq, tk, sub):
    t = q.shape[0]
    nq = seq // tq
    kv = pl.BlockSpec((seq, 2 * LANES), lambda b, p, i: (b, p))
    return pl.pallas_call(
        functools.partial(_fox_kernel, tq=tq, tk=tk, sub=sub, seq=seq),
        grid=(batch, FOX_HEADS // 2, nq),
        in_specs=[pl.BlockSpec((tq, 2 * LANES), lambda b, p, i: (b * nq + i, p)), kv, kv],
        out_specs=pl.BlockSpec((tq, LANES), lambda b, p, i: (b * nq + i, p)),
        out_shape=jax.ShapeDtypeStruct((t, FOX_W), BF16),
        scratch_shapes=[pltpu.SMEM((2,), F32)],
        compiler_params=_params(("parallel", "parallel", "arbitrary")),
        name="fox_attention",
    )(q, k, v)


def _out_proj_kernel(*refs, n_in):
    x_ref, o_refs, w_ref, y_ref = refs[0], refs[1:1 + n_in], refs[1 + n_in], refs[2 + n_in]
    y = x_ref[...]
    off = 0
    for o_ref in o_refs:
        width = o_ref.shape[1]
        y = y + _dot(o_ref[...], w_ref[off:off + width, :])
        off += width
    y_ref[...] = y


def _out_proj(xf, outs, w, tm):
    t, d = xf.shape
    row = lambda width: pl.BlockSpec((tm, width), lambda i: (i, 0))
    return pl.pallas_call(
        functools.partial(_out_proj_kernel, n_in=len(outs)),
        grid=(t // tm,),
        in_specs=[row(d)] + [row(o.shape[1]) for o in outs] + [pl.BlockSpec(w.shape, lambda i: (0, 0))],
        out_specs=row(d),
        out_shape=jax.ShapeDtypeStruct((t, d), F32),
        compiler_params=_params(("parallel",)),
        name="out_proj",
    )(xf, *outs, w)


FFN_CHUNK = 256
HALO = 8


def _ffn_kernel(*refs, tm, nsb, final):
    if final:
        x_ref, g_ref, win_ref, cw_ref, cb_ref, wout_ref, fn_ref, o_ref, a_scr = refs
    else:
        x_ref, g_ref, win_ref, cw_ref, cb_ref, wout_ref, o_ref, a_scr = refs
    i = pl.program_id(0)

    @pl.when(i % nsb == 0)
    def _():
        a_scr[0:HALO, :] = jnp.zeros((HALO, D_FF), F32)

    @pl.when(i % nsb != 0)
    def _():
        a_scr[0:HALO, :] = a_scr[tm:tm + HALO, :]

    x = x_ref[...]
    hb = _rmsnorm(x, g_ref[...]).astype(BF16)
    acc = jnp.zeros((tm, D_MODEL), F32)
    for c in range(D_FF // FFN_CHUNK):
        sl = slice(c * FFN_CHUNK, (c + 1) * FFN_CHUNK)
        a = _dot(hb, win_ref[:, sl])
        b = _dot(hb, win_ref[:, D_FF + c * FFN_CHUNK:D_FF + (c + 1) * FFN_CHUNK])
        a_scr[HALO:HALO + tm, sl] = a
        conv = (cw_ref[0:1, sl] * a_scr[HALO - 2:HALO - 2 + tm, sl]
                + cw_ref[1:2, sl] * a_scr[HALO - 1:HALO - 1 + tm, sl]
                + cw_ref[2:3, sl] * a + cb_ref[:, sl])
        gated = conv * (1.0 / (1.0 + jnp.exp(-conv))) * b
        acc = acc + _dot(gated.astype(BF16), wout_ref[sl, :])
    y = x + acc
    if final:
        y = _rmsnorm(y, fn_ref[...])
    o_ref[...] = y


def _ffn(xf, g, w_in, conv_w, conv_b, w_out, final_norm, seq, tm):
    t, d = xf.shape
    nsb = seq // tm
    row = pl.BlockSpec((tm, d), lambda i: (i, 0))
    const = lambda shape: pl.BlockSpec(shape, lambda i: (0, 0), pipeline_mode=pl.Buffered(1))
    small = lambda shape: pl.BlockSpec(shape, lambda i: (0, 0))
    final = final_norm is not None
    in_specs = [row, small((1, d)), const(w_in.shape), small(conv_w.shape), small((1, D_FF)), const(w_out.shape)]
    args = [xf, g, w_in, conv_w, conv_b, w_out]
    if final:
        in_specs.append(small((1, d)))
        args.append(final_norm)
    return pl.pallas_call(
        functools.partial(_ffn_kernel, tm=tm, nsb=nsb, final=final),
        grid=(t // tm,),
        in_specs=in_specs,
        out_specs=row,
        out_shape=jax.ShapeDtypeStruct((t, d), F32),
        scratch_shapes=[pltpu.VMEM((tm + HALO, D_FF), F32)],
        compiler_params=_params(("arbitrary",)),
        name="conv_glu_ffn",
    )(*args)


def kernel(x, attn_norm, ffn_norm, ev_w_in, ev_cmp_pos_k, ev_cmp_pos_v, ev_cmp_w_k, ev_cmp_w_v, ev_w_out,
           od_w_in, od_b_f, od_w_out, ffn_w_in, ffn_conv_w, ffn_conv_b, ffn_w_out, final_norm):
    batch, seq, d = x.shape
    t = batch * seq
    depth = attn_norm.shape[0]
    tm = min(512, seq)
    sb_tile = min(256, seq)
    fox_tq, fox_tk, fox_sub = min(512, seq), min(512, seq), 256
    sel_tk, sel_rows = min(512, seq), NSA_REP * QB

    xf = x.reshape(t, d)
    tabs = _rope_tables(jnp.arange(seq))
    ctabs = _rope_tables(jnp.arange(seq // CMP_STRIDE) * CMP_STRIDE + (CMP_LEN - 1))
    mt, oh = _nsa_consts(seq)

    for layer in range(depth):
        g_attn = attn_norm[layer].reshape(1, d)
        if layer % 2 == 0:
            e = layer // 2
            (sbq, sbk, sbv, nq, kc_raw, vc_raw, ksl, vsl, kwn, vwn, gates) = _even_proj(
                xf, g_attn, _prep_even_w(ev_w_in[e]), tabs, seq, tm)
            pek, wk = _prep_cmp(ev_cmp_pos_k[e], ev_cmp_w_k[e])
            pev, wv = _prep_cmp(ev_cmp_pos_v[e], ev_cmp_w_v[e])
            kc, vc = _compress(kc_raw, vc_raw, pek, pev, wk, wv, ctabs, batch, seq)
            o_sb = _sb_attention(sbq, sbk, sbv, batch, seq, sb_tile)
            o_nsa = _nsa_attention(nq, kc, vc, ksl, vsl, kwn, vwn, gates, mt, oh, batch, seq, sel_tk, sel_rows)
            xf = _out_proj(xf, [o_sb, o_nsa], ev_w_out[e].astype(BF16), tm)
        else:
            o = layer // 2
            w = jnp.pad(od_w_in[o], ((0, 0), (0, LANES - FOX_HEADS))).astype(BF16)
            bf = jnp.pad(od_b_f[o], (0, LANES - FOX_HEADS)).reshape(1, LANES)
            q, k, v = _odd_proj(xf, g_attn, w, bf, seq, tm)
            o_fox = _fox_attention(q, k, v, batch, seq, fox_tq, fox_tk, fox_sub)
            xf = _out_proj(xf, [o_fox], od_w_out[o].astype(BF16), tm)
        last = layer == depth - 1
        xf = _ffn(xf, ffn_norm[layer].reshape(1, d), ffn_w_in[layer].astype(BF16), ffn_conv_w[layer],
                  ffn_conv_b[layer].reshape(1, D_FF), ffn_w_out[layer].astype(BF16),
                  final_norm.reshape(1, d) if last else None, seq, tm)
    return xf.reshape(batch, seq, d)
```

```python
import functools
import math

import jax
import jax.numpy as jnp
from jax import lax
from jax.experimental import pallas as pl
from jax.experimental.pallas import tpu as pltpu

F32, BF16 = jnp.float32, jnp.bfloat16

D_MODEL = 1024
HEAD_DIM = 64
N_HEADS = D_MODEL // HEAD_DIM
SB_HEADS = N_HEADS // 2
NSA_HEADS = N_HEADS - SB_HEADS
NSA_KV_GROUPS = 2
NSA_REP = NSA_HEADS // NSA_KV_GROUPS
FOX_HEADS = N_HEADS
CMP_LEN = 32
CMP_STRIDE = 16
SEL_LEN = 64
SEL_TOP = 16
WINDOW = 512
N_BRANCH = 3
ROPE_THETA = 500000.0
ROT_DIM = HEAD_DIM // 4
D_FF = 2816
CONV_WIDTH = 3
NORM_EPS = 1e-6
SB_W = SB_HEADS * HEAD_DIM
NSA_QW = NSA_HEADS * HEAD_DIM
NSA_KVW = NSA_KV_GROUPS * HEAD_DIM
FOX_W = FOX_HEADS * HEAD_DIM

LANES = 128
SEL_LANES = 128
QK_SCALE = HEAD_DIM ** -0.5
LOG2E = math.log2(math.e)
SEL_NEG = -(2.0 ** 30)
NSA_TIGHT = 60.0
NORM_SLACK = 1.001
VMEM_LIMIT = 56 * 2 ** 20

_NT = (((1,), (1,)), ((), ()))


def _params(sem):
    return pltpu.CompilerParams(dimension_semantics=sem, vmem_limit_bytes=VMEM_LIMIT)


def _dot(a, b):
    return jnp.dot(a, b, preferred_element_type=F32)


def _dot_nt(a, b):
    return lax.dot_general(a, b, _NT, preferred_element_type=F32)


def _rmsnorm(x, g):
    ms = jnp.mean(x * x, axis=-1, keepdims=True)
    return (x * lax.rsqrt(ms + NORM_EPS)) * g


def _rope(y, c, s1, s2):
    return y * c + pltpu.roll(y, LANES - ROT_DIM // 2, 1) * s1 + pltpu.roll(y, ROT_DIM // 2, 1) * s2


def _split3(x):
    hi = x.astype(BF16)
    r1 = x - hi.astype(F32)
    mid = r1.astype(BF16)
    lo = (r1 - mid.astype(F32)).astype(BF16)
    return hi, mid, lo


def _rope_tables(pos):
    half = ROT_DIM // 2
    inv_freq = ROPE_THETA ** (-(jnp.arange(half, dtype=F32) * 2.0 / ROT_DIM))
    ang = pos.astype(F32)[:, None] * inv_freq[None, :]
    cos, sin = jnp.cos(ang), jnp.sin(ang)
    n = pos.shape[0]
    one = jnp.ones((n, HEAD_DIM - ROT_DIM), F32)
    zero = jnp.zeros((n, HEAD_DIM - ROT_DIM), F32)
    z8 = jnp.zeros((n, half), F32)
    c = jnp.concatenate([cos, cos, one], -1)
    s1 = jnp.concatenate([-sin, z8, zero], -1)
    s2 = jnp.concatenate([z8, sin, zero], -1)
    two = lambda t: jnp.concatenate([t, t], -1)
    return two(c), two(s1), two(s2)


def _even_proj_kernel(x_ref, g_ref, w_ref, c_ref, s1_ref, s2_ref,
                      sbq_ref, sbk_ref, sbv_ref, nq_ref, kc_ref, vc_ref,
                      ksl_ref, vsl_ref, kwn_ref, vwn_ref, gate_ref):
    hb = _rmsnorm(x_ref[...], g_ref[...]).astype(BF16)
    c, s1, s2 = c_ref[...], s1_ref[...], s2_ref[...]

    def seg(a, n):
        return _dot(hb, w_ref[:, a:a + n])

    sbq_ref[...] = (seg(0, SB_W) * QK_SCALE).astype(BF16)
    sbk_ref[...] = seg(SB_W, SB_W).astype(BF16)
    sbv_ref[...] = seg(2 * SB_W, SB_W).astype(BF16)
    base = 3 * SB_W
    mxu_n = 2 * LANES
    lo, hi = slice(0, LANES), slice(LANES, mxu_n)
    for j in range(NSA_QW // mxu_n):
        y2 = seg(base + j * mxu_n, mxu_n)
        for half in (lo, hi):
            y = _rope(y2[:, half], c, s1, s2)
            nq_ref[:, j * mxu_n + half.start:j * mxu_n + half.stop] = (y * QK_SCALE).astype(BF16)
    base += NSA_QW
    cmp2 = seg(base, mxu_n)
    kc_ref[...] = cmp2[:, lo]
    vc_ref[...] = cmp2[:, hi]
    slc2 = seg(base + mxu_n, mxu_n)
    ksl_ref[...] = _rope(slc2[:, lo], c, s1, s2).astype(BF16)
    vsl_ref[...] = slc2[:, hi].astype(BF16)
    win2 = seg(base + 2 * mxu_n, mxu_n)
    kwn_ref[...] = _rope(win2[:, lo], c, s1, s2).astype(BF16)
    vwn_ref[...] = win2[:, hi].astype(BF16)
    gl = seg(base + 6 * LANES, 2 * LANES)
    gate_ref[...] = 1.0 / (1.0 + jnp.exp(-gl))


def _even_proj(xf, g, w, tabs, seq, tm):
    t, d = xf.shape
    nsb = seq // tm
    n = w.shape[1]
    row = lambda width: pl.BlockSpec((tm, width), lambda i: (i, 0))
    tab = pl.BlockSpec((tm, LANES), lambda i: (i % nsb, 0))
    out_shape = (
        [jax.ShapeDtypeStruct((t, SB_W), BF16)] * 3
        + [jax.ShapeDtypeStruct((t, NSA_QW), BF16)]
        + [jax.ShapeDtypeStruct((t, LANES), F32)] * 2
        + [jax.ShapeDtypeStruct((t, LANES), BF16)] * 4
        + [jax.ShapeDtypeStruct((t, 2 * LANES), F32)]
    )
    out_specs = [row(SB_W)] * 3 + [row(NSA_QW)] + [row(LANES)] * 6 + [row(2 * LANES)]
    return pl.pallas_call(
        _even_proj_kernel,
        grid=(t // tm,),
        in_specs=[row(d), pl.BlockSpec((1, d), lambda i: (0, 0)),
                  pl.BlockSpec((d, n), lambda i: (0, 0)), tab, tab, tab],
        out_specs=out_specs,
        out_shape=out_shape,
        compiler_params=_params(("parallel",)),
        name="even_proj",
    )(xf, g, w, *tabs)


def _prep_even_w(w):
    d = w.shape[0]
    main = w[:, :3 * SB_W + NSA_QW + 6 * NSA_KVW]
    gates = w[:, 3 * SB_W + NSA_QW + 6 * NSA_KVW:].reshape(d, NSA_KV_GROUPS, NSA_REP, N_BRANCH)
    gates = jnp.transpose(gates, (0, 1, 3, 2)).reshape(d, NSA_KV_GROUPS, N_BRANCH * NSA_REP)
    gates = jnp.pad(gates, ((0, 0), (0, 0), (0, LANES - N_BRANCH * NSA_REP)))
    return jnp.concatenate([main, gates.reshape(d, NSA_KV_GROUPS * LANES)], axis=1).astype(BF16)


SB_DEAD = -105.0


def _sb_kernel(q_ref, k_ref, v_ref, o_ref, *, tile, nsub):
    i = pl.program_id(2)
    lane = lax.broadcasted_iota(jnp.int32, (1, LANES), 1)
    r = lax.broadcasted_iota(jnp.int32, (tile, tile), 0)
    c = lax.broadcasted_iota(jnp.int32, (tile, tile), 1)
    later = jnp.where(r > c, 1.0, 0.0).astype(BF16)
    diag = c < r
    chains = [(h, sub) for h in range(2) for sub in range(nsub)]
    qs = []
    for h, sub in chains:
        q = q_ref[sub * tile:(sub + 1) * tile, :]
        qs.append(jnp.where((lane < HEAD_DIM) if h == 0 else (lane >= HEAD_DIM), q, jnp.zeros_like(q)))

    def step(jt, state, mode):
        k0 = pl.multiple_of(jt * tile, tile)
        kt = k_ref[pl.ds(k0, tile), :]
        vt = v_ref[pl.ds(k0, tile), :]
        out = []
        for c, (acc, keep_sum) in enumerate(state):
            if mode[c] is None:
                out.append((acc, keep_sum))
                continue
            z = _dot_nt(qs[c], kt)
            ls = jnp.minimum(z, 0.0) - jnp.log1p(jnp.exp(-jnp.abs(z)))
            lk = ls - z
            if mode[c]:
                lk = jnp.where(diag, lk, 0.0)
            hi = lk.astype(BF16)
            lo = (lk - hi.astype(F32)).astype(BF16)
            after = _dot(hi, later) + _dot(lo, later)
            a = jnp.exp(ls + after + keep_sum)
            if mode[c]:
                a = jnp.where(diag, a, 0.0)
            out.append((acc + _dot(a.astype(BF16), vt), keep_sum + after[:, 0:1] + lk[:, 0:1]))
        return tuple(out)

    def alive(state):
        return functools.reduce(jnp.maximum, [jnp.max(keep_sum) for _, keep_sum in state]) >= SB_DEAD

    state = tuple((jnp.zeros((tile, LANES), F32), jnp.zeros((tile, 1), F32)) for _ in chains)
    for top in range(nsub - 1, -1, -1):
        mode = [None if sub < top else sub == top for _, sub in chains]
        state = step(nsub * i + top, state, mode)

    def body(carry):
        jt, _, state = carry
        state = step(jt, state, [False] * len(chains))
        return jt - 1, alive(state), state

    _, _, state = lax.while_loop(lambda cr: (cr[0] >= 0) & cr[1], body, (nsub * i - 1, alive(state), state))
    heads = [jnp.concatenate([state[h * nsub + sub][0] for sub in range(nsub)], axis=0) for h in range(2)]
    o_ref[...] = jnp.where(lane < HEAD_DIM, heads[0], heads[1]).astype(BF16)


def _sb_attention(q, k, v, batch, seq, tile, nsub):
    t, w = q.shape
    nq = seq // (tile * nsub)
    kv = pl.BlockSpec((seq, LANES), lambda b, p, i: (b, p))
    qo = pl.BlockSpec((tile * nsub, LANES), lambda b, p, i: (b * nq + i, p))
    return pl.pallas_call(
        functools.partial(_sb_kernel, tile=tile, nsub=nsub),
        grid=(batch, w // LANES, nq),
        in_specs=[qo, kv, kv],
        out_specs=qo,
        out_shape=jax.ShapeDtypeStruct((t, w), BF16),
        compiler_params=_params(("parallel", "parallel", "arbitrary")),
        name="sb_attention",
    )(q, k, v)


def _compress_kernel(xk_ref, xv_ref, pek_ref, pev_ref, wk_ref, wv_ref, c_ref, s1_ref, s2_ref,
                     kc_ref, vc_ref, *, nc):
    def comp(x_ref, pe_ref, w_ref):
        x = x_ref[0]
        top = _dot((x + pe_ref[0:1, :]).astype(BF16), w_ref[0])
        bot = _dot((x + pe_ref[1:2, :]).astype(BF16), w_ref[1])
        return top + pltpu.roll(bot, nc - 1, 0)

    kc = _rope(comp(xk_ref, pek_ref, wk_ref), c_ref[...], s1_ref[...], s2_ref[...])
    kc_ref[0] = kc.astype(BF16)
    vc_ref[0] = comp(xv_ref, pev_ref, wv_ref).astype(BF16)


def _prep_cmp(pe, w):
    half = CMP_LEN // 2
    w3 = w.reshape(CMP_LEN, HEAD_DIM, HEAD_DIM)
    eye = jnp.eye(NSA_KV_GROUPS, dtype=w.dtype)
    parts, pes = [], []
    for s in range(2):
        wh = jnp.einsum('ldo,gh->lgdho', w3[s * half:(s + 1) * half], eye)
        parts.append(wh.reshape(half * NSA_KVW, NSA_KVW))
        pes.append(jnp.broadcast_to(pe[s * half:(s + 1) * half, None, :],
                                    (half, NSA_KV_GROUPS, HEAD_DIM)).reshape(1, half * NSA_KVW))
    return jnp.concatenate(pes, 0), jnp.stack(parts).astype(BF16)


def _compress(kc_raw, vc_raw, pek, pev, wk, wv, ctabs, batch, seq):
    nc = seq // CMP_STRIDE
    cw = CMP_STRIDE * NSA_KVW
    xk = kc_raw.reshape(batch, nc, cw)
    xv = vc_raw.reshape(batch, nc, cw)
    xs = pl.BlockSpec((1, nc, cw), lambda b: (b, 0, 0))
    pes = pl.BlockSpec((2, cw), lambda b: (0, 0))
    ws = pl.BlockSpec((2, cw, NSA_KVW), lambda b: (0, 0, 0))
    tab = pl.BlockSpec((nc, LANES), lambda b: (0, 0))
    out = pl.BlockSpec((1, nc, LANES), lambda b: (b, 0, 0))
    return pl.pallas_call(
        functools.partial(_compress_kernel, nc=nc),
        grid=(batch,),
        in_specs=[xs, xs, pes, pes, ws, ws, tab, tab, tab],
        out_specs=[out, out],
        out_shape=[jax.ShapeDtypeStruct((batch, nc, LANES), BF16)] * 2,
        compiler_params=_params(("parallel",)),
        name="nsa_compress",
    )(xk, xv, pek, pev, wk, wv, *ctabs)


QB = 128


def _softmax_rows(s):
    m = jnp.max(s, axis=-1, keepdims=True)
    m = jnp.where(m > -jnp.inf, m, 0.0)
    e = jnp.exp(s - m)
    d = jnp.sum(e, axis=-1, keepdims=True)
    return e / jnp.where(d > 0, d, 1.0)


def _nsa_kernel(q_ref, kc_ref, vc_ref, ks_ref, vs_ref, kw_ref, vw_ref, g_ref, mt_ref, oh_ref,
                o_ref, score_ref, knorm_ref, *, nc, tk, wlen, seq):
    g = pl.program_id(1)
    i = pl.program_id(2)
    q0 = i * QB
    rows = NSA_REP * QB

    q2 = q_ref[...]
    pr = lax.broadcasted_iota(jnp.int32, (NSA_REP * HEAD_DIM, LANES), 0)
    pc = lax.broadcasted_iota(jnp.int32, (NSA_REP * HEAD_DIM, LANES), 1) - g * HEAD_DIM
    in_group = (pc >= 0) & (pc < HEAD_DIM)
    q4 = jnp.concatenate(
        [_dot(q2, jnp.where(in_group & (pr - hh * HEAD_DIM == pc), 1.0, 0.0).astype(BF16)).astype(BF16)
         for hh in range(NSA_REP)], axis=0)

    sc = _dot_nt(q4, kc_ref[0])
    cend = lax.broadcasted_iota(jnp.int32, (rows, nc), 1) * CMP_STRIDE + (CMP_LEN - 1)
    tpos_c = q0 + (lax.broadcasted_iota(jnp.int32, (rows, nc), 0) & (QB - 1))
    p_cmp = _softmax_rows(jnp.where(cend <= tpos_c, sc, -jnp.inf))
    o_cmp = _dot(p_cmp.astype(BF16), vc_ref[0])

    p_sum = p_cmp[0:QB] + p_cmp[QB:2 * QB] + p_cmp[2 * QB:3 * QB] + p_cmp[3 * QB:4 * QB]
    mt = mt_ref[...]
    imp_t = sum(_dot_nt(mt, part) for part in _split3(p_sum))
    nidx = lax.broadcasted_iota(jnp.int32, (SEL_LANES, QB), 0)
    cur = (q0 + lax.broadcasted_iota(jnp.int32, (SEL_LANES, QB), 1)) // SEL_LEN
    causal = nidx <= cur
    forced = (nidx == 0) | (nidx == cur) | (nidx == cur - 1)
    score = jnp.where(causal, jnp.where(forced, jnp.inf, imp_t), -jnp.inf)
    score_ref[...] = score

    def larger_step(m, cnt):
        return cnt + jnp.where(score_ref[pl.ds(m, 1), :] > score, 1.0, 0.0)

    def rank_step(m, cnt):
        rowm = score_ref[pl.ds(m, 1), :]
        ge = jnp.where(rowm >= score, 1.0, 0.0)
        gt = jnp.where(rowm > score, 1.0, 0.0)
        return cnt + jnp.where(nidx > m, ge, gt)

    n_causal = (q0 + QB - 1) // SEL_LEN + 1
    zero_cnt = jnp.zeros((SEL_LANES, QB), F32)
    cnt = lax.fori_loop(0, n_causal, larger_step, zero_cnt)
    n_top = jnp.sum(jnp.where(causal & (cnt < float(SEL_TOP)), 1.0, 0.0), axis=0, keepdims=True)
    cnt = lax.cond(jnp.max(n_top) > float(SEL_TOP),
                   lambda c: lax.fori_loop(0, n_causal, rank_step, zero_cnt), lambda c: c, cnt)
    bias_t = jnp.where(causal & (cnt < float(SEL_TOP)), 0.0, SEL_NEG)
    selb4 = jnp.concatenate([jnp.transpose(bias_t)] * NSA_REP, axis=0)

    lane = lax.broadcasted_iota(jnp.int32, (1, LANES), 1)
    own_lanes = (lane >= g * HEAD_DIM) & (lane < (g + 1) * HEAD_DIM)

    @pl.when(i == 0)
    def _():
        def chunk(n, best):
            kf = ks_ref[pl.ds(pl.multiple_of(n * tk, tk), tk), :].astype(F32)
            kf = jnp.where(own_lanes, kf, 0.0)
            return jnp.maximum(best, jnp.max(jnp.sum(kf * kf, axis=-1, keepdims=True)))
        knorm_ref[0] = jnp.sqrt(lax.fori_loop(0, seq // tk, chunk, jnp.float32(0.0)))

    q4f = q4.astype(F32)
    shift = jnp.sqrt(jnp.sum(q4f * q4f, axis=-1, keepdims=True)) * (knorm_ref[0] * NORM_SLACK)
    shift = (shift * (1.0 + 2.0 ** -7)).astype(BF16).astype(F32)
    tight = 2.0 * jnp.max(shift) <= NSA_TIGHT
    qs = jnp.concatenate([q4, jnp.where(tight, selb4 - shift, selb4).astype(BF16)], axis=1)
    n_full = q0 // tk

    def sel_tiles(jt):
        k0 = pl.multiple_of(jt * tk, tk)
        kk = jnp.concatenate([ks_ref[pl.ds(k0, tk), :], oh_ref[pl.ds(k0, tk), :]], axis=1)
        vt = vs_ref[pl.ds(k0, tk), :]
        return kk, jnp.where(own_lanes, vt, jnp.ones_like(vt))

    def scores(jt, masked):
        kk, vt = sel_tiles(jt)
        s = _dot_nt(qs, kk)
        if masked:
            kpos = jt * tk + lax.broadcasted_iota(jnp.int32, (rows, tk), 1)
            tpos = q0 + (lax.broadcasted_iota(jnp.int32, (rows, tk), 0) & (QB - 1))
            s = jnp.where(kpos <= tpos, s, -jnp.inf)
        return s, vt

    def fixed_shift_sweep(acc):
        def step(jt, acc, masked):
            s, vt = scores(jt, masked)
            return acc + _dot(jnp.exp(s).astype(BF16), vt)
        acc = lax.fori_loop(0, n_full, lambda jt, a: step(jt, a, False), acc)
        return step(n_full, acc, True)

    def running_max_sweep(acc):
        def step(jt, carry, masked):
            m, acc = carry
            s, vt = scores(jt, masked)
            m_new = jnp.maximum(m, jnp.max(s, axis=-1, keepdims=True))
            return m_new, jnp.exp(m - m_new) * acc + _dot(jnp.exp(s - m_new).astype(BF16), vt)
        carry = lax.fori_loop(0, n_full, lambda jt, cr: step(jt, cr, False),
                              (jnp.full((rows, 1), -jnp.inf, F32), acc))
        return step(n_full, carry, True)[1]

    acc_s = lax.cond(tight, fixed_shift_sweep, running_max_sweep, jnp.zeros((rows, LANES), F32))
    l_s = jnp.where(g == 0, acc_s[:, HEAD_DIM:HEAD_DIM + 1], acc_s[:, 0:1])
    o_slc = acc_s / l_s

    w0 = pl.multiple_of(jnp.maximum(q0 - WINDOW, 0), QB)
    sw = _dot_nt(q4, kw_ref[pl.ds(w0, wlen), :])
    kpos = w0 + lax.broadcasted_iota(jnp.int32, (rows, wlen), 1)
    tpos = q0 + (lax.broadcasted_iota(jnp.int32, (rows, wlen), 0) & (QB - 1))
    p_win = _softmax_rows(jnp.where((kpos <= tpos) & (kpos > tpos - WINDOW), sw, -jnp.inf))
    o_win = _dot(p_win.astype(BF16), vw_ref[pl.ds(w0, wlen), :])

    gates = g_ref[...]
    orow = lax.broadcasted_iota(jnp.int32, (LANES, NSA_REP * HEAD_DIM), 0) - g * HEAD_DIM
    ocol = lax.broadcasted_iota(jnp.int32, (LANES, NSA_REP * HEAD_DIM), 1)
    o_group = (orow >= 0) & (orow < HEAD_DIM)
    out = jnp.zeros((QB, NSA_REP * HEAD_DIM), F32)
    for hh in range(NSA_REP):
        sl = slice(hh * QB, (hh + 1) * QB)
        gate = lambda br: gates[:, br * NSA_REP + hh:br * NSA_REP + hh + 1]
        o_h = gate(0) * o_cmp[sl] + gate(1) * o_slc[sl] + gate(2) * o_win[sl]
        place = jnp.where(o_group & (ocol - hh * HEAD_DIM == orow), 1.0, 0.0).astype(BF16)
        out = out + _dot(o_h.astype(BF16), place)
    o_ref[...] = out.astype(BF16)


def _nsa_consts(seq):
    nc = seq // CMP_STRIDE
    c_start = jnp.arange(nc) * CMP_STRIDE
    sel_start = jnp.arange(SEL_LANES) * SEL_LEN
    real = (jnp.arange(SEL_LANES) < seq // SEL_LEN)[:, None] & (jnp.arange(nc) < (seq - CMP_LEN) // CMP_STRIDE + 1)[None, :]
    overlap = (c_start[None, :] < sel_start[:, None] + SEL_LEN) & (c_start[None, :] + CMP_LEN > sel_start[:, None])
    mt = (overlap & real).astype(BF16)
    oh = (jnp.arange(seq)[:, None] // SEL_LEN == jnp.arange(SEL_LANES)[None, :]).astype(BF16)
    return mt, oh


def _nsa_attention(nq, kc, vc, ksl, vsl, kwn, vwn, gates, mt, oh, batch, seq, tk):
    t = nq.shape[0]
    nqb = seq // QB
    nc = seq // CMP_STRIDE
    wlen = WINDOW + QB
    gw = NSA_REP * HEAD_DIM
    qspec = pl.BlockSpec((QB, gw), lambda b, g, i: (b * nqb + i, g))
    cspec = pl.BlockSpec((1, nc, LANES), lambda b, g, i: (b, 0, 0))
    kvspec = pl.BlockSpec((seq, LANES), lambda b, g, i: (b, 0))
    gspec = pl.BlockSpec((QB, LANES), lambda b, g, i: (b * nqb + i, g))
    return pl.pallas_call(
        functools.partial(_nsa_kernel, nc=nc, tk=tk, wlen=wlen, seq=seq),
        grid=(batch, NSA_KV_GROUPS, nqb),
        in_specs=[qspec, cspec, cspec, kvspec, kvspec, kvspec, kvspec, gspec,
                  pl.BlockSpec((SEL_LANES, nc), lambda b, g, i: (0, 0)),
                  pl.BlockSpec((seq, SEL_LANES), lambda b, g, i: (0, 0))],
        out_specs=qspec,
        out_shape=jax.ShapeDtypeStruct((t, NSA_QW), BF16),
        scratch_shapes=[pltpu.VMEM((SEL_LANES, QB), F32), pltpu.SMEM((1,), F32)],
        compiler_params=_params(("parallel", "parallel", "arbitrary")),
        name="nsa_attention",
    )(nq, kc, vc, ksl, vsl, kwn, vwn, gates, mt, oh)


N_CUM_PARTS = 3


def _odd_proj_kernel(x_ref, g_ref, w_ref, bf_ref, place_ref, q_ref, k_ref, v_ref, carry_ref, *, tm, nsb):
    i = pl.program_id(0)
    hb = _rmsnorm(x_ref[...], g_ref[...]).astype(BF16)
    f = _dot(hb, w_ref[:, 3 * FOX_W:3 * FOX_W + LANES]) + bf_ref[...]
    log_f = jnp.minimum(f, 0.0) - jnp.log1p(jnp.exp(-jnp.abs(f)))
    r = lax.broadcasted_iota(jnp.int32, (tm, tm), 0)
    c = lax.broadcasted_iota(jnp.int32, (tm, tm), 1)
    tri = jnp.where(c <= r, 1.0, 0.0).astype(BF16)
    local = sum(_dot(tri, part) for part in _split3(log_f))

    @pl.when(i % nsb == 0)
    def _():
        carry_ref[...] = jnp.zeros_like(carry_ref)

    cum = local + carry_ref[0:1, :]
    carry_ref[0:1, :] = cum[tm - 1:tm, :]
    lane = lax.broadcasted_iota(jnp.int32, (1, LANES), 1)
    hi, mid, lo = (part.astype(F32) for part in _split3(cum * (-LOG2E)))
    packed = jnp.where(lane < FOX_HEADS, hi,
                       jnp.where(lane < 2 * FOX_HEADS, pltpu.roll(mid, FOX_HEADS, 1),
                                 pltpu.roll(lo, 2 * FOX_HEADS, 1))).astype(BF16)
    low = lane < HEAD_DIM
    q_one = jnp.where((lane >= HEAD_DIM) & (lane < HEAD_DIM + N_CUM_PARTS), 1.0, 0.0)
    v_one = jnp.where(lane >= HEAD_DIM, 1.0, 0.0)
    k_one = jnp.where((lane >= HEAD_DIM + N_CUM_PARTS) & (lane < HEAD_DIM + N_CUM_PARTS + N_SHIFT_PARTS), 1.0, 0.0)
    mxu_n = 2 * LANES
    for p2 in range(FOX_W // mxu_n):
        cols = slice(p2 * mxu_n, (p2 + 1) * mxu_n)
        yq2 = _dot(hb, w_ref[:, cols]) * (QK_SCALE * LOG2E)
        yk2 = _dot(hb, w_ref[:, FOX_W + p2 * mxu_n:FOX_W + (p2 + 1) * mxu_n])
        yv2 = _dot(hb, w_ref[:, 2 * FOX_W + p2 * mxu_n:2 * FOX_W + (p2 + 1) * mxu_n])
        for pp in range(2):
            p = 2 * p2 + pp
            pair = slice(pp * LANES, (pp + 1) * LANES)
            yq, yk, yv = yq2[:, pair], yk2[:, pair], yv2[:, pair]
            extras = _dot(packed, place_ref[:, 2 * p * LANES:(2 * p + 2) * LANES])
            for h in range(2):
                hs = slice((2 * p + h) * LANES, (2 * p + h + 1) * LANES)
                head = (lambda y: y) if h == 0 else (lambda y: pltpu.roll(y, HEAD_DIM, 1))
                q_ref[:, hs] = (jnp.where(low, head(yq), 0.0) + q_one).astype(BF16)
                k_ref[:, hs] = (jnp.where(low, head(yk), 0.0) + extras[:, h * LANES:(h + 1) * LANES]
                                + k_one).astype(BF16)
                v_ref[:, hs] = (jnp.where(low, head(yv), 0.0) + v_one).astype(BF16)


def _cum_placement():
    src = jnp.arange(LANES)[:, None]
    dst = jnp.arange(FOX_HEADS * LANES)[None, :]
    n, h = src // FOX_HEADS, src % FOX_HEADS
    return ((n < N_CUM_PARTS) & (dst == h * LANES + HEAD_DIM + n)).astype(BF16)


def _odd_proj(xf, g, w, bf, seq, tm):
    t, d = xf.shape
    nsb = seq // tm
    n = w.shape[1]
    wide = FOX_HEADS * LANES
    row = lambda width: pl.BlockSpec((tm, width), lambda i: (i, 0))
    return pl.pallas_call(
        functools.partial(_odd_proj_kernel, tm=tm, nsb=nsb),
        grid=(t // tm,),
        in_specs=[row(d), pl.BlockSpec((1, d), lambda i: (0, 0)),
                  pl.BlockSpec((d, n), lambda i: (0, 0)), pl.BlockSpec((1, LANES), lambda i: (0, 0)),
                  pl.BlockSpec((LANES, wide), lambda i: (0, 0))],
        out_specs=[row(wide)] * 3,
        out_shape=[jax.ShapeDtypeStruct((t, wide), BF16)] * 3,
        scratch_shapes=[pltpu.VMEM((8, LANES), F32)],
        compiler_params=_params(("arbitrary",)),
        name="odd_proj",
    )(xf, g, w, bf, _cum_placement())


FOX_DEAD = -160.0
FOX_TIGHT = 64.0
N_SHIFT_PARTS = 2


def _fox_kernel(q_ref, k_ref, v_ref, o_ref, knorm_ref, ncum_ref, *, tq, tk, sub, seq):
    i = pl.program_id(2)
    q0 = pl.multiple_of(i * tq, tq)
    n_full = q0 // tk
    nsub = tq // sub
    chains = [(h, r) for h in range(2) for r in range(nsub)]
    qs = [q_ref[r * sub:(r + 1) * sub, h * LANES:(h + 1) * LANES] for h, r in chains]
    lane = lax.broadcasted_iota(jnp.int32, (1, LANES), 1)
    feat = lane < HEAD_DIM
    cum_lanes = (lane >= HEAD_DIM) & (lane < HEAD_DIM + N_CUM_PARTS)

    def row_norms(x):
        xf = jnp.where(feat, x.astype(F32), 0.0)
        return jnp.sqrt(jnp.sum(xf * xf, axis=-1, keepdims=True))

    @pl.when(i == 0)
    def _():
        for h in range(2):
            def chunk(n, best, h=h):
                rows = k_ref[pl.ds(pl.multiple_of(n * tk, tk), tk), h * LANES:(h + 1) * LANES]
                ncum_ref[h, n] = jnp.sum(jnp.where(cum_lanes, rows[0:1, :].astype(F32), 0.0))
                return jnp.maximum(best, jnp.max(row_norms(rows)))
            knorm_ref[h] = lax.fori_loop(0, seq // tk, chunk, jnp.float32(0.0))

    k_norm = [knorm_ref[h] * NORM_SLACK for h in range(2)]
    q_norm = [row_norms(q) for q in qs]
    qk_bound = [functools.reduce(jnp.maximum, [jnp.max(q_norm[h * nsub + r]) for r in range(nsub)]) * k_norm[h]
                for h in range(2)]
    tight = 2.0 * jnp.maximum(qk_bound[0], qk_bound[1]) <= FOX_TIGHT
    cum_q0 = [ncum_ref[h, n_full] for h in range(2)]

    def tiles(jt, h):
        k0 = pl.multiple_of(jt * tk, tk)
        hs = slice(h * LANES, (h + 1) * LANES)
        return k_ref[pl.ds(k0, tk), hs], v_ref[pl.ds(k0, tk), hs]

    shifts, accs, q_shifted = [], [], []
    for c, ((h, r), q) in enumerate(zip(chains, qs)):
        nk = (r + 1) * sub
        kt = k_ref[pl.ds(q0, nk), h * LANES:(h + 1) * LANES]
        vt = v_ref[pl.ds(q0, nk), h * LANES:(h + 1) * LANES]
        s = _dot_nt(q, kt)
        kpos = lax.broadcasted_iota(jnp.int32, (sub, nk), 1)
        tpos = r * sub + lax.broadcasted_iota(jnp.int32, (sub, nk), 0)
        s = jnp.where(kpos <= tpos, s, -jnp.inf)
        m_diag = jnp.max(s, axis=-1, keepdims=True)
        shift = jnp.where(tight, jnp.maximum(m_diag, q_norm[c] * k_norm[h] + cum_q0[h]), m_diag)
        hi = shift.astype(BF16)
        rest = shift - hi.astype(F32)
        lo = (rest + jnp.abs(rest) * (2.0 ** -7)).astype(BF16)
        shift = hi.astype(F32) + lo.astype(F32)
        accs.append(_dot(jnp.exp2(s - shift).astype(BF16), vt))
        shifts.append(shift)
        q_shifted.append(jnp.where(lane == HEAD_DIM + N_CUM_PARTS, -hi,
                                   jnp.where(lane == HEAD_DIM + N_CUM_PARTS + 1, -lo, q)))

    def may_matter(jt):
        return jnp.maximum(ncum_ref[0, jt + 1] - cum_q0[0], ncum_ref[1, jt + 1] - cum_q0[1]) >= FOX_DEAD

    def fixed_shift_sweep(accs):
        def body(carry):
            jt, accs = carry
            out = []
            for (h, r), q, acc in zip(chains, q_shifted, accs):
                kt, vt = tiles(jt, h)
                out.append(acc + _dot(jnp.exp2(_dot_nt(q, kt)).astype(BF16), vt))
            return jt - 1, tuple(out)

        return lax.while_loop(lambda cr: (cr[0] >= 0) & may_matter(jnp.maximum(cr[0], 0)), body,
                              (n_full - 1, accs))[1]

    def running_max_sweep(accs):
        def alive(jt, ms):
            go = None
            for h in range(2):
                m_min = functools.reduce(jnp.minimum, [jnp.min(ms[h * nsub + r]) for r in range(nsub)])
                live = qk_bound[h] + ncum_ref[h, jt + 1] - m_min >= FOX_DEAD
                go = live if go is None else (go | live)
            return go

        def body(carry):
            jt, _, ms, accs = carry
            new_m, new_acc = [], []
            for (h, r), q, m, acc in zip(chains, qs, ms, accs):
                kt, vt = tiles(jt, h)
                s = _dot_nt(q, kt)
                m_new = jnp.maximum(m, jnp.max(s, axis=-1, keepdims=True))
                new_acc.append(jnp.exp2(m - m_new) * acc + _dot(jnp.exp2(s - m_new).astype(BF16), vt))
                new_m.append(m_new)
            new_m = tuple(new_m)
            return jt - 1, alive(jnp.maximum(jt - 1, 0), new_m), new_m, tuple(new_acc)

        ms = tuple(shifts)
        return lax.while_loop(lambda cr: (cr[0] >= 0) & cr[1], body,
                              (n_full - 1, alive(jnp.maximum(n_full - 1, 0), ms), ms, accs))[3]

    accs = lax.cond(tight, fixed_shift_sweep, running_max_sweep, tuple(accs))
    heads = []
    for h in range(2):
        acc = jnp.concatenate([accs[h * nsub + r] for r in range(nsub)], axis=0)
        heads.append(acc * (1.0 / acc[:, HEAD_DIM:HEAD_DIM + 1]))
    o_ref[...] = jnp.where(feat, heads[0], pltpu.roll(heads[1], HEAD_DIM, 1)).astype(BF16)


def _fox_attention(q, k, v, batch, seq, tq, tk, sub):
    t = q.shape[0]
    nq = seq // tq
    kv = pl.BlockSpec((seq, 2 * LANES), lambda b, p, i: (b, p))
    return pl.pallas_call(
        functools.partial(_fox_kernel, tq=tq, tk=tk, sub=sub, seq=seq),
        grid=(batch, FOX_HEADS // 2, nq),
        in_specs=[pl.BlockSpec((tq, 2 * LANES), lambda b, p, i: (b * nq + i, p)), kv, kv],
        out_specs=pl.BlockSpec((tq, LANES), lambda b, p, i: (b * nq + i, p)),
        out_shape=jax.ShapeDtypeStruct((t, FOX_W), BF16),
        scratch_shapes=[pltpu.SMEM((2,), F32), pltpu.SMEM((2, seq // tk), F32)],
        compiler_params=_params(("parallel", "parallel", "arbitrary")),
        name="fox_attention",
    )(q, k, v)


def _out_proj_kernel(*refs, n_in):
    x_ref, o_refs, w_ref, y_ref = refs[0], refs[1:1 + n_in], refs[1 + n_in], refs[2 + n_in]
    y = x_ref[...]
    off = 0
    for o_ref in o_refs:
        width = o_ref.shape[1]
        y = y + _dot(o_ref[...], w_ref[off:off + width, :])
        off += width
    y_ref[...] = y


def _out_proj(xf, outs, w, tm):
    t, d = xf.shape
    row = lambda width: pl.BlockSpec((tm, width), lambda i: (i, 0))
    return pl.pallas_call(
        functools.partial(_out_proj_kernel, n_in=len(outs)),
        grid=(t // tm,),
        in_specs=[row(d)] + [row(o.shape[1]) for o in outs] + [pl.BlockSpec(w.shape, lambda i: (0, 0))],
        out_specs=row(d),
        out_shape=jax.ShapeDtypeStruct((t, d), F32),
        compiler_params=_params(("parallel",)),
        name="out_proj",
    )(xf, *outs, w)


FFN_CHUNK = 256
HALO = 8


def _ffn_kernel(*refs, tm, nsb, final):
    if final:
        x_ref, g_ref, win_ref, cw_ref, cb_ref, wout_ref, fn_ref, o_ref, a_scr = refs
    else:
        x_ref, g_ref, win_ref, cw_ref, cb_ref, wout_ref, o_ref, a_scr = refs
    i = pl.program_id(0)

    @pl.when(i % nsb == 0)
    def _():
        a_scr[0:HALO, :] = jnp.zeros((HALO, D_FF), F32)

    @pl.when(i % nsb != 0)
    def _():
        a_scr[0:HALO, :] = a_scr[tm:tm + HALO, :]

    x = x_ref[...]
    hb = _rmsnorm(x, g_ref[...]).astype(BF16)
    acc = jnp.zeros((tm, D_MODEL), F32)
    for c in range(D_FF // FFN_CHUNK):
        sl = slice(c * FFN_CHUNK, (c + 1) * FFN_CHUNK)
        a = _dot(hb, win_ref[:, sl])
        b = _dot(hb, win_ref[:, D_FF + c * FFN_CHUNK:D_FF + (c + 1) * FFN_CHUNK])
        a_scr[HALO:HALO + tm, sl] = a
        conv = (cw_ref[0:1, sl] * a_scr[HALO - 2:HALO - 2 + tm, sl]
                + cw_ref[1:2, sl] * a_scr[HALO - 1:HALO - 1 + tm, sl]
                + cw_ref[2:3, sl] * a + cb_ref[:, sl])
        gated = conv * (1.0 / (1.0 + jnp.exp(-conv))) * b
        acc = acc + _dot(gated.astype(BF16), wout_ref[sl, :])
    y = x + acc
    if final:
        y = _rmsnorm(y, fn_ref[...])
    o_ref[...] = y


def _ffn(xf, g, w_in, conv_w, conv_b, w_out, final_norm, seq, tm):
    t, d = xf.shape
    nsb = seq // tm
    row = pl.BlockSpec((tm, d), lambda i: (i, 0))
    const = lambda shape: pl.BlockSpec(shape, lambda i: (0, 0), pipeline_mode=pl.Buffered(1))
    small = lambda shape: pl.BlockSpec(shape, lambda i: (0, 0))
    final = final_norm is not None
    in_specs = [row, small((1, d)), const(w_in.shape), small(conv_w.shape), small((1, D_FF)), const(w_out.shape)]
    args = [xf, g, w_in, conv_w, conv_b, w_out]
    if final:
        in_specs.append(small((1, d)))
        args.append(final_norm)
    return pl.pallas_call(
        functools.partial(_ffn_kernel, tm=tm, nsb=nsb, final=final),
        grid=(t // tm,),
        in_specs=in_specs,
        out_specs=row,
        out_shape=jax.ShapeDtypeStruct((t, d), F32),
        scratch_shapes=[pltpu.VMEM((tm + HALO, D_FF), F32)],
        compiler_params=_params(("arbitrary",)),
        name="conv_glu_ffn",
    )(*args)


def kernel(x, attn_norm, ffn_norm, ev_w_in, ev_cmp_pos_k, ev_cmp_pos_v, ev_cmp_w_k, ev_cmp_w_v, ev_w_out,
           od_w_in, od_b_f, od_w_out, ffn_w_in, ffn_conv_w, ffn_conv_b, ffn_w_out, final_norm):
    batch, seq, d = x.shape
    t = batch * seq
    depth = attn_norm.shape[0]
    tm = min(512, seq)
    sb_tile, sb_nsub = min(256, seq), 2
    fox_tq, fox_tk, fox_sub = min(512, seq), min(512, seq), 512
    sel_tk = min(512, seq)

    xf = x.reshape(t, d)
    tabs = _rope_tables(jnp.arange(seq))
    ctabs = _rope_tables(jnp.arange(seq // CMP_STRIDE) * CMP_STRIDE + (CMP_LEN - 1))
    mt, oh = _nsa_consts(seq)

    for layer in range(depth):
        g_attn = attn_norm[layer].reshape(1, d)
        if layer % 2 == 0:
            e = layer // 2
            (sbq, sbk, sbv, nq, kc_raw, vc_raw, ksl, vsl, kwn, vwn, gates) = _even_proj(
                xf, g_attn, _prep_even_w(ev_w_in[e]), tabs, seq, tm)
            pek, wk = _prep_cmp(ev_cmp_pos_k[e], ev_cmp_w_k[e])
            pev, wv = _prep_cmp(ev_cmp_pos_v[e], ev_cmp_w_v[e])
            kc, vc = _compress(kc_raw, vc_raw, pek, pev, wk, wv, ctabs, batch, seq)
            o_sb = _sb_attention(sbq, sbk, sbv, batch, seq, sb_tile, sb_nsub)
            o_nsa = _nsa_attention(nq, kc, vc, ksl, vsl, kwn, vwn, gates, mt, oh, batch, seq, sel_tk)
            xf = _out_proj(xf, [o_sb, o_nsa], ev_w_out[e].astype(BF16), tm)
        else:
            o = layer // 2
            w = jnp.pad(od_w_in[o], ((0, 0), (0, LANES - FOX_HEADS))).astype(BF16)
            bf = jnp.pad(od_b_f[o], (0, LANES - FOX_HEADS)).reshape(1, LANES)
            q, k, v = _odd_proj(xf, g_attn, w, bf, seq, tm)
            o_fox = _fox_attention(q, k, v, batch, seq, fox_tq, fox_tk, fox_sub)
            xf = _out_proj(xf, [o_fox], od_w_out[o].astype(BF16), tm)
        last = layer == depth - 1
        xf = _ffn(xf, ffn_norm[layer].reshape(1, d), ffn_w_in[layer].astype(BF16), ffn_conv_w[layer],
                  ffn_conv_b[layer].reshape(1, D_FF), ffn_w_out[layer].astype(BF16),
                  final_norm.reshape(1, d) if last else None, seq, tm)
    return xf.reshape(batch, seq, d)
```

```python
import functools
import math

import jax
import jax.numpy as jnp
from jax import lax
from jax.experimental import pallas as pl
from jax.experimental.pallas import tpu as pltpu

F32, BF16 = jnp.float32, jnp.bfloat16

D_MODEL = 1024
HEAD_DIM = 64
N_HEADS = D_MODEL // HEAD_DIM
SB_HEADS = N_HEADS // 2
NSA_HEADS = N_HEADS - SB_HEADS
NSA_KV_GROUPS = 2
NSA_REP = NSA_HEADS // NSA_KV_GROUPS
FOX_HEADS = N_HEADS
CMP_LEN = 32
CMP_STRIDE = 16
SEL_LEN = 64
SEL_TOP = 16
WINDOW = 512
N_BRANCH = 3
ROPE_THETA = 500000.0
ROT_DIM = HEAD_DIM // 4
D_FF = 2816
CONV_WIDTH = 3
NORM_EPS = 1e-6
SB_W = SB_HEADS * HEAD_DIM
NSA_QW = NSA_HEADS * HEAD_DIM
NSA_KVW = NSA_KV_GROUPS * HEAD_DIM
FOX_W = FOX_HEADS * HEAD_DIM

LANES = 128
SEL_LANES = 128
QK_SCALE = HEAD_DIM ** -0.5
LOG2E = math.log2(math.e)
SEL_NEG = -(2.0 ** 30)
NSA_TIGHT = 86.0
NORM_SLACK = 1.001
VMEM_LIMIT = 56 * 2 ** 20

_NT = (((1,), (1,)), ((), ()))


def _params(sem):
    return pltpu.CompilerParams(dimension_semantics=sem, vmem_limit_bytes=VMEM_LIMIT)


def _dot(a, b):
    return jnp.dot(a, b, preferred_element_type=F32)


def _dot_nt(a, b):
    return lax.dot_general(a, b, _NT, preferred_element_type=F32)


def _rmsnorm(x, g):
    ms = jnp.mean(x * x, axis=-1, keepdims=True)
    return (x * lax.rsqrt(ms + NORM_EPS)) * g


def _rope(y, c, s1, s2):
    return y * c + pltpu.roll(y, LANES - ROT_DIM // 2, 1) * s1 + pltpu.roll(y, ROT_DIM // 2, 1) * s2


def _split3(x):
    hi = x.astype(BF16)
    r1 = x - hi.astype(F32)
    mid = r1.astype(BF16)
    lo = (r1 - mid.astype(F32)).astype(BF16)
    return hi, mid, lo


def _rope_tables(pos):
    half = ROT_DIM // 2
    inv_freq = ROPE_THETA ** (-(jnp.arange(half, dtype=F32) * 2.0 / ROT_DIM))
    ang = pos.astype(F32)[:, None] * inv_freq[None, :]
    cos, sin = jnp.cos(ang), jnp.sin(ang)
    n = pos.shape[0]
    one = jnp.ones((n, HEAD_DIM - ROT_DIM), F32)
    zero = jnp.zeros((n, HEAD_DIM - ROT_DIM), F32)
    z8 = jnp.zeros((n, half), F32)
    c = jnp.concatenate([cos, cos, one], -1)
    s1 = jnp.concatenate([-sin, z8, zero], -1)
    s2 = jnp.concatenate([z8, sin, zero], -1)
    two = lambda t: jnp.concatenate([t, t], -1)
    return two(c), two(s1), two(s2)


def _even_proj_kernel(x_ref, g_ref, w_ref, c_ref, s1_ref, s2_ref,
                      sbq_ref, sbk_ref, sbv_ref, nq_ref, kc_ref, vc_ref,
                      ksl_ref, vsl_ref, kwn_ref, vwn_ref, gate_ref):
    hb = _rmsnorm(x_ref[...], g_ref[...]).astype(BF16)
    c, s1, s2 = c_ref[...], s1_ref[...], s2_ref[...]

    def seg(a, n):
        return _dot(hb, w_ref[:, a:a + n])

    sbq_ref[...] = (seg(0, SB_W) * QK_SCALE).astype(BF16)
    sbk_ref[...] = seg(SB_W, SB_W).astype(BF16)
    sbv_ref[...] = seg(2 * SB_W, SB_W).astype(BF16)
    base = 3 * SB_W
    mxu_n = 2 * LANES
    lo, hi = slice(0, LANES), slice(LANES, mxu_n)
    for j in range(NSA_QW // mxu_n):
        y2 = seg(base + j * mxu_n, mxu_n)
        for half in (lo, hi):
            y = _rope(y2[:, half], c, s1, s2)
            nq_ref[:, j * mxu_n + half.start:j * mxu_n + half.stop] = (y * (QK_SCALE * LOG2E)).astype(BF16)
    base += NSA_QW
    cmp2 = seg(base, mxu_n)
    kc_ref[...] = cmp2[:, lo]
    vc_ref[...] = cmp2[:, hi]
    slc2 = seg(base + mxu_n, mxu_n)
    ksl_ref[...] = _rope(slc2[:, lo], c, s1, s2).astype(BF16)
    vsl_ref[...] = slc2[:, hi].astype(BF16)
    win2 = seg(base + 2 * mxu_n, mxu_n)
    kwn_ref[...] = _rope(win2[:, lo], c, s1, s2).astype(BF16)
    vwn_ref[...] = win2[:, hi].astype(BF16)
    gl = seg(base + 6 * LANES, 2 * LANES)
    gate_ref[...] = 1.0 / (1.0 + jnp.exp(-gl))


def _even_proj(xf, g, w, tabs, seq, tm):
    t, d = xf.shape
    nsb = seq // tm
    n = w.shape[1]
    row = lambda width: pl.BlockSpec((tm, width), lambda i: (i, 0))
    tab = pl.BlockSpec((tm, LANES), lambda i: (i % nsb, 0))
    out_shape = (
        [jax.ShapeDtypeStruct((t, SB_W), BF16)] * 3
        + [jax.ShapeDtypeStruct((t, NSA_QW), BF16)]
        + [jax.ShapeDtypeStruct((t, LANES), F32)] * 2
        + [jax.ShapeDtypeStruct((t, LANES), BF16)] * 4
        + [jax.ShapeDtypeStruct((t, 2 * LANES), F32)]
    )
    out_specs = [row(SB_W)] * 3 + [row(NSA_QW)] + [row(LANES)] * 6 + [row(2 * LANES)]
    return pl.pallas_call(
        _even_proj_kernel,
        grid=(t // tm,),
        in_specs=[row(d), pl.BlockSpec((1, d), lambda i: (0, 0)),
                  pl.BlockSpec((d, n), lambda i: (0, 0)), tab, tab, tab],
        out_specs=out_specs,
        out_shape=out_shape,
        compiler_params=_params(("parallel",)),
        name="even_proj",
    )(xf, g, w, *tabs)


def _prep_even_w(w):
    d = w.shape[0]
    main = w[:, :3 * SB_W + NSA_QW + 6 * NSA_KVW]
    gates = w[:, 3 * SB_W + NSA_QW + 6 * NSA_KVW:].reshape(d, NSA_KV_GROUPS, NSA_REP, N_BRANCH)
    gates = jnp.transpose(gates, (0, 1, 3, 2)).reshape(d, NSA_KV_GROUPS, N_BRANCH * NSA_REP)
    gates = jnp.pad(gates, ((0, 0), (0, 0), (0, LANES - N_BRANCH * NSA_REP)))
    return jnp.concatenate([main, gates.reshape(d, NSA_KV_GROUPS * LANES)], axis=1).astype(BF16)


SB_DEAD = -105.0


def _sb_kernel(q_ref, k_ref, v_ref, o_ref, *, tile, nsub):
    i = pl.program_id(2)
    lane = lax.broadcasted_iota(jnp.int32, (1, LANES), 1)
    r = lax.broadcasted_iota(jnp.int32, (tile, tile), 0)
    c = lax.broadcasted_iota(jnp.int32, (tile, tile), 1)
    later = jnp.where(r > c, 1.0, 0.0).astype(BF16)
    diag = c < r
    chains = [(h, sub) for h in range(2) for sub in range(nsub)]
    qs = []
    for h, sub in chains:
        q = q_ref[sub * tile:(sub + 1) * tile, :]
        qs.append(jnp.where((lane < HEAD_DIM) if h == 0 else (lane >= HEAD_DIM), q, jnp.zeros_like(q)))

    def step(jt, state, mode):
        k0 = pl.multiple_of(jt * tile, tile)
        kt = k_ref[pl.ds(k0, tile), :]
        vt = v_ref[pl.ds(k0, tile), :]
        out = []
        for c, (acc, keep_sum) in enumerate(state):
            if mode[c] is None:
                out.append((acc, keep_sum))
                continue
            z = _dot_nt(qs[c], kt)
            ls = jnp.minimum(z, 0.0) - jnp.log(1.0 + jnp.exp(-jnp.abs(z)))
            lk = ls - z
            if mode[c]:
                lk = jnp.where(diag, lk, 0.0)
            hi = lk.astype(BF16)
            lo = (lk - hi.astype(F32)).astype(BF16)
            after = _dot(hi, later) + _dot(lo, later)
            a = jnp.exp(ls + after + keep_sum)
            if mode[c]:
                a = jnp.where(diag, a, 0.0)
            out.append((acc + _dot(a.astype(BF16), vt), keep_sum + after[:, 0:1] + lk[:, 0:1]))
        return tuple(out)

    def alive(state):
        return functools.reduce(jnp.maximum, [jnp.max(keep_sum) for _, keep_sum in state]) >= SB_DEAD

    state = tuple((jnp.zeros((tile, LANES), F32), jnp.zeros((tile, 1), F32)) for _ in chains)
    for top in range(nsub - 1, -1, -1):
        mode = [None if sub < top else sub == top for _, sub in chains]
        state = step(nsub * i + top, state, mode)

    def body(carry):
        jt, _, state = carry
        state = step(jt, state, [False] * len(chains))
        return jt - 1, alive(state), state

    _, _, state = lax.while_loop(lambda cr: (cr[0] >= 0) & cr[1], body, (nsub * i - 1, alive(state), state))
    heads = [jnp.concatenate([state[h * nsub + sub][0] for sub in range(nsub)], axis=0) for h in range(2)]
    o_ref[...] = jnp.where(lane < HEAD_DIM, heads[0], heads[1]).astype(BF16)


def _sb_attention(q, k, v, batch, seq, tile, nsub):
    t, w = q.shape
    nq = seq // (tile * nsub)
    kv = pl.BlockSpec((seq, LANES), lambda b, p, i: (b, p))
    qo = pl.BlockSpec((tile * nsub, LANES), lambda b, p, i: (b * nq + i, p))
    return pl.pallas_call(
        functools.partial(_sb_kernel, tile=tile, nsub=nsub),
        grid=(batch, w // LANES, nq),
        in_specs=[qo, kv, kv],
        out_specs=qo,
        out_shape=jax.ShapeDtypeStruct((t, w), BF16),
        compiler_params=_params(("parallel", "parallel", "arbitrary")),
        name="sb_attention",
    )(q, k, v)


def _compress_kernel(xk_ref, xv_ref, pek_ref, pev_ref, wk_ref, wv_ref, c_ref, s1_ref, s2_ref,
                     kc_ref, vc_ref, *, nc):
    def comp(x_ref, pe_ref, w_ref):
        x = x_ref[0]
        top = _dot((x + pe_ref[0:1, :]).astype(BF16), w_ref[0])
        bot = _dot((x + pe_ref[1:2, :]).astype(BF16), w_ref[1])
        return top + pltpu.roll(bot, nc - 1, 0)

    kc = _rope(comp(xk_ref, pek_ref, wk_ref), c_ref[...], s1_ref[...], s2_ref[...])
    kc_ref[0] = kc.astype(BF16)
    vc_ref[0] = comp(xv_ref, pev_ref, wv_ref).astype(BF16)


def _prep_cmp(pe, w):
    half = CMP_LEN // 2
    w3 = w.reshape(CMP_LEN, HEAD_DIM, HEAD_DIM)
    eye = jnp.eye(NSA_KV_GROUPS, dtype=w.dtype)
    parts, pes = [], []
    for s in range(2):
        wh = jnp.einsum('ldo,gh->lgdho', w3[s * half:(s + 1) * half], eye)
        parts.append(wh.reshape(half * NSA_KVW, NSA_KVW))
        pes.append(jnp.broadcast_to(pe[s * half:(s + 1) * half, None, :],
                                    (half, NSA_KV_GROUPS, HEAD_DIM)).reshape(1, half * NSA_KVW))
    return jnp.concatenate(pes, 0), jnp.stack(parts).astype(BF16)


def _compress(kc_raw, vc_raw, pek, pev, wk, wv, ctabs, batch, seq):
    nc = seq // CMP_STRIDE
    cw = CMP_STRIDE * NSA_KVW
    xk = kc_raw.reshape(batch, nc, cw)
    xv = vc_raw.reshape(batch, nc, cw)
    xs = pl.BlockSpec((1, nc, cw), lambda b: (b, 0, 0))
    pes = pl.BlockSpec((2, cw), lambda b: (0, 0))
    ws = pl.BlockSpec((2, cw, NSA_KVW), lambda b: (0, 0, 0))
    tab = pl.BlockSpec((nc, LANES), lambda b: (0, 0))
    out = pl.BlockSpec((1, nc, LANES), lambda b: (b, 0, 0))
    return pl.pallas_call(
        functools.partial(_compress_kernel, nc=nc),
        grid=(batch,),
        in_specs=[xs, xs, pes, pes, ws, ws, tab, tab, tab],
        out_specs=[out, out],
        out_shape=[jax.ShapeDtypeStruct((batch, nc, LANES), BF16)] * 2,
        compiler_params=_params(("parallel",)),
        name="nsa_compress",
    )(xk, xv, pek, pev, wk, wv, *ctabs)


QB = 128


def _softmax_rows(s):
    m = jnp.max(s, axis=-1, keepdims=True)
    m = jnp.where(m > -jnp.inf, m, 0.0)
    e = jnp.exp2(s - m)
    d = jnp.sum(e, axis=-1, keepdims=True)
    return e / jnp.where(d > 0, d, 1.0)


def _nsa_kernel(q_ref, kc_ref, vc_ref, ks_ref, vs_ref, kw_ref, vw_ref, g_ref, mt_ref, oh_ref,
                o_ref, score_ref, knorm_ref, *, nc, tk, wlen, seq):
    g = pl.program_id(1)
    i = pl.program_id(2)
    q0 = i * QB
    rows = NSA_REP * QB

    q2 = q_ref[...]
    pr = lax.broadcasted_iota(jnp.int32, (NSA_REP * HEAD_DIM, LANES), 0)
    pc = lax.broadcasted_iota(jnp.int32, (NSA_REP * HEAD_DIM, LANES), 1) - g * HEAD_DIM
    in_group = (pc >= 0) & (pc < HEAD_DIM)
    q4 = jnp.concatenate(
        [_dot(q2, jnp.where(in_group & (pr - hh * HEAD_DIM == pc), 1.0, 0.0).astype(BF16)).astype(BF16)
         for hh in range(NSA_REP)], axis=0)

    sc = _dot_nt(q4, kc_ref[0])
    cend = lax.broadcasted_iota(jnp.int32, (rows, nc), 1) * CMP_STRIDE + (CMP_LEN - 1)
    tpos_c = q0 + (lax.broadcasted_iota(jnp.int32, (rows, nc), 0) & (QB - 1))
    p_cmp = _softmax_rows(jnp.where(cend <= tpos_c, sc, -jnp.inf))
    o_cmp = _dot(p_cmp.astype(BF16), vc_ref[0])

    p_sum = p_cmp[0:QB] + p_cmp[QB:2 * QB] + p_cmp[2 * QB:3 * QB] + p_cmp[3 * QB:4 * QB]
    overlap = mt_ref[...]
    imp_t = jnp.transpose(sum(_dot(part, overlap) for part in _split3(p_sum)))
    nidx = lax.broadcasted_iota(jnp.int32, (SEL_LANES, QB), 0)
    cur = (q0 + lax.broadcasted_iota(jnp.int32, (SEL_LANES, QB), 1)) // SEL_LEN
    causal = nidx <= cur
    forced = (nidx == 0) | (nidx == cur) | (nidx == cur - 1)
    score = jnp.where(causal, jnp.where(forced, jnp.inf, imp_t), -jnp.inf)
    score_ref[...] = score

    def larger_step(m, cnt):
        return cnt + jnp.where(score_ref[pl.ds(m, 1), :] > score, 1.0, 0.0)

    def larger_pair(m2, cnt):
        return larger_step(2 * m2 + 1, larger_step(2 * m2, cnt))

    def rank_step(m, cnt):
        rowm = score_ref[pl.ds(m, 1), :]
        ge = jnp.where(rowm >= score, 1.0, 0.0)
        gt = jnp.where(rowm > score, 1.0, 0.0)
        return cnt + jnp.where(nidx > m, ge, gt)

    n_causal = (q0 + QB) // SEL_LEN
    zero_cnt = jnp.zeros((SEL_LANES, QB), F32)
    cnt = lax.fori_loop(0, n_causal // 2, larger_pair, zero_cnt)
    n_top = jnp.sum(jnp.where(causal & (cnt < float(SEL_TOP)), 1.0, 0.0), axis=0, keepdims=True)
    cnt = lax.cond(jnp.max(n_top) > float(SEL_TOP),
                   lambda c: lax.fori_loop(0, n_causal, rank_step, zero_cnt), lambda c: c, cnt)
    bias_t = jnp.where(causal & (cnt < float(SEL_TOP)), 0.0, SEL_NEG)
    selb4 = jnp.concatenate([jnp.transpose(bias_t)] * NSA_REP, axis=0)

    lane = lax.broadcasted_iota(jnp.int32, (1, LANES), 1)
    own_lanes = (lane >= g * HEAD_DIM) & (lane < (g + 1) * HEAD_DIM)

    @pl.when(i == 0)
    def _():
        feat = lax.broadcasted_iota(jnp.int32, (LANES, 1), 0)
        own_rows = (feat >= g * HEAD_DIM) & (feat < (g + 1) * HEAD_DIM)

        def chunk(n, best):
            kf = ks_ref[0, :, pl.ds(pl.multiple_of(n * tk, tk), tk)].astype(F32)
            kf = jnp.where(own_rows, kf, 0.0)
            return jnp.maximum(best, jnp.max(jnp.sum(kf * kf, axis=0, keepdims=True)))
        knorm_ref[0] = jnp.sqrt(lax.fori_loop(0, seq // tk, chunk, jnp.float32(0.0)))

    q4f = q4.astype(F32)
    shift = jnp.sqrt(jnp.sum(q4f * q4f, axis=-1, keepdims=True)) * (knorm_ref[0] * NORM_SLACK)
    shift = (shift * (1.0 + 2.0 ** -7)).astype(BF16).astype(F32)
    tight = 2.0 * jnp.max(shift) <= NSA_TIGHT
    qs = jnp.concatenate([q4, jnp.where(tight, selb4 - shift, selb4).astype(BF16)], axis=1)
    n_full = q0 // tk

    def sel_tiles(jt):
        k0 = pl.multiple_of(jt * tk, tk)
        kk = jnp.concatenate([ks_ref[0, :, pl.ds(k0, tk)], oh_ref[:, pl.ds(k0, tk)]], axis=0)
        vt = vs_ref[pl.ds(k0, tk), :]
        return kk, jnp.where(own_lanes, vt, jnp.ones_like(vt))

    def scores(jt, masked):
        kk, vt = sel_tiles(jt)
        s = _dot(qs, kk)
        if masked:
            kpos = jt * tk + lax.broadcasted_iota(jnp.int32, (rows, tk), 1)
            tpos = q0 + (lax.broadcasted_iota(jnp.int32, (rows, tk), 0) & (QB - 1))
            s = jnp.where(kpos <= tpos, s, -jnp.inf)
        return s, vt

    def fixed_shift_sweep(acc):
        def step(jt, acc, masked):
            s, vt = scores(jt, masked)
            return acc + _dot(jnp.exp2(s).astype(BF16), vt)
        acc = lax.fori_loop(0, n_full, lambda jt, a: step(jt, a, False), acc)
        return step(n_full, acc, True)

    def running_max_sweep(acc):
        def step(jt, carry, masked):
            m, acc = carry
            s, vt = scores(jt, masked)
            m_new = jnp.maximum(m, jnp.max(s, axis=-1, keepdims=True))
            return m_new, jnp.exp2(m - m_new) * acc + _dot(jnp.exp2(s - m_new).astype(BF16), vt)
        carry = lax.fori_loop(0, n_full, lambda jt, cr: step(jt, cr, False),
                              (jnp.full((rows, 1), -jnp.inf, F32), acc))
        return step(n_full, carry, True)[1]

    acc_s = lax.cond(tight, fixed_shift_sweep, running_max_sweep, jnp.zeros((rows, LANES), F32))
    l_s = jnp.where(g == 0, acc_s[:, HEAD_DIM:HEAD_DIM + 1], acc_s[:, 0:1])
    o_slc = acc_s / l_s

    w0 = pl.multiple_of(jnp.maximum(q0 - WINDOW, 0), QB)
    sw = _dot_nt(q4, kw_ref[pl.ds(w0, wlen), :])
    kpos = w0 + lax.broadcasted_iota(jnp.int32, (rows, wlen), 1)
    tpos = q0 + (lax.broadcasted_iota(jnp.int32, (rows, wlen), 0) & (QB - 1))
    p_win = _softmax_rows(jnp.where((kpos <= tpos) & (kpos > tpos - WINDOW), sw, -jnp.inf))
    o_win = _dot(p_win.astype(BF16), vw_ref[pl.ds(w0, wlen), :])

    gates = g_ref[...]
    orow = lax.broadcasted_iota(jnp.int32, (LANES, NSA_REP * HEAD_DIM), 0) - g * HEAD_DIM
    ocol = lax.broadcasted_iota(jnp.int32, (LANES, NSA_REP * HEAD_DIM), 1)
    o_group = (orow >= 0) & (orow < HEAD_DIM)
    out = jnp.zeros((QB, NSA_REP * HEAD_DIM), F32)
    for hh in range(NSA_REP):
        sl = slice(hh * QB, (hh + 1) * QB)
        gate = lambda br: gates[:, br * NSA_REP + hh:br * NSA_REP + hh + 1]
        o_h = gate(0) * o_cmp[sl] + gate(1) * o_slc[sl] + gate(2) * o_win[sl]
        place = jnp.where(o_group & (ocol - hh * HEAD_DIM == orow), 1.0, 0.0).astype(BF16)
        out = out + _dot(o_h.astype(BF16), place)
    o_ref[...] = out.astype(BF16)


def _nsa_consts(seq):
    nc = seq // CMP_STRIDE
    c_start = jnp.arange(nc) * CMP_STRIDE
    sel_start = jnp.arange(SEL_LANES) * SEL_LEN
    real = (jnp.arange(SEL_LANES) < seq // SEL_LEN)[:, None] & (jnp.arange(nc) < (seq - CMP_LEN) // CMP_STRIDE + 1)[None, :]
    overlap = (c_start[None, :] < sel_start[:, None] + SEL_LEN) & (c_start[None, :] + CMP_LEN > sel_start[:, None])
    mt = jnp.transpose(overlap & real).astype(BF16)
    oh = (jnp.arange(seq)[None, :] // SEL_LEN == jnp.arange(SEL_LANES)[:, None]).astype(BF16)
    return mt, oh


def _nsa_attention(nq, kc, vc, ksl, vsl, kwn, vwn, gates, mt, oh, batch, seq, tk):
    t = nq.shape[0]
    nqb = seq // QB
    nc = seq // CMP_STRIDE
    wlen = WINDOW + QB
    gw = NSA_REP * HEAD_DIM
    qspec = pl.BlockSpec((QB, gw), lambda b, g, i: (b * nqb + i, g))
    cspec = pl.BlockSpec((1, nc, LANES), lambda b, g, i: (b, 0, 0))
    kvspec = pl.BlockSpec((seq, LANES), lambda b, g, i: (b, 0))
    gspec = pl.BlockSpec((QB, LANES), lambda b, g, i: (b * nqb + i, g))
    return pl.pallas_call(
        functools.partial(_nsa_kernel, nc=nc, tk=tk, wlen=wlen, seq=seq),
        grid=(batch, NSA_KV_GROUPS, nqb),
        in_specs=[qspec, cspec, cspec, pl.BlockSpec((1, LANES, seq), lambda b, g, i: (b, 0, 0)),
                  kvspec, kvspec, kvspec, gspec,
                  pl.BlockSpec((nc, SEL_LANES), lambda b, g, i: (0, 0)),
                  pl.BlockSpec((SEL_LANES, seq), lambda b, g, i: (0, 0))],
        out_specs=qspec,
        out_shape=jax.ShapeDtypeStruct((t, NSA_QW), BF16),
        scratch_shapes=[pltpu.VMEM((SEL_LANES, QB), F32), pltpu.SMEM((1,), F32)],
        compiler_params=_params(("parallel", "parallel", "arbitrary")),
        name="nsa_attention",
    )(nq, kc, vc, jnp.swapaxes(ksl.reshape(batch, seq, LANES), 1, 2), vsl, kwn, vwn, gates, mt, oh)


N_CUM_PARTS = 3


def _odd_proj_kernel(x_ref, g_ref, w_ref, bf_ref, place_ref, q_ref, k_ref, v_ref, carry_ref, *, tm, nsb):
    i = pl.program_id(0)
    hb = _rmsnorm(x_ref[...], g_ref[...]).astype(BF16)
    f = _dot(hb, w_ref[:, 3 * FOX_W:3 * FOX_W + LANES]) + bf_ref[...]
    log_f = jnp.minimum(f, 0.0) - jnp.log1p(jnp.exp(-jnp.abs(f)))
    r = lax.broadcasted_iota(jnp.int32, (tm, tm), 0)
    c = lax.broadcasted_iota(jnp.int32, (tm, tm), 1)
    tri = jnp.where(c <= r, 1.0, 0.0).astype(BF16)
    local = sum(_dot(tri, part) for part in _split3(log_f))

    @pl.when(i % nsb == 0)
    def _():
        carry_ref[...] = jnp.zeros_like(carry_ref)

    cum = local + carry_ref[0:1, :]
    carry_ref[0:1, :] = cum[tm - 1:tm, :]
    lane = lax.broadcasted_iota(jnp.int32, (1, LANES), 1)
    hi, mid, lo = (part.astype(F32) for part in _split3(cum * (-LOG2E)))
    packed = jnp.where(lane < FOX_HEADS, hi,
                       jnp.where(lane < 2 * FOX_HEADS, pltpu.roll(mid, FOX_HEADS, 1),
                                 pltpu.roll(lo, 2 * FOX_HEADS, 1))).astype(BF16)
    low = lane < HEAD_DIM
    q_one = jnp.where((lane >= HEAD_DIM) & (lane < HEAD_DIM + N_CUM_PARTS), 1.0, 0.0)
    v_one = jnp.where(lane >= HEAD_DIM, 1.0, 0.0)
    k_one = jnp.where((lane >= HEAD_DIM + N_CUM_PARTS) & (lane < HEAD_DIM + N_CUM_PARTS + N_SHIFT_PARTS), 1.0, 0.0)
    mxu_n = 2 * LANES
    for p2 in range(FOX_W // mxu_n):
        cols = slice(p2 * mxu_n, (p2 + 1) * mxu_n)
        yq2 = _dot(hb, w_ref[:, cols]) * (QK_SCALE * LOG2E)
        yk2 = _dot(hb, w_ref[:, FOX_W + p2 * mxu_n:FOX_W + (p2 + 1) * mxu_n])
        yv2 = _dot(hb, w_ref[:, 2 * FOX_W + p2 * mxu_n:2 * FOX_W + (p2 + 1) * mxu_n])
        for pp in range(2):
            p = 2 * p2 + pp
            pair = slice(pp * LANES, (pp + 1) * LANES)
            yq, yk, yv = yq2[:, pair], yk2[:, pair], yv2[:, pair]
            extras = _dot(packed, place_ref[:, 2 * p * LANES:(2 * p + 2) * LANES])
            for h in range(2):
                hs = slice((2 * p + h) * LANES, (2 * p + h + 1) * LANES)
                head = (lambda y: y) if h == 0 else (lambda y: pltpu.roll(y, HEAD_DIM, 1))
                q_ref[:, hs] = (jnp.where(low, head(yq), 0.0) + q_one).astype(BF16)
                k_ref[:, hs] = (jnp.where(low, head(yk), 0.0) + extras[:, h * LANES:(h + 1) * LANES]
                                + k_one).astype(BF16)
                v_ref[:, hs] = (jnp.where(low, head(yv), 0.0) + v_one).astype(BF16)


def _cum_placement():
    src = jnp.arange(LANES)[:, None]
    dst = jnp.arange(FOX_HEADS * LANES)[None, :]
    n, h = src // FOX_HEADS, src % FOX_HEADS
    return ((n < N_CUM_PARTS) & (dst == h * LANES + HEAD_DIM + n)).astype(BF16)


def _odd_proj(xf, g, w, bf, seq, tm):
    t, d = xf.shape
    nsb = seq // tm
    n = w.shape[1]
    wide = FOX_HEADS * LANES
    row = lambda width: pl.BlockSpec((tm, width), lambda i: (i, 0))
    return pl.pallas_call(
        functools.partial(_odd_proj_kernel, tm=tm, nsb=nsb),
        grid=(t // tm,),
        in_specs=[row(d), pl.BlockSpec((1, d), lambda i: (0, 0)),
                  pl.BlockSpec((d, n), lambda i: (0, 0)), pl.BlockSpec((1, LANES), lambda i: (0, 0)),
                  pl.BlockSpec((LANES, wide), lambda i: (0, 0))],
        out_specs=[row(wide)] * 3,
        out_shape=[jax.ShapeDtypeStruct((t, wide), BF16)] * 3,
        scratch_shapes=[pltpu.VMEM((8, LANES), F32)],
        compiler_params=_params(("arbitrary",)),
        name="odd_proj",
    )(xf, g, w, bf, _cum_placement())


FOX_DEAD = -160.0
FOX_TIGHT = 64.0
N_SHIFT_PARTS = 2


def _fox_kernel(q_ref, k_ref, v_ref, o_ref, knorm_ref, ncum_ref, *, tq, tk, sub, dsub, seq):
    i = pl.program_id(2)
    q0 = pl.multiple_of(i * tq, tq)
    n_full = q0 // tk
    nsub = tq // sub
    chains = [(h, r) for h in range(2) for r in range(nsub)]
    qs = [q_ref[r * sub:(r + 1) * sub, h * LANES:(h + 1) * LANES] for h, r in chains]
    lane = lax.broadcasted_iota(jnp.int32, (1, LANES), 1)
    feat = lane < HEAD_DIM
    cum_lanes = (lane >= HEAD_DIM) & (lane < HEAD_DIM + N_CUM_PARTS)

    def row_norms(x):
        xf = jnp.where(feat, x.astype(F32), 0.0)
        return jnp.sqrt(jnp.sum(xf * xf, axis=-1, keepdims=True))

    @pl.when(i == 0)
    def _():
        for h in range(2):
            def chunk(n, best, h=h):
                rows = k_ref[pl.ds(pl.multiple_of(n * tk, tk), tk), h * LANES:(h + 1) * LANES]
                ncum_ref[h, n] = jnp.sum(jnp.where(cum_lanes, rows[0:1, :].astype(F32), 0.0))
                return jnp.maximum(best, jnp.max(row_norms(rows)))
            knorm_ref[h] = lax.fori_loop(0, seq // tk, chunk, jnp.float32(0.0))

    k_norm = [knorm_ref[h] * NORM_SLACK for h in range(2)]
    q_norm = [row_norms(q) for q in qs]
    qk_bound = [functools.reduce(jnp.maximum, [jnp.max(q_norm[h * nsub + r]) for r in range(nsub)]) * k_norm[h]
                for h in range(2)]
    tight = 2.0 * jnp.maximum(qk_bound[0], qk_bound[1]) <= FOX_TIGHT
    cum_q0 = [ncum_ref[h, n_full] for h in range(2)]

    def tiles(jt, h):
        k0 = pl.multiple_of(jt * tk, tk)
        hs = slice(h * LANES, (h + 1) * LANES)
        return k_ref[pl.ds(k0, tk), hs], v_ref[pl.ds(k0, tk), hs]

    shifts, accs, q_shifted = [], [], []
    for c, ((h, r), q) in enumerate(zip(chains, qs)):
        pieces = []
        for d in range(sub // dsub):
            row0 = r * sub + d * dsub
            nk = row0 + dsub
            kt = k_ref[pl.ds(q0, nk), h * LANES:(h + 1) * LANES]
            vt = v_ref[pl.ds(q0, nk), h * LANES:(h + 1) * LANES]
            qd = q[d * dsub:(d + 1) * dsub]
            s = _dot_nt(qd, kt)
            kpos = lax.broadcasted_iota(jnp.int32, (dsub, nk), 1)
            tpos = row0 + lax.broadcasted_iota(jnp.int32, (dsub, nk), 0)
            s = jnp.where(kpos <= tpos, s, -jnp.inf)
            m_diag = jnp.max(s, axis=-1, keepdims=True)
            bound = q_norm[c][d * dsub:(d + 1) * dsub] * k_norm[h] + cum_q0[h]
            shift = jnp.where(tight, jnp.maximum(m_diag, bound), m_diag)
            hi = shift.astype(BF16)
            rest = shift - hi.astype(F32)
            lo = (rest + jnp.abs(rest) * (2.0 ** -7)).astype(BF16)
            shift = hi.astype(F32) + lo.astype(F32)
            acc = _dot(jnp.exp2(s - shift).astype(BF16), vt)
            qsh = jnp.where(lane == HEAD_DIM + N_CUM_PARTS, -hi,
                            jnp.where(lane == HEAD_DIM + N_CUM_PARTS + 1, -lo, qd))
            pieces.append((shift, acc, qsh))
        shifts.append(jnp.concatenate([p[0] for p in pieces], axis=0))
        accs.append(jnp.concatenate([p[1] for p in pieces], axis=0))
        q_shifted.append(jnp.concatenate([p[2] for p in pieces], axis=0))

    def may_matter(jt):
        return jnp.maximum(ncum_ref[0, jt + 1] - cum_q0[0], ncum_ref[1, jt + 1] - cum_q0[1]) >= FOX_DEAD

    def fixed_shift_sweep(accs):
        def body(carry):
            jt, accs = carry
            out = []
            for (h, r), q, acc in zip(chains, q_shifted, accs):
                kt, vt = tiles(jt, h)
                out.append(acc + _dot(jnp.exp2(_dot_nt(q, kt)).astype(BF16), vt))
            return jt - 1, tuple(out)

        return lax.while_loop(lambda cr: (cr[0] >= 0) & may_matter(jnp.maximum(cr[0], 0)), body,
                              (n_full - 1, accs))[1]

    def running_max_sweep(accs):
        def alive(jt, ms):
            go = None
            for h in range(2):
                m_min = functools.reduce(jnp.minimum, [jnp.min(ms[h * nsub + r]) for r in range(nsub)])
                live = qk_bound[h] + ncum_ref[h, jt + 1] - m_min >= FOX_DEAD
                go = live if go is None else (go | live)
            return go

        def body(carry):
            jt, _, ms, accs = carry
            new_m, new_acc = [], []
            for (h, r), q, m, acc in zip(chains, qs, ms, accs):
                kt, vt = tiles(jt, h)
                s = _dot_nt(q, kt)
                m_new = jnp.maximum(m, jnp.max(s, axis=-1, keepdims=True))
                new_acc.append(jnp.exp2(m - m_new) * acc + _dot(jnp.exp2(s - m_new).astype(BF16), vt))
                new_m.append(m_new)
            new_m = tuple(new_m)
            return jt - 1, alive(jnp.maximum(jt - 1, 0), new_m), new_m, tuple(new_acc)

        ms = tuple(shifts)
        return lax.while_loop(lambda cr: (cr[0] >= 0) & cr[1], body,
                              (n_full - 1, alive(jnp.maximum(n_full - 1, 0), ms), ms, accs))[3]

    accs = lax.cond(tight, fixed_shift_sweep, running_max_sweep, tuple(accs))
    heads = []
    for h in range(2):
        acc = jnp.concatenate([accs[h * nsub + r] for r in range(nsub)], axis=0)
        heads.append(acc * (1.0 / acc[:, HEAD_DIM:HEAD_DIM + 1]))
    o_ref[...] = jnp.where(feat, heads[0], pltpu.roll(heads[1], HEAD_DIM, 1)).astype(BF16)


def _fox_attention(q, k, v, batch, seq, tq, tk, sub, dsub):
    t = q.shape[0]
    nq = seq // tq
    kv = pl.BlockSpec((seq, 2 * LANES), lambda b, p, i: (b, p))
    return pl.pallas_call(
        functools.partial(_fox_kernel, tq=tq, tk=tk, sub=sub, dsub=dsub, seq=seq),
        grid=(batch, FOX_HEADS // 2, nq),
        in_specs=[pl.BlockSpec((tq, 2 * LANES), lambda b, p, i: (b * nq + i, p)), kv, kv],
        out_specs=pl.BlockSpec((tq, LANES), lambda b, p, i: (b * nq + i, p)),
        out_shape=jax.ShapeDtypeStruct((t, FOX_W), BF16),
        scratch_shapes=[pltpu.SMEM((2,), F32), pltpu.SMEM((2, seq // tk), F32)],
        compiler_params=_params(("parallel", "parallel", "arbitrary")),
        name="fox_attention",
    )(q, k, v)


FFN_CHUNK = 256
HALO = 8


def _ffn_kernel(*refs, tm, nsb, final, n_attn):
    attn_refs, wattn_ref, refs = refs[:n_attn], refs[n_attn], refs[n_attn + 1:]
    if final:
        x_ref, g_ref, win_ref, cw_ref, cb_ref, wout_ref, fn_ref, o_ref, a_scr = refs
    else:
        x_ref, g_ref, win_ref, cw_ref, cb_ref, wout_ref, o_ref, a_scr = refs
    i = pl.program_id(0)

    @pl.when(i % nsb == 0)
    def _():
        a_scr[0:HALO, :] = jnp.zeros((HALO, D_FF), F32)

    @pl.when(i % nsb != 0)
    def _():
        a_scr[0:HALO, :] = a_scr[tm:tm + HALO, :]

    x = x_ref[...]
    off = 0
    for a_ref in attn_refs:
        width = a_ref.shape[1]
        x = x + _dot(a_ref[...], wattn_ref[off:off + width, :])
        off += width
    hb = _rmsnorm(x, g_ref[...]).astype(BF16)
    acc = jnp.zeros((tm, D_MODEL), F32)
    for c in range(D_FF // FFN_CHUNK):
        sl = slice(c * FFN_CHUNK, (c + 1) * FFN_CHUNK)
        a = _dot(hb, win_ref[:, sl])
        b = _dot(hb, win_ref[:, D_FF + c * FFN_CHUNK:D_FF + (c + 1) * FFN_CHUNK])
        a_scr[HALO:HALO + tm, sl] = a
        conv = (cw_ref[0:1, sl] * a_scr[HALO - 2:HALO - 2 + tm, sl]
                + cw_ref[1:2, sl] * a_scr[HALO - 1:HALO - 1 + tm, sl]
                + cw_ref[2:3, sl] * a + cb_ref[:, sl])
        gated = conv * (1.0 / (1.0 + jnp.exp(-conv))) * b
        acc = acc + _dot(gated.astype(BF16), wout_ref[sl, :])
    y = x + acc
    if final:
        y = _rmsnorm(y, fn_ref[...])
    o_ref[...] = y


def _mixer_out_ffn(xf, attn_outs, w_attn, g, w_in, conv_w, conv_b, w_out, final_norm, seq, tm):
    t, d = xf.shape
    nsb = seq // tm
    row = pl.BlockSpec((tm, d), lambda i: (i, 0))
    const = lambda shape: pl.BlockSpec(shape, lambda i: (0, 0), pipeline_mode=pl.Buffered(1))
    small = lambda shape: pl.BlockSpec(shape, lambda i: (0, 0))
    final = final_norm is not None
    in_specs = [pl.BlockSpec((tm, a.shape[1]), lambda i: (i, 0)) for a in attn_outs] + [const(w_attn.shape)]
    in_specs += [row, small((1, d)), const(w_in.shape), small(conv_w.shape), small((1, D_FF)), const(w_out.shape)]
    args = list(attn_outs) + [w_attn, xf, g, w_in, conv_w, conv_b, w_out]
    if final:
        in_specs.append(small((1, d)))
        args.append(final_norm)
    return pl.pallas_call(
        functools.partial(_ffn_kernel, tm=tm, nsb=nsb, final=final, n_attn=len(attn_outs)),
        grid=(t // tm,),
        in_specs=in_specs,
        out_specs=row,
        out_shape=jax.ShapeDtypeStruct((t, d), F32),
        scratch_shapes=[pltpu.VMEM((tm + HALO, D_FF), F32)],
        compiler_params=_params(("arbitrary",)),
        name="conv_glu_ffn",
    )(*args)


def kernel(x, attn_norm, ffn_norm, ev_w_in, ev_cmp_pos_k, ev_cmp_pos_v, ev_cmp_w_k, ev_cmp_w_v, ev_w_out,
           od_w_in, od_b_f, od_w_out, ffn_w_in, ffn_conv_w, ffn_conv_b, ffn_w_out, final_norm):
    batch, seq, d = x.shape
    t = batch * seq
    depth = attn_norm.shape[0]
    tm = min(512, seq)
    sb_tile, sb_nsub = min(256, seq), 2
    fox_tq, fox_tk, fox_sub, fox_dsub = min(512, seq), min(512, seq), 512, 512
    sel_tk = min(512, seq)

    xf = x.reshape(t, d)
    tabs = _rope_tables(jnp.arange(seq))
    ctabs = _rope_tables(jnp.arange(seq // CMP_STRIDE) * CMP_STRIDE + (CMP_LEN - 1))
    mt, oh = _nsa_consts(seq)

    for layer in range(depth):
        g_attn = attn_norm[layer].reshape(1, d)
        if layer % 2 == 0:
            e = layer // 2
            (sbq, sbk, sbv, nq, kc_raw, vc_raw, ksl, vsl, kwn, vwn, gates) = _even_proj(
                xf, g_attn, _prep_even_w(ev_w_in[e]), tabs, seq, tm)
            pek, wk = _prep_cmp(ev_cmp_pos_k[e], ev_cmp_w_k[e])
            pev, wv = _prep_cmp(ev_cmp_pos_v[e], ev_cmp_w_v[e])
            kc, vc = _compress(kc_raw, vc_raw, pek, pev, wk, wv, ctabs, batch, seq)
            o_sb = _sb_attention(sbq, sbk, sbv, batch, seq, sb_tile, sb_nsub)
            o_nsa = _nsa_attention(nq, kc, vc, ksl, vsl, kwn, vwn, gates, mt, oh, batch, seq, sel_tk)
            attn_outs, w_attn = [o_sb, o_nsa], ev_w_out[e].astype(BF16)
        else:
            o = layer // 2
            w = jnp.pad(od_w_in[o], ((0, 0), (0, LANES - FOX_HEADS))).astype(BF16)
            bf = jnp.pad(od_b_f[o], (0, LANES - FOX_HEADS)).reshape(1, LANES)
            q, k, v = _odd_proj(xf, g_attn, w, bf, seq, tm)
            o_fox = _fox_attention(q, k, v, batch, seq, fox_tq, fox_tk, fox_sub, fox_dsub)
            attn_outs, w_attn = [o_fox], od_w_out[o].astype(BF16)
        last = layer == depth - 1
        xf = _mixer_out_ffn(xf, attn_outs, w_attn, ffn_norm[layer].reshape(1, d), ffn_w_in[layer].astype(BF16),
                            ffn_conv_w[layer], ffn_conv_b[layer].reshape(1, D_FF), ffn_w_out[layer].astype(BF16),
                            final_norm.reshape(1, d) if last else None, seq, tm)
    return xf.reshape(batch, seq, d)
```

```python
import functools
import math

import jax
import jax.numpy as jnp
from jax import lax
from jax.experimental import pallas as pl
from jax.experimental.pallas import tpu as pltpu

F32, BF16 = jnp.float32, jnp.bfloat16

D_MODEL = 1024
HEAD_DIM = 64
N_HEADS = D_MODEL // HEAD_DIM
SB_HEADS = N_HEADS // 2
NSA_HEADS = N_HEADS - SB_HEADS
NSA_KV_GROUPS = 2
NSA_REP = NSA_HEADS // NSA_KV_GROUPS
FOX_HEADS = N_HEADS
CMP_LEN = 32
CMP_STRIDE = 16
SEL_LEN = 64
SEL_TOP = 16
WINDOW = 512
N_BRANCH = 3
ROPE_THETA = 500000.0
ROT_DIM = HEAD_DIM // 4
D_FF = 2816
CONV_WIDTH = 3
NORM_EPS = 1e-6
SB_W = SB_HEADS * HEAD_DIM
NSA_QW = NSA_HEADS * HEAD_DIM
NSA_KVW = NSA_KV_GROUPS * HEAD_DIM
FOX_W = FOX_HEADS * HEAD_DIM

LANES = 128
SEL_LANES = 128
QK_SCALE = HEAD_DIM ** -0.5
LOG2E = math.log2(math.e)
SEL_NEG = -(2.0 ** 30)
NSA_TIGHT = 86.0
NORM_SLACK = 1.001
VMEM_LIMIT = 56 * 2 ** 20

_NT = (((1,), (1,)), ((), ()))


def _params(sem):
    return pltpu.CompilerParams(dimension_semantics=sem, vmem_limit_bytes=VMEM_LIMIT)


def _dot(a, b):
    return jnp.dot(a, b, preferred_element_type=F32)


def _dot_nt(a, b):
    return lax.dot_general(a, b, _NT, preferred_element_type=F32)


def _rmsnorm(x, g):
    ms = jnp.mean(x * x, axis=-1, keepdims=True)
    return (x * lax.rsqrt(ms + NORM_EPS)) * g


def _rope(y, c, s1, s2):
    return y * c + pltpu.roll(y, LANES - ROT_DIM // 2, 1) * s1 + pltpu.roll(y, ROT_DIM // 2, 1) * s2


def _split3(x):
    hi = x.astype(BF16)
    r1 = x - hi.astype(F32)
    mid = r1.astype(BF16)
    lo = (r1 - mid.astype(F32)).astype(BF16)
    return hi, mid, lo


def _rope_tables(pos):
    half = ROT_DIM // 2
    inv_freq = ROPE_THETA ** (-(jnp.arange(half, dtype=F32) * 2.0 / ROT_DIM))
    ang = pos.astype(F32)[:, None] * inv_freq[None, :]
    cos, sin = jnp.cos(ang), jnp.sin(ang)
    n = pos.shape[0]
    one = jnp.ones((n, HEAD_DIM - ROT_DIM), F32)
    zero = jnp.zeros((n, HEAD_DIM - ROT_DIM), F32)
    z8 = jnp.zeros((n, half), F32)
    c = jnp.concatenate([cos, cos, one], -1)
    s1 = jnp.concatenate([-sin, z8, zero], -1)
    s2 = jnp.concatenate([z8, sin, zero], -1)
    two = lambda t: jnp.concatenate([t, t], -1)
    return two(c), two(s1), two(s2)


def _even_proj_kernel(x_ref, g_ref, w_ref, c_ref, s1_ref, s2_ref,
                      sbq_ref, sbk_ref, sbv_ref, nq_ref, kc_ref, vc_ref,
                      ksl_ref, vsl_ref, kwn_ref, vwn_ref, gate_ref):
    hb = _rmsnorm(x_ref[...], g_ref[...]).astype(BF16)
    c, s1, s2 = c_ref[...], s1_ref[...], s2_ref[...]

    def seg(a, n):
        return _dot(hb, w_ref[:, a:a + n])

    sbq_ref[...] = (seg(0, SB_W) * QK_SCALE).astype(BF16)
    sbk_ref[...] = seg(SB_W, SB_W).astype(BF16)
    sbv_ref[...] = seg(2 * SB_W, SB_W).astype(BF16)
    base = 3 * SB_W
    mxu_n = 2 * LANES
    lo, hi = slice(0, LANES), slice(LANES, mxu_n)
    for j in range(NSA_QW // mxu_n):
        y2 = seg(base + j * mxu_n, mxu_n)
        for half in (lo, hi):
            y = _rope(y2[:, half], c, s1, s2)
            nq_ref[:, j * mxu_n + half.start:j * mxu_n + half.stop] = (y * (QK_SCALE * LOG2E)).astype(BF16)
    base += NSA_QW
    cmp2 = seg(base, mxu_n)
    kc_ref[...] = cmp2[:, lo]
    vc_ref[...] = cmp2[:, hi]
    slc2 = seg(base + mxu_n, mxu_n)
    ksl_ref[...] = _rope(slc2[:, lo], c, s1, s2).astype(BF16)
    vsl_ref[...] = slc2[:, hi].astype(BF16)
    win2 = seg(base + 2 * mxu_n, mxu_n)
    kwn_ref[...] = _rope(win2[:, lo], c, s1, s2).astype(BF16)
    vwn_ref[...] = win2[:, hi].astype(BF16)
    gl = seg(base + 6 * LANES, 2 * LANES)
    gate_ref[...] = 1.0 / (1.0 + jnp.exp(-gl))


def _even_proj(xf, g, w, tabs, seq, tm):
    t, d = xf.shape
    nsb = seq // tm
    n = w.shape[1]
    row = lambda width: pl.BlockSpec((tm, width), lambda i: (i, 0))
    tab = pl.BlockSpec((tm, LANES), lambda i: (i % nsb, 0))
    out_shape = (
        [jax.ShapeDtypeStruct((t, SB_W), BF16)] * 3
        + [jax.ShapeDtypeStruct((t, NSA_QW), BF16)]
        + [jax.ShapeDtypeStruct((t, LANES), F32)] * 2
        + [jax.ShapeDtypeStruct((t, LANES), BF16)] * 4
        + [jax.ShapeDtypeStruct((t, 2 * LANES), F32)]
    )
    out_specs = [row(SB_W)] * 3 + [row(NSA_QW)] + [row(LANES)] * 6 + [row(2 * LANES)]
    return pl.pallas_call(
        _even_proj_kernel,
        grid=(t // tm,),
        in_specs=[row(d), pl.BlockSpec((1, d), lambda i: (0, 0)),
                  pl.BlockSpec((d, n), lambda i: (0, 0)), tab, tab, tab],
        out_specs=out_specs,
        out_shape=out_shape,
        compiler_params=_params(("parallel",)),
        name="even_proj",
    )(xf, g, w, *tabs)


def _prep_even_w(w):
    d = w.shape[0]
    main = w[:, :3 * SB_W + NSA_QW + 6 * NSA_KVW]
    gates = w[:, 3 * SB_W + NSA_QW + 6 * NSA_KVW:].reshape(d, NSA_KV_GROUPS, NSA_REP, N_BRANCH)
    gates = jnp.transpose(gates, (0, 1, 3, 2)).reshape(d, NSA_KV_GROUPS, N_BRANCH * NSA_REP)
    gates = jnp.pad(gates, ((0, 0), (0, 0), (0, LANES - N_BRANCH * NSA_REP)))
    return jnp.concatenate([main, gates.reshape(d, NSA_KV_GROUPS * LANES)], axis=1).astype(BF16)


SB_DEAD = -105.0


def _sb_kernel(q_ref, k_ref, v_ref, o_ref, *, tile, nsub):
    i = pl.program_id(2)
    lane = lax.broadcasted_iota(jnp.int32, (1, LANES), 1)
    r = lax.broadcasted_iota(jnp.int32, (tile, tile), 0)
    c = lax.broadcasted_iota(jnp.int32, (tile, tile), 1)
    later = jnp.where(r > c, 1.0, 0.0).astype(BF16)
    diag = c < r
    chains = [(h, sub) for h in range(2) for sub in range(nsub)]
    qs = []
    for h, sub in chains:
        q = q_ref[sub * tile:(sub + 1) * tile, :]
        qs.append(jnp.where((lane < HEAD_DIM) if h == 0 else (lane >= HEAD_DIM), q, jnp.zeros_like(q)))

    def step(jt, state, mode):
        k0 = pl.multiple_of(jt * tile, tile)
        kt = k_ref[pl.ds(k0, tile), :]
        vt = v_ref[pl.ds(k0, tile), :]
        out = []
        for c, (acc, keep_sum) in enumerate(state):
            if mode[c] is None:
                out.append((acc, keep_sum))
                continue
            z = _dot_nt(qs[c], kt)
            ls = jnp.minimum(z, 0.0) - jnp.log(1.0 + jnp.exp(-jnp.abs(z)))
            lk = ls - z
            if mode[c]:
                lk = jnp.where(diag, lk, 0.0)
            hi = lk.astype(BF16)
            lo = (lk - hi.astype(F32)).astype(BF16)
            after = _dot(hi, later) + _dot(lo, later)
            a = jnp.exp(ls + after + keep_sum)
            if mode[c]:
                a = jnp.where(diag, a, 0.0)
            out.append((acc + _dot(a.astype(BF16), vt), keep_sum + after[:, 0:1] + lk[:, 0:1]))
        return tuple(out)

    def alive(state):
        return functools.reduce(jnp.maximum, [jnp.max(keep_sum) for _, keep_sum in state]) >= SB_DEAD

    state = tuple((jnp.zeros((tile, LANES), F32), jnp.zeros((tile, 1), F32)) for _ in chains)
    for top in range(nsub - 1, -1, -1):
        mode = [None if sub < top else sub == top for _, sub in chains]
        state = step(nsub * i + top, state, mode)

    def body(carry):
        jt, _, state = carry
        state = step(jt, state, [False] * len(chains))
        return jt - 1, alive(state), state

    _, _, state = lax.while_loop(lambda cr: (cr[0] >= 0) & cr[1], body, (nsub * i - 1, alive(state), state))
    heads = [jnp.concatenate([state[h * nsub + sub][0] for sub in range(nsub)], axis=0) for h in range(2)]
    o_ref[...] = jnp.where(lane < HEAD_DIM, heads[0], heads[1]).astype(BF16)


def _sb_attention(q, k, v, batch, seq, tile, nsub):
    t, w = q.shape
    nq = seq // (tile * nsub)
    kv = pl.BlockSpec((seq, LANES), lambda b, p, i: (b, p))
    qo = pl.BlockSpec((tile * nsub, LANES), lambda b, p, i: (b * nq + i, p))
    return pl.pallas_call(
        functools.partial(_sb_kernel, tile=tile, nsub=nsub),
        grid=(batch, w // LANES, nq),
        in_specs=[qo, kv, kv],
        out_specs=qo,
        out_shape=jax.ShapeDtypeStruct((t, w), BF16),
        compiler_params=_params(("parallel", "parallel", "arbitrary")),
        name="sb_attention",
    )(q, k, v)


def _compress_kernel(xk_ref, xv_ref, pek_ref, pev_ref, wk_ref, wv_ref, c_ref, s1_ref, s2_ref,
                     kc_ref, vc_ref, *, nc):
    def comp(x_ref, pe_ref, w_ref):
        x = x_ref[0]
        top = _dot((x + pe_ref[0:1, :]).astype(BF16), w_ref[0])
        bot = _dot((x + pe_ref[1:2, :]).astype(BF16), w_ref[1])
        return top + pltpu.roll(bot, nc - 1, 0)

    kc = _rope(comp(xk_ref, pek_ref, wk_ref), c_ref[...], s1_ref[...], s2_ref[...])
    kc_ref[0] = kc.astype(BF16)
    vc_ref[0] = comp(xv_ref, pev_ref, wv_ref).astype(BF16)


def _prep_cmp(pe, w):
    half = CMP_LEN // 2
    w3 = w.reshape(CMP_LEN, HEAD_DIM, HEAD_DIM)
    eye = jnp.eye(NSA_KV_GROUPS, dtype=w.dtype)
    parts, pes = [], []
    for s in range(2):
        wh = jnp.einsum('ldo,gh->lgdho', w3[s * half:(s + 1) * half], eye)
        parts.append(wh.reshape(half * NSA_KVW, NSA_KVW))
        pes.append(jnp.broadcast_to(pe[s * half:(s + 1) * half, None, :],
                                    (half, NSA_KV_GROUPS, HEAD_DIM)).reshape(1, half * NSA_KVW))
    return jnp.concatenate(pes, 0), jnp.stack(parts).astype(BF16)


def _compress(kc_raw, vc_raw, pek, pev, wk, wv, ctabs, batch, seq):
    nc = seq // CMP_STRIDE
    cw = CMP_STRIDE * NSA_KVW
    xk = kc_raw.reshape(batch, nc, cw)
    xv = vc_raw.reshape(batch, nc, cw)
    xs = pl.BlockSpec((1, nc, cw), lambda b: (b, 0, 0))
    pes = pl.BlockSpec((2, cw), lambda b: (0, 0))
    ws = pl.BlockSpec((2, cw, NSA_KVW), lambda b: (0, 0, 0))
    tab = pl.BlockSpec((nc, LANES), lambda b: (0, 0))
    out = pl.BlockSpec((1, nc, LANES), lambda b: (b, 0, 0))
    return pl.pallas_call(
        functools.partial(_compress_kernel, nc=nc),
        grid=(batch,),
        in_specs=[xs, xs, pes, pes, ws, ws, tab, tab, tab],
        out_specs=[out, out],
        out_shape=[jax.ShapeDtypeStruct((batch, nc, LANES), BF16)] * 2,
        compiler_params=_params(("parallel",)),
        name="nsa_compress",
    )(xk, xv, pek, pev, wk, wv, *ctabs)


QB = 128


def _softmax_rows(s):
    m = jnp.max(s, axis=-1, keepdims=True)
    m = jnp.where(m > -jnp.inf, m, 0.0)
    e = jnp.exp2(s - m)
    d = jnp.sum(e, axis=-1, keepdims=True)
    return e / jnp.where(d > 0, d, 1.0)


def _nsa_kernel(q_ref, kc_ref, vc_ref, ks_ref, vs_ref, kw_ref, vw_ref, g_ref, mt_ref, oh_ref,
                o_ref, score_ref, knorm_ref, *, nc, tk, wlen, seq):
    g = pl.program_id(1)
    i = pl.program_id(2)
    q0 = i * QB
    rows = NSA_REP * QB

    q2 = q_ref[...]
    pr = lax.broadcasted_iota(jnp.int32, (NSA_REP * HEAD_DIM, LANES), 0)
    pc = lax.broadcasted_iota(jnp.int32, (NSA_REP * HEAD_DIM, LANES), 1) - g * HEAD_DIM
    in_group = (pc >= 0) & (pc < HEAD_DIM)
    q4 = jnp.concatenate(
        [_dot(q2, jnp.where(in_group & (pr - hh * HEAD_DIM == pc), 1.0, 0.0).astype(BF16)).astype(BF16)
         for hh in range(NSA_REP)], axis=0)

    def compressed(ncv):
        sc = _dot_nt(q4, kc_ref[0, 0:ncv, :])
        cend = lax.broadcasted_iota(jnp.int32, (rows, ncv), 1) * CMP_STRIDE + (CMP_LEN - 1)
        tpos_c = q0 + (lax.broadcasted_iota(jnp.int32, (rows, ncv), 0) & (QB - 1))
        p_cmp = _softmax_rows(jnp.where(cend <= tpos_c, sc, -jnp.inf))
        o_cmp = _dot(p_cmp.astype(BF16), vc_ref[0, 0:ncv, :])
        p_sum = p_cmp[0:QB] + p_cmp[QB:2 * QB] + p_cmp[2 * QB:3 * QB] + p_cmp[3 * QB:4 * QB]
        imp = sum(_dot(part, mt_ref[0:ncv, :]) for part in _split3(p_sum))
        return o_cmp, imp

    o_cmp, imp = lax.cond((q0 + QB) * 2 <= seq, lambda: compressed(nc // 2), lambda: compressed(nc))
    imp_t = jnp.transpose(imp)
    nidx = lax.broadcasted_iota(jnp.int32, (SEL_LANES, QB), 0)
    cur = (q0 + lax.broadcasted_iota(jnp.int32, (SEL_LANES, QB), 1)) // SEL_LEN
    causal = nidx <= cur
    forced = (nidx == 0) | (nidx == cur) | (nidx == cur - 1)
    score = jnp.where(causal, jnp.where(forced, jnp.inf, imp_t), -jnp.inf)
    score_ref[...] = score

    def larger_step(m, cnt):
        return cnt + jnp.where(score_ref[pl.ds(m, 1), :] > score, 1.0, 0.0)

    def larger_pair(m2, cnt):
        return larger_step(2 * m2 + 1, larger_step(2 * m2, cnt))

    def rank_step(m, cnt):
        rowm = score_ref[pl.ds(m, 1), :]
        ge = jnp.where(rowm >= score, 1.0, 0.0)
        gt = jnp.where(rowm > score, 1.0, 0.0)
        return cnt + jnp.where(nidx > m, ge, gt)

    n_causal = (q0 + QB) // SEL_LEN
    zero_cnt = jnp.zeros((SEL_LANES, QB), F32)
    cnt = lax.fori_loop(0, n_causal // 2, larger_pair, zero_cnt)
    n_top = jnp.sum(jnp.where(causal & (cnt < float(SEL_TOP)), 1.0, 0.0), axis=0, keepdims=True)
    cnt = lax.cond(jnp.max(n_top) > float(SEL_TOP),
                   lambda c: lax.fori_loop(0, n_causal, rank_step, zero_cnt), lambda c: c, cnt)
    bias_t = jnp.where(causal & (cnt < float(SEL_TOP)), 0.0, SEL_NEG)
    selb4 = jnp.concatenate([jnp.transpose(bias_t)] * NSA_REP, axis=0)

    lane = lax.broadcasted_iota(jnp.int32, (1, LANES), 1)
    own_lanes = (lane >= g * HEAD_DIM) & (lane < (g + 1) * HEAD_DIM)

    @pl.when(i == 0)
    def _():
        feat = lax.broadcasted_iota(jnp.int32, (LANES, 1), 0)
        own_rows = (feat >= g * HEAD_DIM) & (feat < (g + 1) * HEAD_DIM)

        def chunk(n, best):
            kf = ks_ref[0, :, pl.ds(pl.multiple_of(n * tk, tk), tk)].astype(F32)
            kf = jnp.where(own_rows, kf, 0.0)
            return jnp.maximum(best, jnp.max(jnp.sum(kf * kf, axis=0, keepdims=True)))
        knorm_ref[0] = jnp.sqrt(lax.fori_loop(0, seq // tk, chunk, jnp.float32(0.0)))

    q4f = q4.astype(F32)
    shift = jnp.sqrt(jnp.sum(q4f * q4f, axis=-1, keepdims=True)) * (knorm_ref[0] * NORM_SLACK)
    shift = (shift * (1.0 + 2.0 ** -7)).astype(BF16).astype(F32)
    tight = 2.0 * jnp.max(shift) <= NSA_TIGHT
    qs = jnp.concatenate([q4, jnp.where(tight, selb4 - shift, selb4).astype(BF16)], axis=1)
    n_full = q0 // tk

    def sel_tiles(jt):
        k0 = pl.multiple_of(jt * tk, tk)
        kk = jnp.concatenate([ks_ref[0, :, pl.ds(k0, tk)], oh_ref[:, pl.ds(k0, tk)]], axis=0)
        vt = vs_ref[pl.ds(k0, tk), :]
        return kk, jnp.where(own_lanes, vt, jnp.ones_like(vt))

    def scores(jt, masked):
        kk, vt = sel_tiles(jt)
        s = _dot(qs, kk)
        if masked:
            kpos = jt * tk + lax.broadcasted_iota(jnp.int32, (rows, tk), 1)
            tpos = q0 + (lax.broadcasted_iota(jnp.int32, (rows, tk), 0) & (QB - 1))
            s = jnp.where(kpos <= tpos, s, -jnp.inf)
        return s, vt

    def fixed_shift_sweep(acc):
        def step(jt, acc, masked):
            s, vt = scores(jt, masked)
            return acc + _dot(jnp.exp2(s).astype(BF16), vt)
        acc = lax.fori_loop(0, n_full, lambda jt, a: step(jt, a, False), acc)
        return step(n_full, acc, True)

    def running_max_sweep(acc):
        def step(jt, carry, masked):
            m, acc = carry
            s, vt = scores(jt, masked)
            m_new = jnp.maximum(m, jnp.max(s, axis=-1, keepdims=True))
            return m_new, jnp.exp2(m - m_new) * acc + _dot(jnp.exp2(s - m_new).astype(BF16), vt)
        carry = lax.fori_loop(0, n_full, lambda jt, cr: step(jt, cr, False),
                              (jnp.full((rows, 1), -jnp.inf, F32), acc))
        return step(n_full, carry, True)[1]

    acc_s = lax.cond(tight, fixed_shift_sweep, running_max_sweep, jnp.zeros((rows, LANES), F32))
    l_s = jnp.where(g == 0, acc_s[:, HEAD_DIM:HEAD_DIM + 1], acc_s[:, 0:1])
    o_slc = acc_s / l_s

    w0 = pl.multiple_of(jnp.maximum(q0 - WINDOW, 0), QB)
    sw = _dot_nt(q4, kw_ref[pl.ds(w0, wlen), :])
    kpos = w0 + lax.broadcasted_iota(jnp.int32, (rows, wlen), 1)
    tpos = q0 + (lax.broadcasted_iota(jnp.int32, (rows, wlen), 0) & (QB - 1))
    p_win = _softmax_rows(jnp.where((kpos <= tpos) & (kpos > tpos - WINDOW), sw, -jnp.inf))
    o_win = _dot(p_win.astype(BF16), vw_ref[pl.ds(w0, wlen), :])

    gates = g_ref[...]
    orow = lax.broadcasted_iota(jnp.int32, (LANES, NSA_REP * HEAD_DIM), 0) - g * HEAD_DIM
    ocol = lax.broadcasted_iota(jnp.int32, (LANES, NSA_REP * HEAD_DIM), 1)
    o_group = (orow >= 0) & (orow < HEAD_DIM)
    out = jnp.zeros((QB, NSA_REP * HEAD_DIM), F32)
    for hh in range(NSA_REP):
        sl = slice(hh * QB, (hh + 1) * QB)
        gate = lambda br: gates[:, br * NSA_REP + hh:br * NSA_REP + hh + 1]
        o_h = gate(0) * o_cmp[sl] + gate(1) * o_slc[sl] + gate(2) * o_win[sl]
        place = jnp.where(o_group & (ocol - hh * HEAD_DIM == orow), 1.0, 0.0).astype(BF16)
        out = out + _dot(o_h.astype(BF16), place)
    o_ref[...] = out.astype(BF16)


def _nsa_consts(seq):
    nc = seq // CMP_STRIDE
    c_start = jnp.arange(nc) * CMP_STRIDE
    sel_start = jnp.arange(SEL_LANES) * SEL_LEN
    real = (jnp.arange(SEL_LANES) < seq // SEL_LEN)[:, None] & (jnp.arange(nc) < (seq - CMP_LEN) // CMP_STRIDE + 1)[None, :]
    overlap = (c_start[None, :] < sel_start[:, None] + SEL_LEN) & (c_start[None, :] + CMP_LEN > sel_start[:, None])
    mt = jnp.transpose(overlap & real).astype(BF16)
    oh = (jnp.arange(seq)[None, :] // SEL_LEN == jnp.arange(SEL_LANES)[:, None]).astype(BF16)
    return mt, oh


def _nsa_attention(nq, kc, vc, ksl, vsl, kwn, vwn, gates, mt, oh, batch, seq, tk):
    t = nq.shape[0]
    nqb = seq // QB
    nc = seq // CMP_STRIDE
    wlen = WINDOW + QB
    gw = NSA_REP * HEAD_DIM
    qspec = pl.BlockSpec((QB, gw), lambda b, g, i: (b * nqb + i, g))
    cspec = pl.BlockSpec((1, nc, LANES), lambda b, g, i: (b, 0, 0))
    kvspec = pl.BlockSpec((seq, LANES), lambda b, g, i: (b, 0))
    gspec = pl.BlockSpec((QB, LANES), lambda b, g, i: (b * nqb + i, g))
    return pl.pallas_call(
        functools.partial(_nsa_kernel, nc=nc, tk=tk, wlen=wlen, seq=seq),
        grid=(batch, NSA_KV_GROUPS, nqb),
        in_specs=[qspec, cspec, cspec, pl.BlockSpec((1, LANES, seq), lambda b, g, i: (b, 0, 0)),
                  kvspec, kvspec, kvspec, gspec,
                  pl.BlockSpec((nc, SEL_LANES), lambda b, g, i: (0, 0)),
                  pl.BlockSpec((SEL_LANES, seq), lambda b, g, i: (0, 0))],
        out_specs=qspec,
        out_shape=jax.ShapeDtypeStruct((t, NSA_QW), BF16),
        scratch_shapes=[pltpu.VMEM((SEL_LANES, QB), F32), pltpu.SMEM((1,), F32)],
        compiler_params=_params(("parallel", "parallel", "arbitrary")),
        name="nsa_attention",
    )(nq, kc, vc, jnp.swapaxes(ksl.reshape(batch, seq, LANES), 1, 2), vsl, kwn, vwn, gates, mt, oh)


N_CUM_PARTS = 3


def _odd_proj_kernel(x_ref, g_ref, w_ref, bf_ref, place_ref, q_ref, k_ref, v_ref, stats_ref, carry_ref,
                     *, tm, nsb):
    i = pl.program_id(0)
    hb = _rmsnorm(x_ref[...], g_ref[...]).astype(BF16)
    f = _dot(hb, w_ref[:, 3 * FOX_W:3 * FOX_W + LANES]) + bf_ref[...]
    log_f = jnp.minimum(f, 0.0) - jnp.log1p(jnp.exp(-jnp.abs(f)))
    r = lax.broadcasted_iota(jnp.int32, (tm, tm), 0)
    c = lax.broadcasted_iota(jnp.int32, (tm, tm), 1)
    tri = jnp.where(c <= r, 1.0, 0.0).astype(BF16)
    local = sum(_dot(tri, part) for part in _split3(log_f))

    @pl.when(i % nsb == 0)
    def _():
        carry_ref[...] = jnp.zeros_like(carry_ref)

    cum = local + carry_ref[0:1, :]
    carry_ref[0:1, :] = cum[tm - 1:tm, :]
    lane = lax.broadcasted_iota(jnp.int32, (1, LANES), 1)
    neg_cum = cum * (-LOG2E)
    hi, mid, lo = (part.astype(F32) for part in _split3(neg_cum))
    k_sq = jnp.zeros((1, LANES), F32)
    head_of_lane = jnp.where(lax.broadcasted_iota(jnp.int32, (LANES, LANES), 0) // HEAD_DIM
                             == lax.broadcasted_iota(jnp.int32, (LANES, LANES), 1), 1.0, 0.0).astype(BF16)
    packed = jnp.where(lane < FOX_HEADS, hi,
                       jnp.where(lane < 2 * FOX_HEADS, pltpu.roll(mid, FOX_HEADS, 1),
                                 pltpu.roll(lo, 2 * FOX_HEADS, 1))).astype(BF16)
    low = lane < HEAD_DIM
    q_one = jnp.where((lane >= HEAD_DIM) & (lane < HEAD_DIM + N_CUM_PARTS), 1.0, 0.0)
    v_one = jnp.where(lane >= HEAD_DIM, 1.0, 0.0)
    k_one = jnp.where((lane >= HEAD_DIM + N_CUM_PARTS) & (lane < HEAD_DIM + N_CUM_PARTS + N_SHIFT_PARTS), 1.0, 0.0)
    mxu_n = 2 * LANES
    for p2 in range(FOX_W // mxu_n):
        cols = slice(p2 * mxu_n, (p2 + 1) * mxu_n)
        yq2 = _dot(hb, w_ref[:, cols]) * (QK_SCALE * LOG2E)
        yk2 = _dot(hb, w_ref[:, FOX_W + p2 * mxu_n:FOX_W + (p2 + 1) * mxu_n])
        yv2 = _dot(hb, w_ref[:, 2 * FOX_W + p2 * mxu_n:2 * FOX_W + (p2 + 1) * mxu_n])
        for pp in range(2):
            p = 2 * p2 + pp
            pair = slice(pp * LANES, (pp + 1) * LANES)
            yq, yk, yv = yq2[:, pair], yk2[:, pair], yv2[:, pair]
            extras = _dot(packed, place_ref[:, 2 * p * LANES:(2 * p + 2) * LANES])
            pair_sq = jnp.max(_dot((yk * yk * (1.0 + 2.0 ** -7)).astype(BF16), head_of_lane), axis=0, keepdims=True)
            for h in range(2):
                hs = slice((2 * p + h) * LANES, (2 * p + h + 1) * LANES)
                head = (lambda y: y) if h == 0 else (lambda y: pltpu.roll(y, HEAD_DIM, 1))
                kh = jnp.where(low, head(yk), 0.0)
                k_sq = jnp.where(lane == 2 * p + h, pair_sq[:, h:h + 1], k_sq)
                q_ref[:, hs] = (jnp.where(low, head(yq), 0.0) + q_one).astype(BF16)
                k_ref[:, hs] = (kh + extras[:, h * LANES:(h + 1) * LANES] + k_one).astype(BF16)
                v_ref[:, hs] = (jnp.where(low, head(yv), 0.0) + v_one).astype(BF16)
    stats_ref[...] = jnp.zeros(stats_ref.shape, F32)
    stats_ref[0:1, :] = k_sq
    stats_ref[1:2, :] = neg_cum[0:1, :]


def _cum_placement():
    src = jnp.arange(LANES)[:, None]
    dst = jnp.arange(FOX_HEADS * LANES)[None, :]
    n, h = src // FOX_HEADS, src % FOX_HEADS
    return ((n < N_CUM_PARTS) & (dst == h * LANES + HEAD_DIM + n)).astype(BF16)


def _odd_proj(xf, g, w, bf, seq, tm):
    t, d = xf.shape
    nsb = seq // tm
    n = w.shape[1]
    wide = FOX_HEADS * LANES
    row = lambda width: pl.BlockSpec((tm, width), lambda i: (i, 0))
    return pl.pallas_call(
        functools.partial(_odd_proj_kernel, tm=tm, nsb=nsb),
        grid=(t // tm,),
        in_specs=[row(d), pl.BlockSpec((1, d), lambda i: (0, 0)),
                  pl.BlockSpec((d, n), lambda i: (0, 0)), pl.BlockSpec((1, LANES), lambda i: (0, 0)),
                  pl.BlockSpec((LANES, wide), lambda i: (0, 0))],
        out_specs=[row(wide)] * 3 + [pl.BlockSpec((8, LANES), lambda i: (i, 0))],
        out_shape=[jax.ShapeDtypeStruct((t, wide), BF16)] * 3 + [jax.ShapeDtypeStruct((t // tm * 8, LANES), F32)],
        scratch_shapes=[pltpu.VMEM((8, LANES), F32)],
        compiler_params=_params(("arbitrary",)),
        name="odd_proj",
    )(xf, g, w, bf, _cum_placement())


FOX_DEAD = -160.0
FOX_TIGHT = 64.0
N_SHIFT_PARTS = 2


def _fox_kernel(stats_ref, q_ref, k_ref, v_ref, o_ref, knorm_ref, ncum_ref, *, tq, tk, sub, dsub, seq):
    b, pair, i = pl.program_id(0), pl.program_id(1), pl.program_id(2)
    q0 = pl.multiple_of(i * tq, tq)
    n_full = q0 // tk
    nsub = tq // sub
    chains = [(h, r) for h in range(2) for r in range(nsub)]
    qs = [q_ref[r * sub:(r + 1) * sub, h * LANES:(h + 1) * LANES] for h, r in chains]
    lane = lax.broadcasted_iota(jnp.int32, (1, LANES), 1)
    feat = lane < HEAD_DIM
    n_tiles = seq // tk

    def row_norms(x):
        xf = jnp.where(feat, x.astype(F32), 0.0)
        return jnp.sqrt(jnp.sum(xf * xf, axis=-1, keepdims=True))

    @pl.when(i == 0)
    def _():
        for h in range(2):
            def tile_stats(n, best, h=h):
                ncum_ref[h, n] = stats_ref[b * n_tiles + n, FOX_HEADS + 2 * pair + h]
                return jnp.maximum(best, stats_ref[b * n_tiles + n, 2 * pair + h])
            knorm_ref[h] = jnp.sqrt(lax.fori_loop(0, n_tiles, tile_stats, jnp.float32(0.0)))

    k_norm = [knorm_ref[h] * (NORM_SLACK * (1.0 + 2.0 ** -8)) for h in range(2)]
    q_norm = [row_norms(q) for q in qs]
    qk_bound = [functools.reduce(jnp.maximum, [jnp.max(q_norm[h * nsub + r]) for r in range(nsub)]) * k_norm[h]
                for h in range(2)]
    tight = 2.0 * jnp.maximum(qk_bound[0], qk_bound[1]) <= FOX_TIGHT
    cum_q0 = [ncum_ref[h, n_full] for h in range(2)]

    def tiles(jt, h):
        k0 = pl.multiple_of(jt * tk, tk)
        hs = slice(h * LANES, (h + 1) * LANES)
        return k_ref[pl.ds(k0, tk), hs], v_ref[pl.ds(k0, tk), hs]

    shifts, accs, q_shifted = [], [], []
    for c, ((h, r), q) in enumerate(zip(chains, qs)):
        pieces = []
        for d in range(sub // dsub):
            row0 = r * sub + d * dsub
            nk = row0 + dsub
            kt = k_ref[pl.ds(q0, nk), h * LANES:(h + 1) * LANES]
            vt = v_ref[pl.ds(q0, nk), h * LANES:(h + 1) * LANES]
            qd = q[d * dsub:(d + 1) * dsub]
            s = _dot_nt(qd, kt)
            kpos = lax.broadcasted_iota(jnp.int32, (dsub, nk), 1)
            tpos = row0 + lax.broadcasted_iota(jnp.int32, (dsub, nk), 0)
            s = jnp.where(kpos <= tpos, s, -jnp.inf)
            m_diag = jnp.max(s, axis=-1, keepdims=True)
            bound = q_norm[c][d * dsub:(d + 1) * dsub] * k_norm[h] + cum_q0[h]
            shift = jnp.where(tight, jnp.maximum(m_diag, bound), m_diag)
            hi = shift.astype(BF16)
            rest = shift - hi.astype(F32)
            lo = (rest + jnp.abs(rest) * (2.0 ** -7)).astype(BF16)
            shift = hi.astype(F32) + lo.astype(F32)
            acc = _dot(jnp.exp2(s - shift).astype(BF16), vt)
            qsh = jnp.where(lane == HEAD_DIM + N_CUM_PARTS, -hi,
                            jnp.where(lane == HEAD_DIM + N_CUM_PARTS + 1, -lo, qd))
            pieces.append((shift, acc, qsh))
        shifts.append(jnp.concatenate([p[0] for p in pieces], axis=0))
        accs.append(jnp.concatenate([p[1] for p in pieces], axis=0))
        q_shifted.append(jnp.concatenate([p[2] for p in pieces], axis=0))

    def may_matter(jt):
        return jnp.maximum(ncum_ref[0, jt + 1] - cum_q0[0], ncum_ref[1, jt + 1] - cum_q0[1]) >= FOX_DEAD

    def fixed_shift_sweep(accs):
        def body(carry):
            jt, accs = carry
            out = []
            for (h, r), q, acc in zip(chains, q_shifted, accs):
                kt, vt = tiles(jt, h)
                out.append(acc + _dot(jnp.exp2(_dot_nt(q, kt)).astype(BF16), vt))
            return jt - 1, tuple(out)

        return lax.while_loop(lambda cr: (cr[0] >= 0) & may_matter(jnp.maximum(cr[0], 0)), body,
                              (n_full - 1, accs))[1]

    def running_max_sweep(accs):
        def alive(jt, ms):
            go = None
            for h in range(2):
                m_min = functools.reduce(jnp.minimum, [jnp.min(ms[h * nsub + r]) for r in range(nsub)])
                live = qk_bound[h] + ncum_ref[h, jt + 1] - m_min >= FOX_DEAD
                go = live if go is None else (go | live)
            return go

        def body(carry):
            jt, _, ms, accs = carry
            new_m, new_acc = [], []
            for (h, r), q, m, acc in zip(chains, qs, ms, accs):
                kt, vt = tiles(jt, h)
                s = _dot_nt(q, kt)
                m_new = jnp.maximum(m, jnp.max(s, axis=-1, keepdims=True))
                new_acc.append(jnp.exp2(m - m_new) * acc + _dot(jnp.exp2(s - m_new).astype(BF16), vt))
                new_m.append(m_new)
            new_m = tuple(new_m)
            return jt - 1, alive(jnp.maximum(jt - 1, 0), new_m), new_m, tuple(new_acc)

        ms = tuple(shifts)
        return lax.while_loop(lambda cr: (cr[0] >= 0) & cr[1], body,
                              (n_full - 1, alive(jnp.maximum(n_full - 1, 0), ms), ms, accs))[3]

    accs = lax.cond(tight, fixed_shift_sweep, running_max_sweep, tuple(accs))
    heads = []
    for h in range(2):
        acc = jnp.concatenate([accs[h * nsub + r] for r in range(nsub)], axis=0)
        heads.append(acc * (1.0 / acc[:, HEAD_DIM:HEAD_DIM + 1]))
    o_ref[...] = jnp.where(feat, heads[0], pltpu.roll(heads[1], HEAD_DIM, 1)).astype(BF16)


def _fox_attention(q, k, v, stats, batch, seq, tq, tk, sub, dsub):
    t = q.shape[0]
    nq = seq // tq
    kv = pl.BlockSpec((seq, 2 * LANES), lambda b, p, i: (b, p))
    assert stats.shape[0] * tk == 8 * t
    tile_stats = stats.reshape(t // tk, 8, LANES)
    tile_stats = jnp.concatenate([tile_stats[:, 0, :FOX_HEADS], tile_stats[:, 1, :FOX_HEADS]], axis=1)
    return pl.pallas_call(
        functools.partial(_fox_kernel, tq=tq, tk=tk, sub=sub, dsub=dsub, seq=seq),
        grid=(batch, FOX_HEADS // 2, nq),
        in_specs=[pl.BlockSpec(memory_space=pltpu.SMEM),
                  pl.BlockSpec((tq, 2 * LANES), lambda b, p, i: (b * nq + i, p)), kv, kv],
        out_specs=pl.BlockSpec((tq, LANES), lambda b, p, i: (b * nq + i, p)),
        out_shape=jax.ShapeDtypeStruct((t, FOX_W), BF16),
        scratch_shapes=[pltpu.SMEM((2,), F32), pltpu.SMEM((2, seq // tk), F32)],
        compiler_params=_params(("parallel", "parallel", "arbitrary")),
        name="fox_attention",
    )(tile_stats, q, k, v)


FFN_CHUNK = 256
HALO = 8


def _ffn_kernel(*refs, tm, nsb, final, n_attn):
    attn_refs, wattn_ref, refs = refs[:n_attn], refs[n_attn], refs[n_attn + 1:]
    if final:
        x_ref, g_ref, win_ref, cw_ref, cb_ref, wout_ref, fn_ref, o_ref, a_scr, halo_scr = refs
    else:
        x_ref, g_ref, win_ref, cw_ref, cb_ref, wout_ref, o_ref, a_scr, halo_scr = refs
    i = pl.program_id(0)

    @pl.when(i % nsb == 0)
    def _():
        halo_scr[...] = jnp.zeros((HALO, D_FF), F32)

    x = x_ref[...]
    off = 0
    for a_ref in attn_refs:
        width = a_ref.shape[1]
        x = x + _dot(a_ref[...], wattn_ref[off:off + width, :])
        off += width
    hb = _rmsnorm(x, g_ref[...]).astype(BF16)
    acc = jnp.zeros((tm, D_MODEL), F32)
    for c in range(D_FF // FFN_CHUNK):
        sl = slice(c * FFN_CHUNK, (c + 1) * FFN_CHUNK)
        a = _dot(hb, win_ref[:, sl])
        b = _dot(hb, win_ref[:, D_FF + c * FFN_CHUNK:D_FF + (c + 1) * FFN_CHUNK])
        a_scr[0:HALO, :] = halo_scr[:, sl]
        a_scr[HALO:HALO + tm, :] = a
        halo_scr[:, sl] = a[tm - HALO:tm, :]
        conv = (cw_ref[0:1, sl] * a_scr[HALO - 2:HALO - 2 + tm, :]
                + cw_ref[1:2, sl] * a_scr[HALO - 1:HALO - 1 + tm, :]
                + cw_ref[2:3, sl] * a + cb_ref[:, sl])
        gated = conv * (1.0 / (1.0 + jnp.exp(-conv))) * b
        acc = acc + _dot(gated.astype(BF16), wout_ref[sl, :])
    y = x + acc
    if final:
        y = _rmsnorm(y, fn_ref[...])
    o_ref[...] = y


def _mixer_out_ffn(xf, attn_outs, w_attn, g, w_in, conv_w, conv_b, w_out, final_norm, seq, tm):
    t, d = xf.shape
    nsb = seq // tm
    row = pl.BlockSpec((tm, d), lambda i: (i, 0))
    const = lambda shape: pl.BlockSpec(shape, lambda i: (0, 0), pipeline_mode=pl.Buffered(1))
    small = lambda shape: pl.BlockSpec(shape, lambda i: (0, 0))
    final = final_norm is not None
    in_specs = [pl.BlockSpec((tm, a.shape[1]), lambda i: (i, 0)) for a in attn_outs] + [const(w_attn.shape)]
    in_specs += [row, small((1, d)), const(w_in.shape), small(conv_w.shape), small((1, D_FF)), const(w_out.shape)]
    args = list(attn_outs) + [w_attn, xf, g, w_in, conv_w, conv_b, w_out]
    if final:
        in_specs.append(small((1, d)))
        args.append(final_norm)
    return pl.pallas_call(
        functools.partial(_ffn_kernel, tm=tm, nsb=nsb, final=final, n_attn=len(attn_outs)),
        grid=(t // tm,),
        in_specs=in_specs,
        out_specs=row,
        out_shape=jax.ShapeDtypeStruct((t, d), F32),
        scratch_shapes=[pltpu.VMEM((tm + HALO, FFN_CHUNK), F32), pltpu.VMEM((HALO, D_FF), F32)],
        compiler_params=_params(("arbitrary",)),
        name="conv_glu_ffn",
    )(*args)


def kernel(x, attn_norm, ffn_norm, ev_w_in, ev_cmp_pos_k, ev_cmp_pos_v, ev_cmp_w_k, ev_cmp_w_v, ev_w_out,
           od_w_in, od_b_f, od_w_out, ffn_w_in, ffn_conv_w, ffn_conv_b, ffn_w_out, final_norm):
    batch, seq, d = x.shape
    t = batch * seq
    depth = attn_norm.shape[0]
    tm, ffn_tm = min(512, seq), min(512, seq)
    sb_tile, sb_nsub = min(256, seq), 2
    fox_tq, fox_tk, fox_sub, fox_dsub = min(512, seq), min(512, seq), 512, 512
    sel_tk = min(512, seq)

    xf = x.reshape(t, d)
    tabs = _rope_tables(jnp.arange(seq))
    ctabs = _rope_tables(jnp.arange(seq // CMP_STRIDE) * CMP_STRIDE + (CMP_LEN - 1))
    mt, oh = _nsa_consts(seq)

    for layer in range(depth):
        g_attn = attn_norm[layer].reshape(1, d)
        if layer % 2 == 0:
            e = layer // 2
            (sbq, sbk, sbv, nq, kc_raw, vc_raw, ksl, vsl, kwn, vwn, gates) = _even_proj(
                xf, g_attn, _prep_even_w(ev_w_in[e]), tabs, seq, tm)
            pek, wk = _prep_cmp(ev_cmp_pos_k[e], ev_cmp_w_k[e])
            pev, wv = _prep_cmp(ev_cmp_pos_v[e], ev_cmp_w_v[e])
            kc, vc = _compress(kc_raw, vc_raw, pek, pev, wk, wv, ctabs, batch, seq)
            o_sb = _sb_attention(sbq, sbk, sbv, batch, seq, sb_tile, sb_nsub)
            o_nsa = _nsa_attention(nq, kc, vc, ksl, vsl, kwn, vwn, gates, mt, oh, batch, seq, sel_tk)
            attn_outs, w_attn = [o_sb, o_nsa], ev_w_out[e].astype(BF16)
        else:
            o = layer // 2
            w = jnp.pad(od_w_in[o], ((0, 0), (0, LANES - FOX_HEADS))).astype(BF16)
            bf = jnp.pad(od_b_f[o], (0, LANES - FOX_HEADS)).reshape(1, LANES)
            q, k, v, stats = _odd_proj(xf, g_attn, w, bf, seq, tm)
            o_fox = _fox_attention(q, k, v, stats, batch, seq, fox_tq, fox_tk, fox_sub, fox_dsub)
            attn_outs, w_attn = [o_fox], od_w_out[o].astype(BF16)
        last = layer == depth - 1
        xf = _mixer_out_ffn(xf, attn_outs, w_attn, ffn_norm[layer].reshape(1, d), ffn_w_in[layer].astype(BF16),
                            ffn_conv_w[layer], ffn_conv_b[layer].reshape(1, D_FF), ffn_w_out[layer].astype(BF16),
                            final_norm.reshape(1, d) if last else None, seq, ffn_tm)
    return xf.reshape(batch, seq, d)
```

```python
import functools
import math

import jax
import jax.numpy as jnp
from jax import lax
from jax.experimental import pallas as pl
from jax.experimental.pallas import tpu as pltpu

F32, BF16 = jnp.float32, jnp.bfloat16

D_MODEL = 1024
HEAD_DIM = 64
N_HEADS = D_MODEL // HEAD_DIM
SB_HEADS = N_HEADS // 2
NSA_HEADS = N_HEADS - SB_HEADS
NSA_KV_GROUPS = 2
NSA_REP = NSA_HEADS // NSA_KV_GROUPS
FOX_HEADS = N_HEADS
CMP_LEN = 32
CMP_STRIDE = 16
SEL_LEN = 64
SEL_TOP = 16
WINDOW = 512
N_BRANCH = 3
ROPE_THETA = 500000.0
ROT_DIM = HEAD_DIM // 4
D_FF = 2816
CONV_WIDTH = 3
NORM_EPS = 1e-6
SB_W = SB_HEADS * HEAD_DIM
NSA_QW = NSA_HEADS * HEAD_DIM
NSA_KVW = NSA_KV_GROUPS * HEAD_DIM
FOX_W = FOX_HEADS * HEAD_DIM

LANES = 128
SEL_LANES = 128
QK_SCALE = HEAD_DIM ** -0.5
LOG2E = math.log2(math.e)
SEL_NEG = -(2.0 ** 30)
NSA_TIGHT = 86.0
NORM_SLACK = 1.001
VMEM_LIMIT = 56 * 2 ** 20

_NT = (((1,), (1,)), ((), ()))


def _params(sem):
    return pltpu.CompilerParams(dimension_semantics=sem, vmem_limit_bytes=VMEM_LIMIT)


def _dot(a, b):
    return jnp.dot(a, b, preferred_element_type=F32)


def _dot_nt(a, b):
    return lax.dot_general(a, b, _NT, preferred_element_type=F32)


def _rmsnorm(x, g):
    ms = jnp.mean(x * x, axis=-1, keepdims=True)
    return (x * lax.rsqrt(ms + NORM_EPS)) * g


def _rope(y, c, s1, s2):
    return y * c + pltpu.roll(y, LANES - ROT_DIM // 2, 1) * s1 + pltpu.roll(y, ROT_DIM // 2, 1) * s2


def _split3(x):
    hi = x.astype(BF16)
    r1 = x - hi.astype(F32)
    mid = r1.astype(BF16)
    lo = (r1 - mid.astype(F32)).astype(BF16)
    return hi, mid, lo


def _rope_tables(pos):
    half = ROT_DIM // 2
    inv_freq = ROPE_THETA ** (-(jnp.arange(half, dtype=F32) * 2.0 / ROT_DIM))
    ang = pos.astype(F32)[:, None] * inv_freq[None, :]
    cos, sin = jnp.cos(ang), jnp.sin(ang)
    n = pos.shape[0]
    one = jnp.ones((n, HEAD_DIM - ROT_DIM), F32)
    zero = jnp.zeros((n, HEAD_DIM - ROT_DIM), F32)
    z8 = jnp.zeros((n, half), F32)
    c = jnp.concatenate([cos, cos, one], -1)
    s1 = jnp.concatenate([-sin, z8, zero], -1)
    s2 = jnp.concatenate([z8, sin, zero], -1)
    two = lambda t: jnp.concatenate([t, t], -1)
    return two(c), two(s1), two(s2)


def _even_proj_kernel(x_ref, g_ref, w_ref, c_ref, s1_ref, s2_ref,
                      sbq_ref, sbk_ref, sbv_ref, nq_ref, kc_ref, vc_ref,
                      ksl_ref, vsl_ref, kwn_ref, vwn_ref, gate_ref):
    hb = _rmsnorm(x_ref[...], g_ref[...]).astype(BF16)
    c, s1, s2 = c_ref[...], s1_ref[...], s2_ref[...]

    sb = _dot(hb, w_ref[:, 0:3 * SB_W])
    sbq_ref[...] = (sb[:, 0:SB_W] * QK_SCALE).astype(BF16)
    sbk_ref[...] = sb[:, SB_W:2 * SB_W].astype(BF16)
    sbv_ref[...] = sb[:, 2 * SB_W:3 * SB_W].astype(BF16)
    nsa = _dot(hb, w_ref[:, 3 * SB_W:])
    piece = lambda j: nsa[:, j * LANES:(j + 1) * LANES]
    n_q = NSA_QW // LANES
    for j in range(n_q):
        nq_ref[:, j * LANES:(j + 1) * LANES] = (_rope(piece(j), c, s1, s2) * (QK_SCALE * LOG2E)).astype(BF16)
    kc_ref[...] = piece(n_q)
    vc_ref[...] = piece(n_q + 1)
    ksl_ref[...] = _rope(piece(n_q + 2), c, s1, s2).astype(BF16)
    vsl_ref[...] = piece(n_q + 3).astype(BF16)
    kwn_ref[...] = _rope(piece(n_q + 4), c, s1, s2).astype(BF16)
    vwn_ref[...] = piece(n_q + 5).astype(BF16)
    gate_ref[...] = 1.0 / (1.0 + jnp.exp(-nsa[:, (n_q + 6) * LANES:(n_q + 8) * LANES]))


def _even_proj(xf, g, w, tabs, seq, tm):
    t, d = xf.shape
    nsb = seq // tm
    n = w.shape[1]
    row = lambda width: pl.BlockSpec((tm, width), lambda i: (i, 0))
    tab = pl.BlockSpec((tm, LANES), lambda i: (i % nsb, 0))
    out_shape = (
        [jax.ShapeDtypeStruct((t, SB_W), BF16)] * 3
        + [jax.ShapeDtypeStruct((t, NSA_QW), BF16)]
        + [jax.ShapeDtypeStruct((t, LANES), F32)] * 2
        + [jax.ShapeDtypeStruct((t, LANES), BF16)] * 4
        + [jax.ShapeDtypeStruct((t, 2 * LANES), F32)]
    )
    out_specs = [row(SB_W)] * 3 + [row(NSA_QW)] + [row(LANES)] * 6 + [row(2 * LANES)]
    return pl.pallas_call(
        _even_proj_kernel,
        grid=(t // tm,),
        in_specs=[row(d), pl.BlockSpec((1, d), lambda i: (0, 0)),
                  pl.BlockSpec((d, n), lambda i: (0, 0)), tab, tab, tab],
        out_specs=out_specs,
        out_shape=out_shape,
        compiler_params=_params(("parallel",)),
        name="even_proj",
    )(xf, g, w, *tabs)


def _prep_even_w(w):
    d = w.shape[0]
    main = w[:, :3 * SB_W + NSA_QW + 6 * NSA_KVW]
    gates = w[:, 3 * SB_W + NSA_QW + 6 * NSA_KVW:].reshape(d, NSA_KV_GROUPS, NSA_REP, N_BRANCH)
    gates = jnp.transpose(gates, (0, 1, 3, 2)).reshape(d, NSA_KV_GROUPS, N_BRANCH * NSA_REP)
    gates = jnp.pad(gates, ((0, 0), (0, 0), (0, LANES - N_BRANCH * NSA_REP)))
    return jnp.concatenate([main, gates.reshape(d, NSA_KV_GROUPS * LANES)], axis=1).astype(BF16)


SB_DEAD = -105.0


def _sb_kernel(q_ref, k_ref, v_ref, o_ref, *, tile, nsub):
    i = pl.program_id(2)
    lane = lax.broadcasted_iota(jnp.int32, (1, LANES), 1)
    r = lax.broadcasted_iota(jnp.int32, (tile, tile), 0)
    c = lax.broadcasted_iota(jnp.int32, (tile, tile), 1)
    later = jnp.where(r > c, 1.0, 0.0).astype(BF16)
    diag = c < r
    chains = [(h, sub) for h in range(2) for sub in range(nsub)]
    qs = []
    for h, sub in chains:
        q = q_ref[sub * tile:(sub + 1) * tile, :]
        qs.append(jnp.where((lane < HEAD_DIM) if h == 0 else (lane >= HEAD_DIM), q, jnp.zeros_like(q)))

    def step(jt, state, mode):
        k0 = pl.multiple_of(jt * tile, tile)
        kt = k_ref[pl.ds(k0, tile), :]
        vt = v_ref[pl.ds(k0, tile), :]
        out = []
        for c, (acc, keep_sum) in enumerate(state):
            if mode[c] is None:
                out.append((acc, keep_sum))
                continue
            z = _dot_nt(qs[c], kt)
            ls = jnp.minimum(z, 0.0) - jnp.log(1.0 + jnp.exp(-jnp.abs(z)))
            lk = ls - z
            if mode[c]:
                lk = jnp.where(diag, lk, 0.0)
            hi = lk.astype(BF16)
            lo = (lk - hi.astype(F32)).astype(BF16)
            both = _dot(jnp.concatenate([hi, lo], axis=0), later)
            after = both[0:tile] + both[tile:2 * tile]
            a = jnp.exp(ls + after + keep_sum)
            if mode[c]:
                a = jnp.where(diag, a, 0.0)
            out.append((acc + _dot(a.astype(BF16), vt), keep_sum + after[:, 0:1] + lk[:, 0:1]))
        return tuple(out)

    def alive(state):
        return functools.reduce(jnp.maximum, [jnp.max(keep_sum) for _, keep_sum in state]) >= SB_DEAD

    state = tuple((jnp.zeros((tile, LANES), F32), jnp.zeros((tile, 1), F32)) for _ in chains)
    for top in range(nsub - 1, -1, -1):
        mode = [None if sub < top else sub == top for _, sub in chains]
        state = step(nsub * i + top, state, mode)

    def body(carry):
        jt, _, state = carry
        state = step(jt, state, [False] * len(chains))
        return jt - 1, alive(state), state

    _, _, state = lax.while_loop(lambda cr: (cr[0] >= 0) & cr[1], body, (nsub * i - 1, alive(state), state))
    heads = [jnp.concatenate([state[h * nsub + sub][0] for sub in range(nsub)], axis=0) for h in range(2)]
    o_ref[...] = jnp.where(lane < HEAD_DIM, heads[0], heads[1]).astype(BF16)


def _sb_attention(q, k, v, batch, seq, tile, nsub):
    t, w = q.shape
    nq = seq // (tile * nsub)
    kv = pl.BlockSpec((seq, LANES), lambda b, p, i: (b, p))
    qo = pl.BlockSpec((tile * nsub, LANES), lambda b, p, i: (b * nq + i, p))
    return pl.pallas_call(
        functools.partial(_sb_kernel, tile=tile, nsub=nsub),
        grid=(batch, w // LANES, nq),
        in_specs=[qo, kv, kv],
        out_specs=qo,
        out_shape=jax.ShapeDtypeStruct((t, w), BF16),
        compiler_params=_params(("parallel", "parallel", "arbitrary")),
        name="sb_attention",
    )(q, k, v)


def _compress_kernel(xk_ref, xv_ref, pek_ref, pev_ref, wk_ref, wv_ref, c_ref, s1_ref, s2_ref,
                     kc_ref, vc_ref, *, nc):
    def comp(x_ref, pe_ref, w_ref):
        x = x_ref[0]
        top = _dot((x + pe_ref[0:1, :]).astype(BF16), w_ref[0])
        bot = _dot((x + pe_ref[1:2, :]).astype(BF16), w_ref[1])
        return top + pltpu.roll(bot, nc - 1, 0)

    kc = _rope(comp(xk_ref, pek_ref, wk_ref), c_ref[...], s1_ref[...], s2_ref[...])
    kc_ref[0] = kc.astype(BF16)
    vc_ref[0] = comp(xv_ref, pev_ref, wv_ref).astype(BF16)


def _prep_cmp(pe, w):
    half = CMP_LEN // 2
    w3 = w.reshape(CMP_LEN, HEAD_DIM, HEAD_DIM)
    eye = jnp.eye(NSA_KV_GROUPS, dtype=w.dtype)
    parts, pes = [], []
    for s in range(2):
        wh = jnp.einsum('ldo,gh->lgdho', w3[s * half:(s + 1) * half], eye)
        parts.append(wh.reshape(half * NSA_KVW, NSA_KVW))
        pes.append(jnp.broadcast_to(pe[s * half:(s + 1) * half, None, :],
                                    (half, NSA_KV_GROUPS, HEAD_DIM)).reshape(1, half * NSA_KVW))
    return jnp.concatenate(pes, 0), jnp.stack(parts).astype(BF16)


def _compress(kc_raw, vc_raw, pek, pev, wk, wv, ctabs, batch, seq):
    nc = seq // CMP_STRIDE
    cw = CMP_STRIDE * NSA_KVW
    xk = kc_raw.reshape(batch, nc, cw)
    xv = vc_raw.reshape(batch, nc, cw)
    xs = pl.BlockSpec((1, nc, cw), lambda b: (b, 0, 0))
    pes = pl.BlockSpec((2, cw), lambda b: (0, 0))
    ws = pl.BlockSpec((2, cw, NSA_KVW), lambda b: (0, 0, 0))
    tab = pl.BlockSpec((nc, LANES), lambda b: (0, 0))
    out = pl.BlockSpec((1, nc, LANES), lambda b: (b, 0, 0))
    return pl.pallas_call(
        functools.partial(_compress_kernel, nc=nc),
        grid=(batch,),
        in_specs=[xs, xs, pes, pes, ws, ws, tab, tab, tab],
        out_specs=[out, out],
        out_shape=[jax.ShapeDtypeStruct((batch, nc, LANES), BF16)] * 2,
        compiler_params=_params(("parallel",)),
        name="nsa_compress",
    )(xk, xv, pek, pev, wk, wv, *ctabs)


QB = 128


def _softmax_rows(s):
    m = jnp.max(s, axis=-1, keepdims=True)
    m = jnp.where(m > -jnp.inf, m, 0.0)
    e = jnp.exp2(s - m)
    d = jnp.sum(e, axis=-1, keepdims=True)
    return e / jnp.where(d > 0, d, 1.0)


def _nsa_kernel(q_ref, kc_ref, vc_ref, ks_ref, vs_ref, kw_ref, vw_ref, g_ref, mt_ref, oh_ref,
                o_ref, score_ref, knorm_ref, *, nc, tk, wlen, seq):
    g = pl.program_id(1)
    i = pl.program_id(2)
    q0 = i * QB
    rows = NSA_REP * QB

    q2 = q_ref[...]
    pr = lax.broadcasted_iota(jnp.int32, (NSA_REP * HEAD_DIM, LANES), 0)
    pc = lax.broadcasted_iota(jnp.int32, (NSA_REP * HEAD_DIM, LANES), 1) - g * HEAD_DIM
    in_group = (pc >= 0) & (pc < HEAD_DIM)
    q4 = jnp.concatenate(
        [_dot(q2, jnp.where(in_group & (pr - hh * HEAD_DIM == pc), 1.0, 0.0).astype(BF16)).astype(BF16)
         for hh in range(NSA_REP)], axis=0)

    sc = _dot_nt(q4, kc_ref[0])
    cend = lax.broadcasted_iota(jnp.int32, (rows, nc), 1) * CMP_STRIDE + (CMP_LEN - 1)
    tpos_c = q0 + (lax.broadcasted_iota(jnp.int32, (rows, nc), 0) & (QB - 1))
    p_cmp = _softmax_rows(jnp.where(cend <= tpos_c, sc, -jnp.inf))
    o_cmp = _dot(p_cmp.astype(BF16), vc_ref[0])

    p_sum = p_cmp[0:QB] + p_cmp[QB:2 * QB] + p_cmp[2 * QB:3 * QB] + p_cmp[3 * QB:4 * QB]
    overlap = mt_ref[...]
    parts = _dot(jnp.concatenate(_split3(p_sum), axis=0), overlap)
    imp_t = jnp.transpose(parts[0:QB] + parts[QB:2 * QB] + parts[2 * QB:3 * QB])
    nidx = lax.broadcasted_iota(jnp.int32, (SEL_LANES, QB), 0)
    cur = (q0 + lax.broadcasted_iota(jnp.int32, (SEL_LANES, QB), 1)) // SEL_LEN
    causal = nidx <= cur
    forced = (nidx == 0) | (nidx == cur) | (nidx == cur - 1)
    score = jnp.where(causal, jnp.where(forced, jnp.inf, imp_t), -jnp.inf)
    score_ref[...] = score

    def larger_step(m, cnt):
        return cnt + jnp.where(score_ref[pl.ds(m, 1), :] > score, 1.0, 0.0)

    def larger_pair(m2, cnt):
        return larger_step(2 * m2 + 1, larger_step(2 * m2, cnt))

    def rank_step(m, cnt):
        rowm = score_ref[pl.ds(m, 1), :]
        ge = jnp.where(rowm >= score, 1.0, 0.0)
        gt = jnp.where(rowm > score, 1.0, 0.0)
        return cnt + jnp.where(nidx > m, ge, gt)

    n_causal = (q0 + QB) // SEL_LEN
    zero_cnt = jnp.zeros((SEL_LANES, QB), F32)
    cnt = lax.fori_loop(0, n_causal // 2, larger_pair, zero_cnt)
    n_top = jnp.sum(jnp.where(causal & (cnt < float(SEL_TOP)), 1.0, 0.0), axis=0, keepdims=True)
    cnt = lax.cond(jnp.max(n_top) > float(SEL_TOP),
                   lambda c: lax.fori_loop(0, n_causal, rank_step, zero_cnt), lambda c: c, cnt)
    bias_t = jnp.where(causal & (cnt < float(SEL_TOP)), 0.0, SEL_NEG)
    selb4 = jnp.concatenate([jnp.transpose(bias_t)] * NSA_REP, axis=0)

    lane = lax.broadcasted_iota(jnp.int32, (1, LANES), 1)
    own_lanes = (lane >= g * HEAD_DIM) & (lane < (g + 1) * HEAD_DIM)

    @pl.when(i == 0)
    def _():
        feat = lax.broadcasted_iota(jnp.int32, (LANES, 1), 0)
        own_rows = (feat >= g * HEAD_DIM) & (feat < (g + 1) * HEAD_DIM)

        def chunk(n, best):
            kf = ks_ref[0, :, pl.ds(pl.multiple_of(n * tk, tk), tk)].astype(F32)
            kf = jnp.where(own_rows, kf, 0.0)
            return jnp.maximum(best, jnp.max(jnp.sum(kf * kf, axis=0, keepdims=True)))
        knorm_ref[0] = jnp.sqrt(lax.fori_loop(0, seq // tk, chunk, jnp.float32(0.0)))

    q4f = q4.astype(F32)
    shift = jnp.sqrt(jnp.sum(q4f * q4f, axis=-1, keepdims=True)) * (knorm_ref[0] * NORM_SLACK)
    shift = (shift * (1.0 + 2.0 ** -7)).astype(BF16).astype(F32)
    tight = 2.0 * jnp.max(shift) <= NSA_TIGHT
    qs = jnp.concatenate([q4, jnp.where(tight, selb4 - shift, selb4).astype(BF16)], axis=1)
    n_full = q0 // tk

    def sel_tiles(jt):
        k0 = pl.multiple_of(jt * tk, tk)
        kk = jnp.concatenate([ks_ref[0, :, pl.ds(k0, tk)], oh_ref[:, pl.ds(k0, tk)]], axis=0)
        vt = vs_ref[pl.ds(k0, tk), :]
        return kk, jnp.where(own_lanes, vt, jnp.ones_like(vt))

    def scores(jt, masked):
        kk, vt = sel_tiles(jt)
        s = _dot(qs, kk)
        if masked:
            kpos = jt * tk + lax.broadcasted_iota(jnp.int32, (rows, tk), 1)
            tpos = q0 + (lax.broadcasted_iota(jnp.int32, (rows, tk), 0) & (QB - 1))
            s = jnp.where(kpos <= tpos, s, -jnp.inf)
        return s, vt

    def fixed_shift_sweep(acc):
        def step(jt, acc, masked):
            s, vt = scores(jt, masked)
            return acc + _dot(jnp.exp2(s).astype(BF16), vt)
        acc = lax.fori_loop(0, n_full, lambda jt, a: step(jt, a, False), acc)
        return step(n_full, acc, True)

    def running_max_sweep(acc):
        def step(jt, carry, masked):
            m, acc = carry
            s, vt = scores(jt, masked)
            m_new = jnp.maximum(m, jnp.max(s, axis=-1, keepdims=True))
            return m_new, jnp.exp2(m - m_new) * acc + _dot(jnp.exp2(s - m_new).astype(BF16), vt)
        carry = lax.fori_loop(0, n_full, lambda jt, cr: step(jt, cr, False),
                              (jnp.full((rows, 1), -jnp.inf, F32), acc))
        return step(n_full, carry, True)[1]

    acc_s = lax.cond(tight, fixed_shift_sweep, running_max_sweep, jnp.zeros((rows, LANES), F32))
    l_s = jnp.where(g == 0, acc_s[:, HEAD_DIM:HEAD_DIM + 1], acc_s[:, 0:1])
    o_slc = acc_s / l_s

    w0 = pl.multiple_of(jnp.maximum(q0 - WINDOW, 0), QB)
    sw = _dot_nt(q4, kw_ref[pl.ds(w0, wlen), :])
    kpos = w0 + lax.broadcasted_iota(jnp.int32, (rows, wlen), 1)
    tpos = q0 + (lax.broadcasted_iota(jnp.int32, (rows, wlen), 0) & (QB - 1))
    p_win = _softmax_rows(jnp.where((kpos <= tpos) & (kpos > tpos - WINDOW), sw, -jnp.inf))
    o_win = _dot(p_win.astype(BF16), vw_ref[pl.ds(w0, wlen), :])

    gates = g_ref[...]
    orow = lax.broadcasted_iota(jnp.int32, (LANES, NSA_REP * HEAD_DIM), 0) - g * HEAD_DIM
    ocol = lax.broadcasted_iota(jnp.int32, (LANES, NSA_REP * HEAD_DIM), 1)
    o_group = (orow >= 0) & (orow < HEAD_DIM)
    out = jnp.zeros((QB, NSA_REP * HEAD_DIM), F32)
    for hh in range(NSA_REP):
        sl = slice(hh * QB, (hh + 1) * QB)
        gate = lambda br: gates[:, br * NSA_REP + hh:br * NSA_REP + hh + 1]
        o_h = gate(0) * o_cmp[sl] + gate(1) * o_slc[sl] + gate(2) * o_win[sl]
        place = jnp.where(o_group & (ocol - hh * HEAD_DIM == orow), 1.0, 0.0).astype(BF16)
        out = out + _dot(o_h.astype(BF16), place)
    o_ref[...] = out.astype(BF16)


def _nsa_consts(seq):
    nc = seq // CMP_STRIDE
    c_start = jnp.arange(nc) * CMP_STRIDE
    sel_start = jnp.arange(SEL_LANES) * SEL_LEN
    real = (jnp.arange(SEL_LANES) < seq // SEL_LEN)[:, None] & (jnp.arange(nc) < (seq - CMP_LEN) // CMP_STRIDE + 1)[None, :]
    overlap = (c_start[None, :] < sel_start[:, None] + SEL_LEN) & (c_start[None, :] + CMP_LEN > sel_start[:, None])
    mt = jnp.transpose(overlap & real).astype(BF16)
    oh = (jnp.arange(seq)[None, :] // SEL_LEN == jnp.arange(SEL_LANES)[:, None]).astype(BF16)
    return mt, oh


def _nsa_attention(nq, kc, vc, ksl, vsl, kwn, vwn, gates, mt, oh, batch, seq, tk):
    t = nq.shape[0]
    nqb = seq // QB
    nc = seq // CMP_STRIDE
    wlen = WINDOW + QB
    gw = NSA_REP * HEAD_DIM
    qspec = pl.BlockSpec((QB, gw), lambda b, g, i: (b * nqb + i, g))
    cspec = pl.BlockSpec((1, nc, LANES), lambda b, g, i: (b, 0, 0))
    kvspec = pl.BlockSpec((seq, LANES), lambda b, g, i: (b, 0))
    gspec = pl.BlockSpec((QB, LANES), lambda b, g, i: (b * nqb + i, g))
    return pl.pallas_call(
        functools.partial(_nsa_kernel, nc=nc, tk=tk, wlen=wlen, seq=seq),
        grid=(batch, NSA_KV_GROUPS, nqb),
        in_specs=[qspec, cspec, cspec, pl.BlockSpec((1, LANES, seq), lambda b, g, i: (b, 0, 0)),
                  kvspec, kvspec, kvspec, gspec,
                  pl.BlockSpec((nc, SEL_LANES), lambda b, g, i: (0, 0)),
                  pl.BlockSpec((SEL_LANES, seq), lambda b, g, i: (0, 0))],
        out_specs=qspec,
        out_shape=jax.ShapeDtypeStruct((t, NSA_QW), BF16),
        scratch_shapes=[pltpu.VMEM((SEL_LANES, QB), F32), pltpu.SMEM((1,), F32)],
        compiler_params=_params(("parallel", "parallel", "arbitrary")),
        name="nsa_attention",
    )(nq, kc, vc, jnp.swapaxes(ksl.reshape(batch, seq, LANES), 1, 2), vsl, kwn, vwn, gates, mt, oh)


N_CUM_PARTS = 3


def _odd_proj_kernel(x_ref, g_ref, w_ref, bf_ref, place_ref, q_ref, k_ref, v_ref, stats_ref, carry_ref,
                     *, tm, nsb):
    i = pl.program_id(0)
    hb = _rmsnorm(x_ref[...], g_ref[...]).astype(BF16)
    f = _dot(hb, w_ref[:, 3 * FOX_W:3 * FOX_W + LANES]) + bf_ref[...]
    log_f = jnp.minimum(f, 0.0) - jnp.log1p(jnp.exp(-jnp.abs(f)))
    r = lax.broadcasted_iota(jnp.int32, (tm, tm), 0)
    c = lax.broadcasted_iota(jnp.int32, (tm, tm), 1)
    tri = jnp.where(c <= r, 1.0, 0.0).astype(BF16)
    local = _dot(tri, jnp.concatenate(_split3(log_f), axis=1))
    local = local[:, 0:LANES] + local[:, LANES:2 * LANES] + local[:, 2 * LANES:3 * LANES]

    @pl.when(i % nsb == 0)
    def _():
        carry_ref[...] = jnp.zeros_like(carry_ref)

    cum = local + carry_ref[0:1, :]
    carry_ref[0:1, :] = cum[tm - 1:tm, :]
    lane = lax.broadcasted_iota(jnp.int32, (1, LANES), 1)
    neg_cum = cum * (-LOG2E)
    hi, mid, lo = (part.astype(F32) for part in _split3(neg_cum))
    k_sq = jnp.zeros((1, LANES), F32)
    head_of_lane = jnp.where(lax.broadcasted_iota(jnp.int32, (LANES, LANES), 0) // HEAD_DIM
                             == lax.broadcasted_iota(jnp.int32, (LANES, LANES), 1), 1.0, 0.0).astype(BF16)
    packed = jnp.where(lane < FOX_HEADS, hi,
                       jnp.where(lane < 2 * FOX_HEADS, pltpu.roll(mid, FOX_HEADS, 1),
                                 pltpu.roll(lo, 2 * FOX_HEADS, 1))).astype(BF16)
    low = lane < HEAD_DIM
    q_one = jnp.where((lane >= HEAD_DIM) & (lane < HEAD_DIM + N_CUM_PARTS), 1.0, 0.0)
    v_one = jnp.where(lane >= HEAD_DIM, 1.0, 0.0)
    k_one = jnp.where((lane >= HEAD_DIM + N_CUM_PARTS) & (lane < HEAD_DIM + N_CUM_PARTS + N_SHIFT_PARTS), 1.0, 0.0)
    yq_all = _dot(hb, w_ref[:, 0:FOX_W]) * (QK_SCALE * LOG2E)
    yk_all = _dot(hb, w_ref[:, FOX_W:2 * FOX_W])
    yv_all = _dot(hb, w_ref[:, 2 * FOX_W:3 * FOX_W])
    extras_all = _dot(packed, place_ref[...])
    for p in range(FOX_HEADS // 2):
        pair = slice(p * LANES, (p + 1) * LANES)
        yq, yk, yv = yq_all[:, pair], yk_all[:, pair], yv_all[:, pair]
        pair_sq = jnp.max(_dot((yk * yk * (1.0 + 2.0 ** -7)).astype(BF16), head_of_lane), axis=0, keepdims=True)
        for h in range(2):
            hs = slice((2 * p + h) * LANES, (2 * p + h + 1) * LANES)
            head = (lambda y: y) if h == 0 else (lambda y: pltpu.roll(y, HEAD_DIM, 1))
            kh = jnp.where(low, head(yk), 0.0)
            k_sq = jnp.where(lane == 2 * p + h, pair_sq[:, h:h + 1], k_sq)
            q_ref[:, hs] = (jnp.where(low, head(yq), 0.0) + q_one).astype(BF16)
            k_ref[:, hs] = (kh + extras_all[:, hs] + k_one).astype(BF16)
            v_ref[:, hs] = (jnp.where(low, head(yv), 0.0) + v_one).astype(BF16)
    stats_ref[...] = jnp.zeros(stats_ref.shape, F32)
    stats_ref[0:1, :] = k_sq
    stats_ref[1:2, :] = neg_cum[0:1, :]


def _cum_placement():
    src = jnp.arange(LANES)[:, None]
    dst = jnp.arange(FOX_HEADS * LANES)[None, :]
    n, h = src // FOX_HEADS, src % FOX_HEADS
    return ((n < N_CUM_PARTS) & (dst == h * LANES + HEAD_DIM + n)).astype(BF16)


def _odd_proj(xf, g, w, bf, seq, tm):
    t, d = xf.shape
    nsb = seq // tm
    n = w.shape[1]
    wide = FOX_HEADS * LANES
    row = lambda width: pl.BlockSpec((tm, width), lambda i: (i, 0))
    return pl.pallas_call(
        functools.partial(_odd_proj_kernel, tm=tm, nsb=nsb),
        grid=(t // tm,),
        in_specs=[row(d), pl.BlockSpec((1, d), lambda i: (0, 0)),
                  pl.BlockSpec((d, n), lambda i: (0, 0)), pl.BlockSpec((1, LANES), lambda i: (0, 0)),
                  pl.BlockSpec((LANES, wide), lambda i: (0, 0))],
        out_specs=[row(wide)] * 3 + [pl.BlockSpec((8, LANES), lambda i: (i, 0))],
        out_shape=[jax.ShapeDtypeStruct((t, wide), BF16)] * 3 + [jax.ShapeDtypeStruct((t // tm * 8, LANES), F32)],
        scratch_shapes=[pltpu.VMEM((8, LANES), F32)],
        compiler_params=_params(("arbitrary",)),
        name="odd_proj",
    )(xf, g, w, bf, _cum_placement())


FOX_DEAD = -160.0
FOX_TIGHT = 64.0
N_SHIFT_PARTS = 2


def _fox_kernel(stats_ref, q_ref, k_ref, v_ref, o_ref, knorm_ref, ncum_ref, *, tq, tk, sub, dsub, seq):
    b, pair, i = pl.program_id(0), pl.program_id(1), pl.program_id(2)
    q0 = pl.multiple_of(i * tq, tq)
    n_full = q0 // tk
    nsub = tq // sub
    chains = [(h, r) for h in range(2) for r in range(nsub)]
    qs = [q_ref[r * sub:(r + 1) * sub, h * LANES:(h + 1) * LANES] for h, r in chains]
    lane = lax.broadcasted_iota(jnp.int32, (1, LANES), 1)
    feat = lane < HEAD_DIM
    n_tiles = seq // tk

    def row_norms(x):
        xf = jnp.where(feat, x.astype(F32), 0.0)
        return jnp.sqrt(jnp.sum(xf * xf, axis=-1, keepdims=True))

    @pl.when(i == 0)
    def _():
        for h in range(2):
            def tile_stats(n, best, h=h):
                ncum_ref[h, n] = stats_ref[b * n_tiles + n, FOX_HEADS + 2 * pair + h]
                return jnp.maximum(best, stats_ref[b * n_tiles + n, 2 * pair + h])
            knorm_ref[h] = jnp.sqrt(lax.fori_loop(0, n_tiles, tile_stats, jnp.float32(0.0)))

    k_norm = [knorm_ref[h] * (NORM_SLACK * (1.0 + 2.0 ** -8)) for h in range(2)]
    q_norm = [row_norms(q) for q in qs]
    qk_bound = [functools.reduce(jnp.maximum, [jnp.max(q_norm[h * nsub + r]) for r in range(nsub)]) * k_norm[h]
                for h in range(2)]
    tight = 2.0 * jnp.maximum(qk_bound[0], qk_bound[1]) <= FOX_TIGHT
    cum_q0 = [ncum_ref[h, n_full] for h in range(2)]

    def tiles(jt, h):
        k0 = pl.multiple_of(jt * tk, tk)
        hs = slice(h * LANES, (h + 1) * LANES)
        return k_ref[pl.ds(k0, tk), hs], v_ref[pl.ds(k0, tk), hs]

    shifts, accs, q_shifted = [], [], []
    for c, ((h, r), q) in enumerate(zip(chains, qs)):
        pieces = []
        for d in range(sub // dsub):
            row0 = r * sub + d * dsub
            nk = row0 + dsub
            kt = k_ref[pl.ds(q0, nk), h * LANES:(h + 1) * LANES]
            vt = v_ref[pl.ds(q0, nk), h * LANES:(h + 1) * LANES]
            qd = q[d * dsub:(d + 1) * dsub]
            s = _dot_nt(qd, kt)
            kpos = lax.broadcasted_iota(jnp.int32, (dsub, nk), 1)
            tpos = row0 + lax.broadcasted_iota(jnp.int32, (dsub, nk), 0)
            s = jnp.where(kpos <= tpos, s, -jnp.inf)
            m_diag = jnp.max(s, axis=-1, keepdims=True)
            bound = q_norm[c][d * dsub:(d + 1) * dsub] * k_norm[h] + cum_q0[h]
            shift = jnp.where(tight, jnp.maximum(m_diag, bound), m_diag)
            hi = shift.astype(BF16)
            rest = shift - hi.astype(F32)
            lo = (rest + jnp.abs(rest) * (2.0 ** -7)).astype(BF16)
            shift = hi.astype(F32) + lo.astype(F32)
            acc = _dot(jnp.exp2(s - shift).astype(BF16), vt)
            qsh = jnp.where(lane == HEAD_DIM + N_CUM_PARTS, -hi,
                            jnp.where(lane == HEAD_DIM + N_CUM_PARTS + 1, -lo, qd))
            pieces.append((shift, acc, qsh))
        shifts.append(jnp.concatenate([p[0] for p in pieces], axis=0))
        accs.append(jnp.concatenate([p[1] for p in pieces], axis=0))
        q_shifted.append(jnp.concatenate([p[2] for p in pieces], axis=0))

    def may_matter(jt):
        return jnp.maximum(ncum_ref[0, jt + 1] - cum_q0[0], ncum_ref[1, jt + 1] - cum_q0[1]) >= FOX_DEAD

    def fixed_shift_sweep(accs):
        def body(carry):
            jt, accs = carry
            out = []
            for (h, r), q, acc in zip(chains, q_shifted, accs):
                kt, vt = tiles(jt, h)
                out.append(acc + _dot(jnp.exp2(_dot_nt(q, kt)).astype(BF16), vt))
            return jt - 1, tuple(out)

        return lax.while_loop(lambda cr: (cr[0] >= 0) & may_matter(jnp.maximum(cr[0], 0)), body,
                              (n_full - 1, accs))[1]

    def running_max_sweep(accs):
        def alive(jt, ms):
            go = None
            for h in range(2):
                m_min = functools.reduce(jnp.minimum, [jnp.min(ms[h * nsub + r]) for r in range(nsub)])
                live = qk_bound[h] + ncum_ref[h, jt + 1] - m_min >= FOX_DEAD
                go = live if go is None else (go | live)
            return go

        def body(carry):
            jt, _, ms, accs = carry
            new_m, new_acc = [], []
            for (h, r), q, m, acc in zip(chains, qs, ms, accs):
                kt, vt = tiles(jt, h)
                s = _dot_nt(q, kt)
                m_new = jnp.maximum(m, jnp.max(s, axis=-1, keepdims=True))
                new_acc.append(jnp.exp2(m - m_new) * acc + _dot(jnp.exp2(s - m_new).astype(BF16), vt))
                new_m.append(m_new)
            new_m = tuple(new_m)
            return jt - 1, alive(jnp.maximum(jt - 1, 0), new_m), new_m, tuple(new_acc)

        ms = tuple(shifts)
        return lax.while_loop(lambda cr: (cr[0] >= 0) & cr[1], body,
                              (n_full - 1, alive(jnp.maximum(n_full - 1, 0), ms), ms, accs))[3]

    accs = lax.cond(tight, fixed_shift_sweep, running_max_sweep, tuple(accs))
    heads = []
    for h in range(2):
        acc = jnp.concatenate([accs[h * nsub + r] for r in range(nsub)], axis=0)
        heads.append(acc * (1.0 / acc[:, HEAD_DIM:HEAD_DIM + 1]))
    o_ref[...] = jnp.where(feat, heads[0], pltpu.roll(heads[1], HEAD_DIM, 1)).astype(BF16)


def _fox_attention(q, k, v, stats, batch, seq, tq, tk, sub, dsub):
    t = q.shape[0]
    nq = seq // tq
    kv = pl.BlockSpec((seq, 2 * LANES), lambda b, p, i: (b, p))
    assert stats.shape[0] * tk == 8 * t
    tile_stats = stats.reshape(t // tk, 8, LANES)
    tile_stats = jnp.concatenate([tile_stats[:, 0, :FOX_HEADS], tile_stats[:, 1, :FOX_HEADS]], axis=1)
    return pl.pallas_call(
        functools.partial(_fox_kernel, tq=tq, tk=tk, sub=sub, dsub=dsub, seq=seq),
        grid=(batch, FOX_HEADS // 2, nq),
        in_specs=[pl.BlockSpec(memory_space=pltpu.SMEM),
                  pl.BlockSpec((tq, 2 * LANES), lambda b, p, i: (b * nq + i, p)), kv, kv],
        out_specs=pl.BlockSpec((tq, LANES), lambda b, p, i: (b * nq + i, p)),
        out_shape=jax.ShapeDtypeStruct((t, FOX_W), BF16),
        scratch_shapes=[pltpu.SMEM((2,), F32), pltpu.SMEM((2, seq // tk), F32)],
        compiler_params=_params(("parallel", "parallel", "arbitrary")),
        name="fox_attention",
    )(tile_stats, q, k, v)


FFN_CHUNKS = (2816,)
assert sum(FFN_CHUNKS) == D_FF
HALO = 8


def _ffn_kernel(*refs, tm, nsb, final, n_attn):
    attn_refs, wattn_ref, refs = refs[:n_attn], refs[n_attn], refs[n_attn + 1:]
    if final:
        x_ref, g_ref, win_ref, cw_ref, cb_ref, wout_ref, fn_ref, o_ref, a_scr, halo_scr = refs
    else:
        x_ref, g_ref, win_ref, cw_ref, cb_ref, wout_ref, o_ref, a_scr, halo_scr = refs
    i = pl.program_id(0)

    @pl.when(i % nsb == 0)
    def _():
        halo_scr[...] = jnp.zeros((HALO, D_FF), F32)

    x = x_ref[...]
    off = 0
    for a_ref in attn_refs:
        width = a_ref.shape[1]
        x = x + _dot(a_ref[...], wattn_ref[off:off + width, :])
        off += width
    hb = _rmsnorm(x, g_ref[...]).astype(BF16)
    acc = jnp.zeros((tm, D_MODEL), F32)
    start = 0
    for width in FFN_CHUNKS:
        sl = slice(start, start + width)
        a = _dot(hb, win_ref[:, sl])
        b = _dot(hb, win_ref[:, D_FF + start:D_FF + start + width])
        start += width
        a_scr[0:HALO, 0:width] = halo_scr[:, sl]
        a_scr[HALO:HALO + tm, 0:width] = a
        halo_scr[:, sl] = a[tm - HALO:tm, :]
        conv = (cw_ref[0:1, sl] * a_scr[HALO - 2:HALO - 2 + tm, 0:width]
                + cw_ref[1:2, sl] * a_scr[HALO - 1:HALO - 1 + tm, 0:width]
                + cw_ref[2:3, sl] * a + cb_ref[:, sl])
        gated = conv * (1.0 / (1.0 + jnp.exp(-conv))) * b
        acc = acc + _dot(gated.astype(BF16), wout_ref[sl, :])
    y = x + acc
    if final:
        y = _rmsnorm(y, fn_ref[...])
    o_ref[...] = y


def _mixer_out_ffn(xf, attn_outs, w_attn, g, w_in, conv_w, conv_b, w_out, final_norm, seq, tm):
    t, d = xf.shape
    nsb = seq // tm
    row = pl.BlockSpec((tm, d), lambda i: (i, 0))
    const = lambda shape: pl.BlockSpec(shape, lambda i: (0, 0), pipeline_mode=pl.Buffered(1))
    small = lambda shape: pl.BlockSpec(shape, lambda i: (0, 0))
    final = final_norm is not None
    in_specs = [pl.BlockSpec((tm, a.shape[1]), lambda i: (i, 0)) for a in attn_outs] + [const(w_attn.shape)]
    in_specs += [row, small((1, d)), const(w_in.shape), small(conv_w.shape), small((1, D_FF)), const(w_out.shape)]
    args = list(attn_outs) + [w_attn, xf, g, w_in, conv_w, conv_b, w_out]
    if final:
        in_specs.append(small((1, d)))
        args.append(final_norm)
    return pl.pallas_call(
        functools.partial(_ffn_kernel, tm=tm, nsb=nsb, final=final, n_attn=len(attn_outs)),
        grid=(t // tm,),
        in_specs=in_specs,
        out_specs=row,
        out_shape=jax.ShapeDtypeStruct((t, d), F32),
        scratch_shapes=[pltpu.VMEM((tm + HALO, max(FFN_CHUNKS)), F32), pltpu.VMEM((HALO, D_FF), F32)],
        compiler_params=_params(("arbitrary",)),
        name="conv_glu_ffn",
    )(*args)


def kernel(x, attn_norm, ffn_norm, ev_w_in, ev_cmp_pos_k, ev_cmp_pos_v, ev_cmp_w_k, ev_cmp_w_v, ev_w_out,
           od_w_in, od_b_f, od_w_out, ffn_w_in, ffn_conv_w, ffn_conv_b, ffn_w_out, final_norm):
    batch, seq, d = x.shape
    t = batch * seq
    depth = attn_norm.shape[0]
    tm, ffn_tm = min(512, seq), min(512, seq)
    sb_tile, sb_nsub = min(256, seq), 2
    fox_tq, fox_tk, fox_sub, fox_dsub = min(512, seq), min(512, seq), 512, 512
    sel_tk = min(512, seq)

    xf = x.reshape(t, d)
    tabs = _rope_tables(jnp.arange(seq))
    ctabs = _rope_tables(jnp.arange(seq // CMP_STRIDE) * CMP_STRIDE + (CMP_LEN - 1))
    mt, oh = _nsa_consts(seq)

    for layer in range(depth):
        g_attn = attn_norm[layer].reshape(1, d)
        if layer % 2 == 0:
            e = layer // 2
            (sbq, sbk, sbv, nq, kc_raw, vc_raw, ksl, vsl, kwn, vwn, gates) = _even_proj(
                xf, g_attn, _prep_even_w(ev_w_in[e]), tabs, seq, tm)
            pek, wk = _prep_cmp(ev_cmp_pos_k[e], ev_cmp_w_k[e])
            pev, wv = _prep_cmp(ev_cmp_pos_v[e], ev_cmp_w_v[e])
            kc, vc = _compress(kc_raw, vc_raw, pek, pev, wk, wv, ctabs, batch, seq)
            o_sb = _sb_attention(sbq, sbk, sbv, batch, seq, sb_tile, sb_nsub)
            o_nsa = _nsa_attention(nq, kc, vc, ksl, vsl, kwn, vwn, gates, mt, oh, batch, seq, sel_tk)
            attn_outs, w_attn = [o_sb, o_nsa], ev_w_out[e].astype(BF16)
        else:
            o = layer // 2
            w = jnp.pad(od_w_in[o], ((0, 0), (0, LANES - FOX_HEADS))).astype(BF16)
            bf = jnp.pad(od_b_f[o], (0, LANES - FOX_HEADS)).reshape(1, LANES)
            q, k, v, stats = _odd_proj(xf, g_attn, w, bf, seq, tm)
            o_fox = _fox_attention(q, k, v, stats, batch, seq, fox_tq, fox_tk, fox_sub, fox_dsub)
            attn_outs, w_attn = [o_fox], od_w_out[o].astype(BF16)
        last = layer == depth - 1
        xf = _mixer_out_ffn(xf, attn_outs, w_attn, ffn_norm[layer].reshape(1, d), ffn_w_in[layer].astype(BF16),
                            ffn_conv_w[layer], ffn_conv_b[layer].reshape(1, D_FF), ffn_w_out[layer].astype(BF16),
                            final_norm.reshape(1, d) if last else None, seq, ffn_tm)
    return xf.reshape(batch, seq, d)
```

```python
import functools
import math

import jax
import jax.numpy as jnp
from jax import lax
from jax.experimental import pallas as pl
from jax.experimental.pallas import tpu as pltpu

F32, BF16 = jnp.float32, jnp.bfloat16

D_MODEL = 1024
HEAD_DIM = 64
N_HEADS = D_MODEL // HEAD_DIM
SB_HEADS = N_HEADS // 2
NSA_HEADS = N_HEADS - SB_HEADS
NSA_KV_GROUPS = 2
NSA_REP = NSA_HEADS // NSA_KV_GROUPS
FOX_HEADS = N_HEADS
CMP_LEN = 32
CMP_STRIDE = 16
SEL_LEN = 64
SEL_TOP = 16
WINDOW = 512
N_BRANCH = 3
ROPE_THETA = 500000.0
ROT_DIM = HEAD_DIM // 4
D_FF = 2816
CONV_WIDTH = 3
NORM_EPS = 1e-6
SB_W = SB_HEADS * HEAD_DIM
NSA_QW = NSA_HEADS * HEAD_DIM
NSA_KVW = NSA_KV_GROUPS * HEAD_DIM
FOX_W = FOX_HEADS * HEAD_DIM

LANES = 128
SEL_LANES = 128
QK_SCALE = HEAD_DIM ** -0.5
LOG2E = math.log2(math.e)
SEL_NEG = -(2.0 ** 30)
NSA_TIGHT = 86.0
NORM_SLACK = 1.001
VMEM_LIMIT = 56 * 2 ** 20

_NT = (((1,), (1,)), ((), ()))


def _params(sem):
    return pltpu.CompilerParams(dimension_semantics=sem, vmem_limit_bytes=VMEM_LIMIT)


def _dot(a, b):
    return jnp.dot(a, b, preferred_element_type=F32)


def _dot_nt(a, b):
    return lax.dot_general(a, b, _NT, preferred_element_type=F32)


def _rmsnorm(x, g):
    ms = jnp.mean(x * x, axis=-1, keepdims=True)
    return (x * lax.rsqrt(ms + NORM_EPS)) * g


def _rope(y, c, s1, s2):
    return y * c + pltpu.roll(y, LANES - ROT_DIM // 2, 1) * s1 + pltpu.roll(y, ROT_DIM // 2, 1) * s2


def _split3(x):
    hi = x.astype(BF16)
    r1 = x - hi.astype(F32)
    mid = r1.astype(BF16)
    lo = (r1 - mid.astype(F32)).astype(BF16)
    return hi, mid, lo


def _rope_tables(pos):
    half = ROT_DIM // 2
    inv_freq = ROPE_THETA ** (-(jnp.arange(half, dtype=F32) * 2.0 / ROT_DIM))
    ang = pos.astype(F32)[:, None] * inv_freq[None, :]
    cos, sin = jnp.cos(ang), jnp.sin(ang)
    n = pos.shape[0]
    one = jnp.ones((n, HEAD_DIM - ROT_DIM), F32)
    zero = jnp.zeros((n, HEAD_DIM - ROT_DIM), F32)
    z8 = jnp.zeros((n, half), F32)
    c = jnp.concatenate([cos, cos, one], -1)
    s1 = jnp.concatenate([-sin, z8, zero], -1)
    s2 = jnp.concatenate([z8, sin, zero], -1)
    two = lambda t: jnp.concatenate([t, t], -1)
    return two(c), two(s1), two(s2)


def _even_proj_kernel(x_ref, g_ref, w_ref, c_ref, s1_ref, s2_ref,
                      sbq_ref, sbk_ref, sbv_ref, nq_ref, kc_ref, vc_ref,
                      ksl_ref, vsl_ref, kwn_ref, vwn_ref, gate_ref):
    hb = _rmsnorm(x_ref[...], g_ref[...]).astype(BF16)
    c, s1, s2 = c_ref[...], s1_ref[...], s2_ref[...]

    sb = _dot(hb, w_ref[:, 0:3 * SB_W])
    sbq_ref[...] = (sb[:, 0:SB_W] * QK_SCALE).astype(BF16)
    sbk_ref[...] = sb[:, SB_W:2 * SB_W].astype(BF16)
    sbv_ref[...] = sb[:, 2 * SB_W:3 * SB_W].astype(BF16)
    nsa = _dot(hb, w_ref[:, 3 * SB_W:])
    piece = lambda j: nsa[:, j * LANES:(j + 1) * LANES]
    n_q = NSA_QW // LANES
    for j in range(n_q):
        nq_ref[:, j * LANES:(j + 1) * LANES] = (_rope(piece(j), c, s1, s2) * (QK_SCALE * LOG2E)).astype(BF16)
    kc_ref[...] = piece(n_q)
    vc_ref[...] = piece(n_q + 1)
    ksl_ref[...] = _rope(piece(n_q + 2), c, s1, s2).astype(BF16)
    vsl_ref[...] = piece(n_q + 3).astype(BF16)
    kwn_ref[...] = _rope(piece(n_q + 4), c, s1, s2).astype(BF16)
    vwn_ref[...] = piece(n_q + 5).astype(BF16)
    gate_ref[...] = 1.0 / (1.0 + jnp.exp(-nsa[:, (n_q + 6) * LANES:(n_q + 8) * LANES]))


def _even_proj(xf, g, w, tabs, seq, tm):
    t, d = xf.shape
    nsb = seq // tm
    n = w.shape[1]
    row = lambda width: pl.BlockSpec((tm, width), lambda i: (i, 0))
    tab = pl.BlockSpec((tm, LANES), lambda i: (i % nsb, 0))
    out_shape = (
        [jax.ShapeDtypeStruct((t, SB_W), BF16)] * 3
        + [jax.ShapeDtypeStruct((t, NSA_QW), BF16)]
        + [jax.ShapeDtypeStruct((t, LANES), F32)] * 2
        + [jax.ShapeDtypeStruct((t, LANES), BF16)] * 4
        + [jax.ShapeDtypeStruct((t, 2 * LANES), F32)]
    )
    out_specs = [row(SB_W)] * 3 + [row(NSA_QW)] + [row(LANES)] * 6 + [row(2 * LANES)]
    return pl.pallas_call(
        _even_proj_kernel,
        grid=(t // tm,),
        in_specs=[row(d), pl.BlockSpec((1, d), lambda i: (0, 0)),
                  pl.BlockSpec((d, n), lambda i: (0, 0)), tab, tab, tab],
        out_specs=out_specs,
        out_shape=out_shape,
        compiler_params=_params(("parallel",)),
        name="even_proj",
    )(xf, g, w, *tabs)


def _prep_even_w(w):
    d = w.shape[0]
    main = w[:, :3 * SB_W + NSA_QW + 6 * NSA_KVW]
    gates = w[:, 3 * SB_W + NSA_QW + 6 * NSA_KVW:].reshape(d, NSA_KV_GROUPS, NSA_REP, N_BRANCH)
    gates = jnp.transpose(gates, (0, 1, 3, 2)).reshape(d, NSA_KV_GROUPS, N_BRANCH * NSA_REP)
    gates = jnp.pad(gates, ((0, 0), (0, 0), (0, LANES - N_BRANCH * NSA_REP)))
    return jnp.concatenate([main, gates.reshape(d, NSA_KV_GROUPS * LANES)], axis=1).astype(BF16)


SB_DEAD = -105.0


def _sb_kernel(q_ref, k_ref, v_ref, o_ref, *, tile, nsub):
    i = pl.program_id(2)
    lane = lax.broadcasted_iota(jnp.int32, (1, LANES), 1)
    r = lax.broadcasted_iota(jnp.int32, (tile, tile), 0)
    c = lax.broadcasted_iota(jnp.int32, (tile, tile), 1)
    later = jnp.where(r > c, 1.0, 0.0).astype(BF16)
    diag = c < r
    chains = [(h, sub) for h in range(2) for sub in range(nsub)]
    qs = []
    for h, sub in chains:
        q = q_ref[sub * tile:(sub + 1) * tile, :]
        qs.append(jnp.where((lane < HEAD_DIM) if h == 0 else (lane >= HEAD_DIM), q, jnp.zeros_like(q)))

    def step(jt, state, mode):
        k0 = pl.multiple_of(jt * tile, tile)
        kt = k_ref[pl.ds(k0, tile), :]
        vt = v_ref[pl.ds(k0, tile), :]
        out = []
        for c, (acc, keep_sum) in enumerate(state):
            if mode[c] is None:
                out.append((acc, keep_sum))
                continue
            z = _dot_nt(qs[c], kt)
            ls = jnp.minimum(z, 0.0) - jnp.log(1.0 + jnp.exp(-jnp.abs(z)))
            lk = ls - z
            if mode[c]:
                lk = jnp.where(diag, lk, 0.0)
            hi = lk.astype(BF16)
            lo = (lk - hi.astype(F32)).astype(BF16)
            both = _dot(jnp.concatenate([hi, lo], axis=0), later)
            after = both[0:tile] + both[tile:2 * tile]
            a = jnp.exp(ls + after + keep_sum)
            if mode[c]:
                a = jnp.where(diag, a, 0.0)
            out.append((acc + _dot(a.astype(BF16), vt), keep_sum + after[:, 0:1] + lk[:, 0:1]))
        return tuple(out)

    def alive(state):
        return functools.reduce(jnp.maximum, [jnp.max(keep_sum) for _, keep_sum in state]) >= SB_DEAD

    state = tuple((jnp.zeros((tile, LANES), F32), jnp.zeros((tile, 1), F32)) for _ in chains)
    for top in range(nsub - 1, -1, -1):
        mode = [None if sub < top else sub == top for _, sub in chains]
        state = step(nsub * i + top, state, mode)

    def body(carry):
        jt, _, state = carry
        state = step(jt, state, [False] * len(chains))
        return jt - 1, alive(state), state

    _, _, state = lax.while_loop(lambda cr: (cr[0] >= 0) & cr[1], body, (nsub * i - 1, alive(state), state))
    heads = [jnp.concatenate([state[h * nsub + sub][0] for sub in range(nsub)], axis=0) for h in range(2)]
    o_ref[...] = jnp.where(lane < HEAD_DIM, heads[0], heads[1]).astype(BF16)


def _sb_attention(q, k, v, batch, seq, tile, nsub):
    t, w = q.shape
    nq = seq // (tile * nsub)
    kv = pl.BlockSpec((seq, LANES), lambda b, p, i: (b, p))
    qo = pl.BlockSpec((tile * nsub, LANES), lambda b, p, i: (b * nq + i, p))
    return pl.pallas_call(
        functools.partial(_sb_kernel, tile=tile, nsub=nsub),
        grid=(batch, w // LANES, nq),
        in_specs=[qo, kv, kv],
        out_specs=qo,
        out_shape=jax.ShapeDtypeStruct((t, w), BF16),
        compiler_params=_params(("parallel", "parallel", "arbitrary")),
        name="sb_attention",
    )(q, k, v)


def _compress_kernel(xk_ref, xv_ref, pek_ref, pev_ref, wk_ref, wv_ref, c_ref, s1_ref, s2_ref,
                     kc_ref, vc_ref, *, nc):
    def comp(x_ref, pe_ref, w_ref):
        x = x_ref[0]
        top = _dot((x + pe_ref[0:1, :]).astype(BF16), w_ref[0])
        bot = _dot((x + pe_ref[1:2, :]).astype(BF16), w_ref[1])
        return top + pltpu.roll(bot, nc - 1, 0)

    kc = _rope(comp(xk_ref, pek_ref, wk_ref), c_ref[...], s1_ref[...], s2_ref[...])
    kc_ref[0] = kc.astype(BF16)
    vc_ref[0] = comp(xv_ref, pev_ref, wv_ref).astype(BF16)


def _prep_cmp(pe, w):
    half = CMP_LEN // 2
    w3 = w.reshape(CMP_LEN, HEAD_DIM, HEAD_DIM)
    eye = jnp.eye(NSA_KV_GROUPS, dtype=w.dtype)
    parts, pes = [], []
    for s in range(2):
        wh = jnp.einsum('ldo,gh->lgdho', w3[s * half:(s + 1) * half], eye)
        parts.append(wh.reshape(half * NSA_KVW, NSA_KVW))
        pes.append(jnp.broadcast_to(pe[s * half:(s + 1) * half, None, :],
                                    (half, NSA_KV_GROUPS, HEAD_DIM)).reshape(1, half * NSA_KVW))
    return jnp.concatenate(pes, 0), jnp.stack(parts).astype(BF16)


def _compress(kc_raw, vc_raw, pek, pev, wk, wv, ctabs, batch, seq):
    nc = seq // CMP_STRIDE
    cw = CMP_STRIDE * NSA_KVW
    xk = kc_raw.reshape(batch, nc, cw)
    xv = vc_raw.reshape(batch, nc, cw)
    xs = pl.BlockSpec((1, nc, cw), lambda b: (b, 0, 0))
    pes = pl.BlockSpec((2, cw), lambda b: (0, 0))
    ws = pl.BlockSpec((2, cw, NSA_KVW), lambda b: (0, 0, 0))
    tab = pl.BlockSpec((nc, LANES), lambda b: (0, 0))
    out = pl.BlockSpec((1, nc, LANES), lambda b: (b, 0, 0))
    return pl.pallas_call(
        functools.partial(_compress_kernel, nc=nc),
        grid=(batch,),
        in_specs=[xs, xs, pes, pes, ws, ws, tab, tab, tab],
        out_specs=[out, out],
        out_shape=[jax.ShapeDtypeStruct((batch, nc, LANES), BF16)] * 2,
        compiler_params=_params(("parallel",)),
        name="nsa_compress",
    )(xk, xv, pek, pev, wk, wv, *ctabs)


QB = 128


def _softmax_rows(s):
    m = jnp.max(s, axis=-1, keepdims=True)
    m = jnp.where(m > -jnp.inf, m, 0.0)
    e = jnp.exp2(s - m)
    d = jnp.sum(e, axis=-1, keepdims=True)
    return e / jnp.where(d > 0, d, 1.0)


def _nsa_kernel(q_ref, kc_ref, vc_ref, ks_ref, vs_ref, kw_ref, vw_ref, g_ref, mt_ref, oh_ref,
                o_ref, score_ref, knorm_ref, *, nc, tk, wlen, seq):
    g = pl.program_id(1)
    i = pl.program_id(2)
    q0 = i * QB
    rows = NSA_REP * QB

    q2 = q_ref[...]
    pr = lax.broadcasted_iota(jnp.int32, (NSA_REP * HEAD_DIM, LANES), 0)
    pc = lax.broadcasted_iota(jnp.int32, (NSA_REP * HEAD_DIM, LANES), 1) - g * HEAD_DIM
    in_group = (pc >= 0) & (pc < HEAD_DIM)
    q4 = jnp.concatenate(
        [_dot(q2, jnp.where(in_group & (pr - hh * HEAD_DIM == pc), 1.0, 0.0).astype(BF16)).astype(BF16)
         for hh in range(NSA_REP)], axis=0)

    sc = _dot_nt(q4, kc_ref[0])
    cend = lax.broadcasted_iota(jnp.int32, (rows, nc), 1) * CMP_STRIDE + (CMP_LEN - 1)
    tpos_c = q0 + (lax.broadcasted_iota(jnp.int32, (rows, nc), 0) & (QB - 1))
    p_cmp = _softmax_rows(jnp.where(cend <= tpos_c, sc, -jnp.inf))
    o_cmp = _dot(p_cmp.astype(BF16), vc_ref[0])

    p_sum = p_cmp[0:QB] + p_cmp[QB:2 * QB] + p_cmp[2 * QB:3 * QB] + p_cmp[3 * QB:4 * QB]
    overlap = mt_ref[...]
    parts = _dot(jnp.concatenate(_split3(p_sum), axis=0), overlap)
    imp_t = jnp.transpose(parts[0:QB] + parts[QB:2 * QB] + parts[2 * QB:3 * QB])
    nidx = lax.broadcasted_iota(jnp.int32, (SEL_LANES, QB), 0)
    cur = (q0 + lax.broadcasted_iota(jnp.int32, (SEL_LANES, QB), 1)) // SEL_LEN
    causal = nidx <= cur
    forced = (nidx == 0) | (nidx == cur) | (nidx == cur - 1)
    score = jnp.where(causal, jnp.where(forced, jnp.inf, imp_t), -jnp.inf)
    score_ref[...] = score

    def larger_step(m, cnt):
        return cnt + jnp.where(score_ref[pl.ds(m, 1), :] > score, 1.0, 0.0)

    def larger_pair(m2, cnt):
        return larger_step(2 * m2 + 1, larger_step(2 * m2, cnt))

    def rank_step(m, cnt):
        rowm = score_ref[pl.ds(m, 1), :]
        ge = jnp.where(rowm >= score, 1.0, 0.0)
        gt = jnp.where(rowm > score, 1.0, 0.0)
        return cnt + jnp.where(nidx > m, ge, gt)

    n_causal = (q0 + QB) // SEL_LEN
    zero_cnt = jnp.zeros((SEL_LANES, QB), F32)
    cnt = lax.fori_loop(0, n_causal // 2, larger_pair, zero_cnt)
    n_top = jnp.sum(jnp.where(causal & (cnt < float(SEL_TOP)), 1.0, 0.0), axis=0, keepdims=True)
    cnt = lax.cond(jnp.max(n_top) > float(SEL_TOP),
                   lambda c: lax.fori_loop(0, n_causal, rank_step, zero_cnt), lambda c: c, cnt)
    bias_t = jnp.where(causal & (cnt < float(SEL_TOP)), 0.0, SEL_NEG)
    selb4 = jnp.concatenate([jnp.transpose(bias_t)] * NSA_REP, axis=0)

    lane = lax.broadcasted_iota(jnp.int32, (1, LANES), 1)
    own_lanes = (lane >= g * HEAD_DIM) & (lane < (g + 1) * HEAD_DIM)

    @pl.when(i == 0)
    def _():
        feat = lax.broadcasted_iota(jnp.int32, (LANES, 1), 0)
        own_rows = (feat >= g * HEAD_DIM) & (feat < (g + 1) * HEAD_DIM)

        def chunk(n, best):
            kf = ks_ref[0, :, pl.ds(pl.multiple_of(n * tk, tk), tk)].astype(F32)
            kf = jnp.where(own_rows, kf, 0.0)
            return jnp.maximum(best, jnp.max(jnp.sum(kf * kf, axis=0, keepdims=True)))
        knorm_ref[0] = jnp.sqrt(lax.fori_loop(0, seq // tk, chunk, jnp.float32(0.0)))

    q4f = q4.astype(F32)
    shift = jnp.sqrt(jnp.sum(q4f * q4f, axis=-1, keepdims=True)) * (knorm_ref[0] * NORM_SLACK)
    shift = (shift * (1.0 + 2.0 ** -7)).astype(BF16).astype(F32)
    tight = 2.0 * jnp.max(shift) <= NSA_TIGHT
    qs = jnp.concatenate([q4, jnp.where(tight, selb4 - shift, selb4).astype(BF16)], axis=1)
    half = tk // 2
    diag0 = pl.multiple_of((q0 // half) * half, half)
    n_big = diag0 // tk
    odd_half = diag0 - n_big * tk > 0
    half0 = pl.multiple_of(n_big * tk, tk)

    def scores(k0, size, masked):
        kk = jnp.concatenate([ks_ref[0, :, pl.ds(k0, size)], oh_ref[:, pl.ds(k0, size)]], axis=0)
        vt = vs_ref[pl.ds(k0, size), :]
        vt = jnp.where(own_lanes, vt, jnp.ones_like(vt))
        s = _dot(qs, kk)
        if masked:
            kpos = k0 + lax.broadcasted_iota(jnp.int32, (rows, size), 1)
            tpos = q0 + (lax.broadcasted_iota(jnp.int32, (rows, size), 0) & (QB - 1))
            s = jnp.where(kpos <= tpos, s, -jnp.inf)
        return s, vt

    def sweep(step, carry):
        carry = lax.fori_loop(0, n_big, lambda j, cr: step(pl.multiple_of(j * tk, tk), tk, False, cr), carry)
        carry = lax.cond(odd_half, lambda cr: step(half0, half, False, cr), lambda cr: cr, carry)
        return step(diag0, half, True, carry)

    def fixed_shift_sweep(acc):
        def step(k0, size, masked, acc):
            s, vt = scores(k0, size, masked)
            return acc + _dot(jnp.exp2(s).astype(BF16), vt)
        return sweep(step, acc)

    def running_max_sweep(acc):
        def step(k0, size, masked, carry):
            m, acc = carry
            s, vt = scores(k0, size, masked)
            m_new = jnp.maximum(m, jnp.max(s, axis=-1, keepdims=True))
            return m_new, jnp.exp2(m - m_new) * acc + _dot(jnp.exp2(s - m_new).astype(BF16), vt)
        return sweep(step, (jnp.full((rows, 1), -jnp.inf, F32), acc))[1]

    acc_s = lax.cond(tight, fixed_shift_sweep, running_max_sweep, jnp.zeros((rows, LANES), F32))
    l_s = jnp.where(g == 0, acc_s[:, HEAD_DIM:HEAD_DIM + 1], acc_s[:, 0:1])
    o_slc = acc_s / l_s

    w0 = pl.multiple_of(jnp.maximum(q0 - WINDOW, 0), QB)
    sw = _dot_nt(q4, kw_ref[pl.ds(w0, wlen), :])
    kpos = w0 + lax.broadcasted_iota(jnp.int32, (rows, wlen), 1)
    tpos = q0 + (lax.broadcasted_iota(jnp.int32, (rows, wlen), 0) & (QB - 1))
    p_win = _softmax_rows(jnp.where((kpos <= tpos) & (kpos > tpos - WINDOW), sw, -jnp.inf))
    o_win = _dot(p_win.astype(BF16), vw_ref[pl.ds(w0, wlen), :])

    gates = g_ref[...]
    orow = lax.broadcasted_iota(jnp.int32, (LANES, NSA_REP * HEAD_DIM), 0) - g * HEAD_DIM
    ocol = lax.broadcasted_iota(jnp.int32, (LANES, NSA_REP * HEAD_DIM), 1)
    o_group = (orow >= 0) & (orow < HEAD_DIM)
    out = jnp.zeros((QB, NSA_REP * HEAD_DIM), F32)
    for hh in range(NSA_REP):
        sl = slice(hh * QB, (hh + 1) * QB)
        gate = lambda br: gates[:, br * NSA_REP + hh:br * NSA_REP + hh + 1]
        o_h = gate(0) * o_cmp[sl] + gate(1) * o_slc[sl] + gate(2) * o_win[sl]
        place = jnp.where(o_group & (ocol - hh * HEAD_DIM == orow), 1.0, 0.0).astype(BF16)
        out = out + _dot(o_h.astype(BF16), place)
    o_ref[...] = out.astype(BF16)


def _nsa_consts(seq):
    nc = seq // CMP_STRIDE
    c_start = jnp.arange(nc) * CMP_STRIDE
    sel_start = jnp.arange(SEL_LANES) * SEL_LEN
    real = (jnp.arange(SEL_LANES) < seq // SEL_LEN)[:, None] & (jnp.arange(nc) < (seq - CMP_LEN) // CMP_STRIDE + 1)[None, :]
    overlap = (c_start[None, :] < sel_start[:, None] + SEL_LEN) & (c_start[None, :] + CMP_LEN > sel_start[:, None])
    mt = jnp.transpose(overlap & real).astype(BF16)
    oh = (jnp.arange(seq)[None, :] // SEL_LEN == jnp.arange(SEL_LANES)[:, None]).astype(BF16)
    return mt, oh


def _nsa_attention(nq, kc, vc, ksl, vsl, kwn, vwn, gates, mt, oh, batch, seq, tk):
    t = nq.shape[0]
    nqb = seq // QB
    nc = seq // CMP_STRIDE
    wlen = WINDOW + QB
    gw = NSA_REP * HEAD_DIM
    qspec = pl.BlockSpec((QB, gw), lambda b, g, i: (b * nqb + i, g))
    cspec = pl.BlockSpec((1, nc, LANES), lambda b, g, i: (b, 0, 0))
    kvspec = pl.BlockSpec((seq, LANES), lambda b, g, i: (b, 0))
    gspec = pl.BlockSpec((QB, LANES), lambda b, g, i: (b * nqb + i, g))
    return pl.pallas_call(
        functools.partial(_nsa_kernel, nc=nc, tk=tk, wlen=wlen, seq=seq),
        grid=(batch, NSA_KV_GROUPS, nqb),
        in_specs=[qspec, cspec, cspec, pl.BlockSpec((1, LANES, seq), lambda b, g, i: (b, 0, 0)),
                  kvspec, kvspec, kvspec, gspec,
                  pl.BlockSpec((nc, SEL_LANES), lambda b, g, i: (0, 0)),
                  pl.BlockSpec((SEL_LANES, seq), lambda b, g, i: (0, 0))],
        out_specs=qspec,
        out_shape=jax.ShapeDtypeStruct((t, NSA_QW), BF16),
        scratch_shapes=[pltpu.VMEM((SEL_LANES, QB), F32), pltpu.SMEM((1,), F32)],
        compiler_params=_params(("parallel", "parallel", "arbitrary")),
        name="nsa_attention",
    )(nq, kc, vc, jnp.swapaxes(ksl.reshape(batch, seq, LANES), 1, 2), vsl, kwn, vwn, gates, mt, oh)


N_CUM_PARTS = 3


def _odd_proj_kernel(x_ref, g_ref, w_ref, bf_ref, place_ref, q_ref, k_ref, v_ref, stats_ref, carry_ref,
                     *, tm, nsb):
    i = pl.program_id(0)
    hb = _rmsnorm(x_ref[...], g_ref[...]).astype(BF16)
    f = _dot(hb, w_ref[:, 3 * FOX_W:3 * FOX_W + LANES]) + bf_ref[...]
    log_f = jnp.minimum(f, 0.0) - jnp.log1p(jnp.exp(-jnp.abs(f)))
    r = lax.broadcasted_iota(jnp.int32, (tm, tm), 0)
    c = lax.broadcasted_iota(jnp.int32, (tm, tm), 1)
    tri = jnp.where(c <= r, 1.0, 0.0).astype(BF16)
    local = _dot(tri, jnp.concatenate(_split3(log_f), axis=1))
    local = local[:, 0:LANES] + local[:, LANES:2 * LANES] + local[:, 2 * LANES:3 * LANES]

    @pl.when(i % nsb == 0)
    def _():
        carry_ref[...] = jnp.zeros_like(carry_ref)

    cum = local + carry_ref[0:1, :]
    carry_ref[0:1, :] = cum[tm - 1:tm, :]
    lane = lax.broadcasted_iota(jnp.int32, (1, LANES), 1)
    neg_cum = cum * (-LOG2E)
    hi, mid, lo = (part.astype(F32) for part in _split3(neg_cum))
    k_sq = jnp.zeros((1, LANES), F32)
    head_of_lane = jnp.where(lax.broadcasted_iota(jnp.int32, (LANES, LANES), 0) // HEAD_DIM
                             == lax.broadcasted_iota(jnp.int32, (LANES, LANES), 1), 1.0, 0.0).astype(BF16)
    packed = jnp.where(lane < FOX_HEADS, hi,
                       jnp.where(lane < 2 * FOX_HEADS, pltpu.roll(mid, FOX_HEADS, 1),
                                 pltpu.roll(lo, 2 * FOX_HEADS, 1))).astype(BF16)
    low = lane < HEAD_DIM
    q_one = jnp.where((lane >= HEAD_DIM) & (lane < HEAD_DIM + N_CUM_PARTS), 1.0, 0.0)
    v_one = jnp.where(lane >= HEAD_DIM, 1.0, 0.0)
    k_one = jnp.where((lane >= HEAD_DIM + N_CUM_PARTS) & (lane < HEAD_DIM + N_CUM_PARTS + N_SHIFT_PARTS), 1.0, 0.0)
    yq_all = _dot(hb, w_ref[:, 0:FOX_W]) * (QK_SCALE * LOG2E)
    yk_all = _dot(hb, w_ref[:, FOX_W:2 * FOX_W])
    yv_all = _dot(hb, w_ref[:, 2 * FOX_W:3 * FOX_W])
    extras_all = _dot(packed, place_ref[...])
    for p in range(FOX_HEADS // 2):
        pair = slice(p * LANES, (p + 1) * LANES)
        yq, yk, yv = yq_all[:, pair], yk_all[:, pair], yv_all[:, pair]
        pair_sq = jnp.max(_dot((yk * yk * (1.0 + 2.0 ** -7)).astype(BF16), head_of_lane), axis=0, keepdims=True)
        for h in range(2):
            hs = slice((2 * p + h) * LANES, (2 * p + h + 1) * LANES)
            head = (lambda y: y) if h == 0 else (lambda y: pltpu.roll(y, HEAD_DIM, 1))
            kh = jnp.where(low, head(yk), 0.0)
            k_sq = jnp.where(lane == 2 * p + h, pair_sq[:, h:h + 1], k_sq)
            q_ref[:, hs] = (jnp.where(low, head(yq), 0.0) + q_one).astype(BF16)
            k_ref[:, hs] = (kh + extras_all[:, hs] + k_one).astype(BF16)
            v_ref[:, hs] = (jnp.where(low, head(yv), 0.0) + v_one).astype(BF16)
    stats_ref[...] = jnp.zeros(stats_ref.shape, F32)
    stats_ref[0:1, :] = k_sq
    stats_ref[1:2, :] = neg_cum[0:1, :]


def _cum_placement():
    src = jnp.arange(LANES)[:, None]
    dst = jnp.arange(FOX_HEADS * LANES)[None, :]
    n, h = src // FOX_HEADS, src % FOX_HEADS
    return ((n < N_CUM_PARTS) & (dst == h * LANES + HEAD_DIM + n)).astype(BF16)


def _odd_proj(xf, g, w, bf, seq, tm):
    t, d = xf.shape
    nsb = seq // tm
    n = w.shape[1]
    wide = FOX_HEADS * LANES
    row = lambda width: pl.BlockSpec((tm, width), lambda i: (i, 0))
    return pl.pallas_call(
        functools.partial(_odd_proj_kernel, tm=tm, nsb=nsb),
        grid=(t // tm,),
        in_specs=[row(d), pl.BlockSpec((1, d), lambda i: (0, 0)),
                  pl.BlockSpec((d, n), lambda i: (0, 0)), pl.BlockSpec((1, LANES), lambda i: (0, 0)),
                  pl.BlockSpec((LANES, wide), lambda i: (0, 0))],
        out_specs=[row(wide)] * 3 + [pl.BlockSpec((8, LANES), lambda i: (i, 0))],
        out_shape=[jax.ShapeDtypeStruct((t, wide), BF16)] * 3 + [jax.ShapeDtypeStruct((t // tm * 8, LANES), F32)],
        scratch_shapes=[pltpu.VMEM((8, LANES), F32)],
        compiler_params=_params(("arbitrary",)),
        name="odd_proj",
    )(xf, g, w, bf, _cum_placement())


FOX_DEAD = -160.0
FOX_TIGHT = 64.0
N_SHIFT_PARTS = 2


def _fox_kernel(stats_ref, q_ref, k_ref, v_ref, o_ref, knorm_ref, ncum_ref, *, tq, tk, sub, dsub, seq, stats_rows):
    b, pair, i = pl.program_id(0), pl.program_id(1), pl.program_id(2)
    q0 = pl.multiple_of(i * tq, tq)
    n_full = q0 // tk
    nsub = tq // sub
    chains = [(h, r) for h in range(2) for r in range(nsub)]
    qs = [q_ref[r * sub:(r + 1) * sub, h * LANES:(h + 1) * LANES] for h, r in chains]
    lane = lax.broadcasted_iota(jnp.int32, (1, LANES), 1)
    feat = lane < HEAD_DIM
    n_stats, per = seq // stats_rows, tk // stats_rows

    def row_norms(x):
        xf = jnp.where(feat, x.astype(F32), 0.0)
        return jnp.sqrt(jnp.sum(xf * xf, axis=-1, keepdims=True))

    @pl.when(i == 0)
    def _():
        for h in range(2):
            def tile_stats(n, best, h=h):
                best = jnp.maximum(best, stats_ref[b * n_stats + n, 2 * pair + h])

                @pl.when(n % per == 0)
                def _():
                    ncum_ref[h, n // per] = stats_ref[b * n_stats + n, FOX_HEADS + 2 * pair + h]
                return best
            knorm_ref[h] = jnp.sqrt(lax.fori_loop(0, n_stats, tile_stats, jnp.float32(0.0)))

    k_norm = [knorm_ref[h] * (NORM_SLACK * (1.0 + 2.0 ** -8)) for h in range(2)]
    q_norm = [row_norms(q) for q in qs]
    qk_bound = [functools.reduce(jnp.maximum, [jnp.max(q_norm[h * nsub + r]) for r in range(nsub)]) * k_norm[h]
                for h in range(2)]
    tight = 2.0 * jnp.maximum(qk_bound[0], qk_bound[1]) <= FOX_TIGHT
    cum_q0 = [ncum_ref[h, n_full] for h in range(2)]

    def tiles(jt, h):
        k0 = pl.multiple_of(jt * tk, tk)
        hs = slice(h * LANES, (h + 1) * LANES)
        return k_ref[pl.ds(k0, tk), hs], v_ref[pl.ds(k0, tk), hs]

    shifts, accs, q_shifted = [], [], []
    for c, ((h, r), q) in enumerate(zip(chains, qs)):
        pieces = []
        for d in range(sub // dsub):
            row0 = r * sub + d * dsub
            nk = row0 + dsub
            kt = k_ref[pl.ds(q0, nk), h * LANES:(h + 1) * LANES]
            vt = v_ref[pl.ds(q0, nk), h * LANES:(h + 1) * LANES]
            qd = q[d * dsub:(d + 1) * dsub]
            s = _dot_nt(qd, kt)
            kpos = lax.broadcasted_iota(jnp.int32, (dsub, nk), 1)
            tpos = row0 + lax.broadcasted_iota(jnp.int32, (dsub, nk), 0)
            s = jnp.where(kpos <= tpos, s, -jnp.inf)
            m_diag = jnp.max(s, axis=-1, keepdims=True)
            bound = q_norm[c][d * dsub:(d + 1) * dsub] * k_norm[h] + cum_q0[h]
            shift = jnp.where(tight, jnp.maximum(m_diag, bound), m_diag)
            hi = shift.astype(BF16)
            rest = shift - hi.astype(F32)
            lo = (rest + jnp.abs(rest) * (2.0 ** -7)).astype(BF16)
            shift = hi.astype(F32) + lo.astype(F32)
            acc = _dot(jnp.exp2(s - shift).astype(BF16), vt)
            qsh = jnp.where(lane == HEAD_DIM + N_CUM_PARTS, -hi,
                            jnp.where(lane == HEAD_DIM + N_CUM_PARTS + 1, -lo, qd))
            pieces.append((shift, acc, qsh))
        shifts.append(jnp.concatenate([p[0] for p in pieces], axis=0))
        accs.append(jnp.concatenate([p[1] for p in pieces], axis=0))
        q_shifted.append(jnp.concatenate([p[2] for p in pieces], axis=0))

    def may_matter(jt):
        return jnp.maximum(ncum_ref[0, jt + 1] - cum_q0[0], ncum_ref[1, jt + 1] - cum_q0[1]) >= FOX_DEAD

    def fixed_shift_sweep(accs):
        def body(carry):
            jt, accs = carry
            out = []
            for (h, r), q, acc in zip(chains, q_shifted, accs):
                kt, vt = tiles(jt, h)
                out.append(acc + _dot(jnp.exp2(_dot_nt(q, kt)).astype(BF16), vt))
            return jt - 1, tuple(out)

        return lax.while_loop(lambda cr: (cr[0] >= 0) & may_matter(jnp.maximum(cr[0], 0)), body,
                              (n_full - 1, accs))[1]

    def running_max_sweep(accs):
        def alive(jt, ms):
            go = None
            for h in range(2):
                m_min = functools.reduce(jnp.minimum, [jnp.min(ms[h * nsub + r]) for r in range(nsub)])
                live = qk_bound[h] + ncum_ref[h, jt + 1] - m_min >= FOX_DEAD
                go = live if go is None else (go | live)
            return go

        def body(carry):
            jt, _, ms, accs = carry
            new_m, new_acc = [], []
            for (h, r), q, m, acc in zip(chains, qs, ms, accs):
                kt, vt = tiles(jt, h)
                s = _dot_nt(q, kt)
                m_new = jnp.maximum(m, jnp.max(s, axis=-1, keepdims=True))
                new_acc.append(jnp.exp2(m - m_new) * acc + _dot(jnp.exp2(s - m_new).astype(BF16), vt))
                new_m.append(m_new)
            new_m = tuple(new_m)
            return jt - 1, alive(jnp.maximum(jt - 1, 0), new_m), new_m, tuple(new_acc)

        ms = tuple(shifts)
        return lax.while_loop(lambda cr: (cr[0] >= 0) & cr[1], body,
                              (n_full - 1, alive(jnp.maximum(n_full - 1, 0), ms), ms, accs))[3]

    accs = lax.cond(tight, fixed_shift_sweep, running_max_sweep, tuple(accs))
    heads = []
    for h in range(2):
        acc = jnp.concatenate([accs[h * nsub + r] for r in range(nsub)], axis=0)
        heads.append(acc * (1.0 / acc[:, HEAD_DIM:HEAD_DIM + 1]))
    o_ref[...] = jnp.where(feat, heads[0], pltpu.roll(heads[1], HEAD_DIM, 1)).astype(BF16)


def _fox_attention(q, k, v, stats, batch, seq, tq, tk, sub, dsub):
    t = q.shape[0]
    nq = seq // tq
    kv = pl.BlockSpec((seq, 2 * LANES), lambda b, p, i: (b, p))
    stats_rows = 8 * t // stats.shape[0]
    assert tk % stats_rows == 0
    tile_stats = stats.reshape(t // stats_rows, 8, LANES)
    tile_stats = jnp.concatenate([tile_stats[:, 0, :FOX_HEADS], tile_stats[:, 1, :FOX_HEADS]], axis=1)
    return pl.pallas_call(
        functools.partial(_fox_kernel, tq=tq, tk=tk, sub=sub, dsub=dsub, seq=seq, stats_rows=stats_rows),
        grid=(batch, FOX_HEADS // 2, nq),
        in_specs=[pl.BlockSpec(memory_space=pltpu.SMEM),
                  pl.BlockSpec((tq, 2 * LANES), lambda b, p, i: (b * nq + i, p)), kv, kv],
        out_specs=pl.BlockSpec((tq, LANES), lambda b, p, i: (b * nq + i, p)),
        out_shape=jax.ShapeDtypeStruct((t, FOX_W), BF16),
        scratch_shapes=[pltpu.SMEM((2,), F32), pltpu.SMEM((2, seq // tk), F32)],
        compiler_params=_params(("parallel", "parallel", "arbitrary")),
        name="fox_attention",
    )(tile_stats, q, k, v)


FFN_CHUNKS = (2816,)
assert sum(FFN_CHUNKS) == D_FF
HALO = 8


def _ffn_kernel(*refs, tm, nsb, final, n_attn):
    attn_refs, wattn_ref, refs = refs[:n_attn], refs[n_attn], refs[n_attn + 1:]
    if final:
        x_ref, g_ref, win_ref, cw_ref, cb_ref, wout_ref, fn_ref, o_ref, a_scr, halo_scr = refs
    else:
        x_ref, g_ref, win_ref, cw_ref, cb_ref, wout_ref, o_ref, a_scr, halo_scr = refs
    i = pl.program_id(0)

    @pl.when(i % nsb == 0)
    def _():
        halo_scr[...] = jnp.zeros((HALO, D_FF), F32)

    x = x_ref[...]
    off = 0
    for a_ref in attn_refs:
        width = a_ref.shape[1]
        x = x + _dot(a_ref[...], wattn_ref[off:off + width, :])
        off += width
    hb = _rmsnorm(x, g_ref[...]).astype(BF16)
    acc = jnp.zeros((tm, D_MODEL), F32)
    start = 0
    for width in FFN_CHUNKS:
        sl = slice(start, start + width)
        a = _dot(hb, win_ref[:, sl])
        b = _dot(hb, win_ref[:, D_FF + start:D_FF + start + width])
        start += width
        a_scr[0:HALO, 0:width] = halo_scr[:, sl]
        a_scr[HALO:HALO + tm, 0:width] = a
        halo_scr[:, sl] = a[tm - HALO:tm, :]
        conv = (cw_ref[0:1, sl] * a_scr[HALO - 2:HALO - 2 + tm, 0:width]
                + cw_ref[1:2, sl] * a_scr[HALO - 1:HALO - 1 + tm, 0:width]
                + cw_ref[2:3, sl] * a + cb_ref[:, sl])
        gated = conv * (1.0 / (1.0 + jnp.exp(-conv))) * b
        acc = acc + _dot(gated.astype(BF16), wout_ref[sl, :])
    y = x + acc
    if final:
        y = _rmsnorm(y, fn_ref[...])
    o_ref[...] = y


def _mixer_out_ffn(xf, attn_outs, w_attn, g, w_in, conv_w, conv_b, w_out, final_norm, seq, tm):
    t, d = xf.shape
    nsb = seq // tm
    row = pl.BlockSpec((tm, d), lambda i: (i, 0))
    const = lambda shape: pl.BlockSpec(shape, lambda i: (0, 0), pipeline_mode=pl.Buffered(1))
    small = lambda shape: pl.BlockSpec(shape, lambda i: (0, 0))
    final = final_norm is not None
    in_specs = [pl.BlockSpec((tm, a.shape[1]), lambda i: (i, 0)) for a in attn_outs] + [const(w_attn.shape)]
    in_specs += [row, small((1, d)), const(w_in.shape), small(conv_w.shape), small((1, D_FF)), const(w_out.shape)]
    args = list(attn_outs) + [w_attn, xf, g, w_in, conv_w, conv_b, w_out]
    if final:
        in_specs.append(small((1, d)))
        args.append(final_norm)
    return pl.pallas_call(
        functools.partial(_ffn_kernel, tm=tm, nsb=nsb, final=final, n_attn=len(attn_outs)),
        grid=(t // tm,),
        in_specs=in_specs,
        out_specs=row,
        out_shape=jax.ShapeDtypeStruct((t, d), F32),
        scratch_shapes=[pltpu.VMEM((tm + HALO, max(FFN_CHUNKS)), F32), pltpu.VMEM((HALO, D_FF), F32)],
        compiler_params=_params(("arbitrary",)),
        name="conv_glu_ffn",
    )(*args)


def kernel(x, attn_norm, ffn_norm, ev_w_in, ev_cmp_pos_k, ev_cmp_pos_v, ev_cmp_w_k, ev_cmp_w_v, ev_w_out,
           od_w_in, od_b_f, od_w_out, ffn_w_in, ffn_conv_w, ffn_conv_b, ffn_w_out, final_norm):
    batch, seq, d = x.shape
    t = batch * seq
    depth = attn_norm.shape[0]
    tm, ffn_tm = min(512, seq), min(512, seq)
    sb_tile, sb_nsub = min(256, seq), 2
    fox_tq, fox_tk, fox_sub, fox_dsub = min(512, seq), min(512, seq), 512, 512
    sel_tk = min(1024, seq)

    xf = x.reshape(t, d)
    tabs = _rope_tables(jnp.arange(seq))
    ctabs = _rope_tables(jnp.arange(seq // CMP_STRIDE) * CMP_STRIDE + (CMP_LEN - 1))
    mt, oh = _nsa_consts(seq)

    for layer in range(depth):
        g_attn = attn_norm[layer].reshape(1, d)
        if layer % 2 == 0:
            e = layer // 2
            (sbq, sbk, sbv, nq, kc_raw, vc_raw, ksl, vsl, kwn, vwn, gates) = _even_proj(
                xf, g_attn, _prep_even_w(ev_w_in[e]), tabs, seq, tm)
            pek, wk = _prep_cmp(ev_cmp_pos_k[e], ev_cmp_w_k[e])
            pev, wv = _prep_cmp(ev_cmp_pos_v[e], ev_cmp_w_v[e])
            kc, vc = _compress(kc_raw, vc_raw, pek, pev, wk, wv, ctabs, batch, seq)
            o_sb = _sb_attention(sbq, sbk, sbv, batch, seq, sb_tile, sb_nsub)
            o_nsa = _nsa_attention(nq, kc, vc, ksl, vsl, kwn, vwn, gates, mt, oh, batch, seq, sel_tk)
            attn_outs, w_attn = [o_sb, o_nsa], ev_w_out[e].astype(BF16)
        else:
            o = layer // 2
            w = jnp.pad(od_w_in[o], ((0, 0), (0, LANES - FOX_HEADS))).astype(BF16)
            bf = jnp.pad(od_b_f[o], (0, LANES - FOX_HEADS)).reshape(1, LANES)
            q, k, v, stats = _odd_proj(xf, g_attn, w, bf, seq, tm)
            o_fox = _fox_attention(q, k, v, stats, batch, seq, fox_tq, fox_tk, fox_sub, fox_dsub)
            attn_outs, w_attn = [o_fox], od_w_out[o].astype(BF16)
        last = layer == depth - 1
        xf = _mixer_out_ffn(xf, attn_outs, w_attn, ffn_norm[layer].reshape(1, d), ffn_w_in[layer].astype(BF16),
                            ffn_conv_w[layer], ffn_conv_b[layer].reshape(1, D_FF), ffn_w_out[layer].astype(BF16),
                            final_norm.reshape(1, d) if last else None, seq, ffn_tm)
    return xf.reshape(batch, seq, d)
```

```python
import functools
import math

import jax
import jax.numpy as jnp
from jax import lax
from jax.experimental import pallas as pl
from jax.experimental.pallas import tpu as pltpu

F32, BF16 = jnp.float32, jnp.bfloat16

D_MODEL = 1024
HEAD_DIM = 64
N_HEADS = D_MODEL // HEAD_DIM
SB_HEADS = N_HEADS // 2
NSA_HEADS = N_HEADS - SB_HEADS
NSA_KV_GROUPS = 2
NSA_REP = NSA_HEADS // NSA_KV_GROUPS
FOX_HEADS = N_HEADS
CMP_LEN = 32
CMP_STRIDE = 16
SEL_LEN = 64
SEL_TOP = 16
WINDOW = 512
N_BRANCH = 3
ROPE_THETA = 500000.0
ROT_DIM = HEAD_DIM // 4
D_FF = 2816
CONV_WIDTH = 3
NORM_EPS = 1e-6
SB_W = SB_HEADS * HEAD_DIM
NSA_QW = NSA_HEADS * HEAD_DIM
NSA_KVW = NSA_KV_GROUPS * HEAD_DIM
FOX_W = FOX_HEADS * HEAD_DIM

LANES = 128
SEL_LANES = 128
QK_SCALE = HEAD_DIM ** -0.5
LOG2E = math.log2(math.e)
SEL_NEG = -(2.0 ** 30)
NSA_TIGHT = 86.0
NORM_SLACK = 1.001
VMEM_LIMIT = 56 * 2 ** 20

_NT = (((1,), (1,)), ((), ()))


def _params(sem):
    return pltpu.CompilerParams(dimension_semantics=sem, vmem_limit_bytes=VMEM_LIMIT)


def _dot(a, b):
    return jnp.dot(a, b, preferred_element_type=F32)


def _dot_nt(a, b):
    return lax.dot_general(a, b, _NT, preferred_element_type=F32)


def _rmsnorm(x, g):
    ms = jnp.mean(x * x, axis=-1, keepdims=True)
    return (x * lax.rsqrt(ms + NORM_EPS)) * g


def _rope(y, c, s1, s2):
    return y * c + pltpu.roll(y, LANES - ROT_DIM // 2, 1) * s1 + pltpu.roll(y, ROT_DIM // 2, 1) * s2


def _split3(x):
    hi = x.astype(BF16)
    r1 = x - hi.astype(F32)
    mid = r1.astype(BF16)
    lo = (r1 - mid.astype(F32)).astype(BF16)
    return hi, mid, lo


def _rope_tables(pos):
    half = ROT_DIM // 2
    inv_freq = ROPE_THETA ** (-(jnp.arange(half, dtype=F32) * 2.0 / ROT_DIM))
    ang = pos.astype(F32)[:, None] * inv_freq[None, :]
    cos, sin = jnp.cos(ang), jnp.sin(ang)
    n = pos.shape[0]
    one = jnp.ones((n, HEAD_DIM - ROT_DIM), F32)
    zero = jnp.zeros((n, HEAD_DIM - ROT_DIM), F32)
    z8 = jnp.zeros((n, half), F32)
    c = jnp.concatenate([cos, cos, one], -1)
    s1 = jnp.concatenate([-sin, z8, zero], -1)
    s2 = jnp.concatenate([z8, sin, zero], -1)
    two = lambda t: jnp.concatenate([t, t], -1)
    return two(c), two(s1), two(s2)


def _even_proj_kernel(x_ref, g_ref, w_ref, c_ref, s1_ref, s2_ref,
                      sbq_ref, sbk_ref, sbv_ref, nq_ref, kc_ref, vc_ref,
                      ksl_ref, vsl_ref, kwn_ref, vwn_ref, gate_ref):
    hb = _rmsnorm(x_ref[...], g_ref[...]).astype(BF16)
    c, s1, s2 = c_ref[...], s1_ref[...], s2_ref[...]

    sb = _dot(hb, w_ref[:, 0:3 * SB_W])
    sbq_ref[...] = (sb[:, 0:SB_W] * QK_SCALE).astype(BF16)
    sbk_ref[...] = sb[:, SB_W:2 * SB_W].astype(BF16)
    sbv_ref[...] = sb[:, 2 * SB_W:3 * SB_W].astype(BF16)
    nsa = _dot(hb, w_ref[:, 3 * SB_W:])
    piece = lambda j: nsa[:, j * LANES:(j + 1) * LANES]
    n_q = NSA_QW // LANES
    for j in range(n_q):
        nq_ref[:, j * LANES:(j + 1) * LANES] = (_rope(piece(j), c, s1, s2) * (QK_SCALE * LOG2E)).astype(BF16)
    kc_ref[...] = piece(n_q)
    vc_ref[...] = piece(n_q + 1)
    ksl_ref[...] = _rope(piece(n_q + 2), c, s1, s2).astype(BF16)
    vsl_ref[...] = piece(n_q + 3).astype(BF16)
    kwn_ref[...] = _rope(piece(n_q + 4), c, s1, s2).astype(BF16)
    vwn_ref[...] = piece(n_q + 5).astype(BF16)
    gate_ref[...] = 1.0 / (1.0 + jnp.exp(-nsa[:, (n_q + 6) * LANES:(n_q + 8) * LANES]))


def _even_proj(xf, g, w, tabs, seq, tm):
    t, d = xf.shape
    nsb = seq // tm
    n = w.shape[1]
    row = lambda width: pl.BlockSpec((tm, width), lambda i: (i, 0))
    tab = pl.BlockSpec((tm, LANES), lambda i: (i % nsb, 0))
    out_shape = (
        [jax.ShapeDtypeStruct((t, SB_W), BF16)] * 3
        + [jax.ShapeDtypeStruct((t, NSA_QW), BF16)]
        + [jax.ShapeDtypeStruct((t, LANES), F32)] * 2
        + [jax.ShapeDtypeStruct((t, LANES), BF16)] * 4
        + [jax.ShapeDtypeStruct((t, 2 * LANES), F32)]
    )
    out_specs = [row(SB_W)] * 3 + [row(NSA_QW)] + [row(LANES)] * 6 + [row(2 * LANES)]
    return pl.pallas_call(
        _even_proj_kernel,
        grid=(t // tm,),
        in_specs=[row(d), pl.BlockSpec((1, d), lambda i: (0, 0)),
                  pl.BlockSpec((d, n), lambda i: (0, 0)), tab, tab, tab],
        out_specs=out_specs,
        out_shape=out_shape,
        compiler_params=_params(("parallel",)),
        name="even_proj",
    )(xf, g, w, *tabs)


def _prep_even_w(w):
    d = w.shape[0]
    main = w[:, :3 * SB_W + NSA_QW + 6 * NSA_KVW]
    gates = w[:, 3 * SB_W + NSA_QW + 6 * NSA_KVW:].reshape(d, NSA_KV_GROUPS, NSA_REP, N_BRANCH)
    gates = jnp.transpose(gates, (0, 1, 3, 2)).reshape(d, NSA_KV_GROUPS, N_BRANCH * NSA_REP)
    gates = jnp.pad(gates, ((0, 0), (0, 0), (0, LANES - N_BRANCH * NSA_REP)))
    return jnp.concatenate([main, gates.reshape(d, NSA_KV_GROUPS * LANES)], axis=1).astype(BF16)


SB_DEAD = -105.0


def _sb_kernel(q_ref, k_ref, v_ref, o_ref, *, tile, nsub):
    i = pl.program_id(2)
    lane = lax.broadcasted_iota(jnp.int32, (1, LANES), 1)
    r = lax.broadcasted_iota(jnp.int32, (tile, tile), 0)
    c = lax.broadcasted_iota(jnp.int32, (tile, tile), 1)
    later = jnp.where(r > c, 1.0, 0.0).astype(BF16)
    diag = c < r
    chains = [(h, sub) for h in range(2) for sub in range(nsub)]
    qs = []
    for h, sub in chains:
        q = q_ref[sub * tile:(sub + 1) * tile, :]
        qs.append(jnp.where((lane < HEAD_DIM) if h == 0 else (lane >= HEAD_DIM), q, jnp.zeros_like(q)))

    def step(jt, state, mode):
        k0 = pl.multiple_of(jt * tile, tile)
        kt = k_ref[pl.ds(k0, tile), :]
        vt = v_ref[pl.ds(k0, tile), :]
        out = []
        for c, (acc, keep_sum) in enumerate(state):
            if mode[c] is None:
                out.append((acc, keep_sum))
                continue
            z = _dot_nt(qs[c], kt)
            ls = jnp.minimum(z, 0.0) - jnp.log(1.0 + jnp.exp(-jnp.abs(z)))
            lk = ls - z
            if mode[c]:
                lk = jnp.where(diag, lk, 0.0)
            hi = lk.astype(BF16)
            lo = (lk - hi.astype(F32)).astype(BF16)
            both = _dot(jnp.concatenate([hi, lo], axis=0), later)
            after = both[0:tile] + both[tile:2 * tile]
            a = jnp.exp(ls + after + keep_sum)
            if mode[c]:
                a = jnp.where(diag, a, 0.0)
            out.append((acc + _dot(a.astype(BF16), vt), keep_sum + after[:, 0:1] + lk[:, 0:1]))
        return tuple(out)

    def alive(state):
        return functools.reduce(jnp.maximum, [jnp.max(keep_sum) for _, keep_sum in state]) >= SB_DEAD

    state = tuple((jnp.zeros((tile, LANES), F32), jnp.zeros((tile, 1), F32)) for _ in chains)
    for top in range(nsub - 1, -1, -1):
        mode = [None if sub < top else sub == top for _, sub in chains]
        state = step(nsub * i + top, state, mode)

    def body(carry):
        jt, _, state = carry
        state = step(jt, state, [False] * len(chains))
        return jt - 1, alive(state), state

    _, _, state = lax.while_loop(lambda cr: (cr[0] >= 0) & cr[1], body, (nsub * i - 1, alive(state), state))
    heads = [jnp.concatenate([state[h * nsub + sub][0] for sub in range(nsub)], axis=0) for h in range(2)]
    o_ref[...] = jnp.where(lane < HEAD_DIM, heads[0], heads[1]).astype(BF16)


def _sb_attention(q, k, v, batch, seq, tile, nsub):
    t, w = q.shape
    nq = seq // (tile * nsub)
    kv = pl.BlockSpec((seq, LANES), lambda b, p, i: (b, p))
    qo = pl.BlockSpec((tile * nsub, LANES), lambda b, p, i: (b * nq + i, p))
    return pl.pallas_call(
        functools.partial(_sb_kernel, tile=tile, nsub=nsub),
        grid=(batch, w // LANES, nq),
        in_specs=[qo, kv, kv],
        out_specs=qo,
        out_shape=jax.ShapeDtypeStruct((t, w), BF16),
        compiler_params=_params(("parallel", "parallel", "arbitrary")),
        name="sb_attention",
    )(q, k, v)


def _compress_kernel(xk_ref, xv_ref, pek_ref, pev_ref, wk_ref, wv_ref, c_ref, s1_ref, s2_ref,
                     kc_ref, vc_ref, *, nc):
    def comp(x_ref, pe_ref, w_ref):
        x = x_ref[0]
        top = _dot((x + pe_ref[0:1, :]).astype(BF16), w_ref[0])
        bot = _dot((x + pe_ref[1:2, :]).astype(BF16), w_ref[1])
        return top + pltpu.roll(bot, nc - 1, 0)

    kc = _rope(comp(xk_ref, pek_ref, wk_ref), c_ref[...], s1_ref[...], s2_ref[...])
    kc_ref[0] = kc.astype(BF16)
    vc_ref[0] = comp(xv_ref, pev_ref, wv_ref).astype(BF16)


def _prep_cmp(pe, w):
    half = CMP_LEN // 2
    w3 = w.reshape(CMP_LEN, HEAD_DIM, HEAD_DIM)
    eye = jnp.eye(NSA_KV_GROUPS, dtype=w.dtype)
    parts, pes = [], []
    for s in range(2):
        wh = jnp.einsum('ldo,gh->lgdho', w3[s * half:(s + 1) * half], eye)
        parts.append(wh.reshape(half * NSA_KVW, NSA_KVW))
        pes.append(jnp.broadcast_to(pe[s * half:(s + 1) * half, None, :],
                                    (half, NSA_KV_GROUPS, HEAD_DIM)).reshape(1, half * NSA_KVW))
    return jnp.concatenate(pes, 0), jnp.stack(parts).astype(BF16)


def _compress(kc_raw, vc_raw, pek, pev, wk, wv, ctabs, batch, seq):
    nc = seq // CMP_STRIDE
    cw = CMP_STRIDE * NSA_KVW
    xk = kc_raw.reshape(batch, nc, cw)
    xv = vc_raw.reshape(batch, nc, cw)
    xs = pl.BlockSpec((1, nc, cw), lambda b: (b, 0, 0))
    pes = pl.BlockSpec((2, cw), lambda b: (0, 0))
    ws = pl.BlockSpec((2, cw, NSA_KVW), lambda b: (0, 0, 0))
    tab = pl.BlockSpec((nc, LANES), lambda b: (0, 0))
    out = pl.BlockSpec((1, nc, LANES), lambda b: (b, 0, 0))
    return pl.pallas_call(
        functools.partial(_compress_kernel, nc=nc),
        grid=(batch,),
        in_specs=[xs, xs, pes, pes, ws, ws, tab, tab, tab],
        out_specs=[out, out],
        out_shape=[jax.ShapeDtypeStruct((batch, nc, LANES), BF16)] * 2,
        compiler_params=_params(("parallel",)),
        name="nsa_compress",
    )(xk, xv, pek, pev, wk, wv, *ctabs)


QB = 128


def _softmax_rows(s):
    m = jnp.max(s, axis=-1, keepdims=True)
    m = jnp.where(m > -jnp.inf, m, 0.0)
    e = jnp.exp2(s - m)
    d = jnp.sum(e, axis=-1, keepdims=True)
    return e / jnp.where(d > 0, d, 1.0)


def _nsa_kernel(q_ref, kc_ref, vc_ref, ks_ref, vs_ref, kw_ref, vw_ref, g_ref, mt_ref, oh_ref,
                o_ref, score_ref, knorm_ref, *, nc, tk, wlen, seq):
    g = pl.program_id(1)
    i = pl.program_id(2)
    q0 = i * QB
    rows = NSA_REP * QB

    q2 = q_ref[...]
    pr = lax.broadcasted_iota(jnp.int32, (NSA_REP * HEAD_DIM, LANES), 0)
    pc = lax.broadcasted_iota(jnp.int32, (NSA_REP * HEAD_DIM, LANES), 1) - g * HEAD_DIM
    in_group = (pc >= 0) & (pc < HEAD_DIM)
    q4 = jnp.concatenate(
        [_dot(q2, jnp.where(in_group & (pr - hh * HEAD_DIM == pc), 1.0, 0.0).astype(BF16)).astype(BF16)
         for hh in range(NSA_REP)], axis=0)

    sc = _dot_nt(q4, kc_ref[0])
    cend = lax.broadcasted_iota(jnp.int32, (rows, nc), 1) * CMP_STRIDE + (CMP_LEN - 1)
    tpos_c = q0 + (lax.broadcasted_iota(jnp.int32, (rows, nc), 0) & (QB - 1))
    p_cmp = _softmax_rows(jnp.where(cend <= tpos_c, sc, -jnp.inf))
    o_cmp = _dot(p_cmp.astype(BF16), vc_ref[0])

    p_sum = p_cmp[0:QB] + p_cmp[QB:2 * QB] + p_cmp[2 * QB:3 * QB] + p_cmp[3 * QB:4 * QB]
    overlap = mt_ref[...]
    parts = _dot(jnp.concatenate(_split3(p_sum), axis=0), overlap)
    imp_t = jnp.transpose(parts[0:QB] + parts[QB:2 * QB] + parts[2 * QB:3 * QB])
    nidx = lax.broadcasted_iota(jnp.int32, (SEL_LANES, QB), 0)
    cur = (q0 + lax.broadcasted_iota(jnp.int32, (SEL_LANES, QB), 1)) // SEL_LEN
    causal = nidx <= cur
    forced = (nidx == 0) | (nidx == cur) | (nidx == cur - 1)
    score = jnp.where(causal, jnp.where(forced, jnp.inf, imp_t), -jnp.inf)
    score_ref[...] = score

    def larger_step(m, cnt):
        return cnt + jnp.where(score_ref[pl.ds(m, 1), :] > score, 1.0, 0.0)

    def larger_pair(m2, cnt):
        return larger_step(2 * m2 + 1, larger_step(2 * m2, cnt))

    def rank_step(m, cnt):
        rowm = score_ref[pl.ds(m, 1), :]
        ge = jnp.where(rowm >= score, 1.0, 0.0)
        gt = jnp.where(rowm > score, 1.0, 0.0)
        return cnt + jnp.where(nidx > m, ge, gt)

    n_causal = (q0 + QB) // SEL_LEN
    zero_cnt = jnp.zeros((SEL_LANES, QB), F32)
    cnt = lax.fori_loop(0, n_causal // 2, larger_pair, zero_cnt)
    n_top = jnp.sum(jnp.where(causal & (cnt < float(SEL_TOP)), 1.0, 0.0), axis=0, keepdims=True)
    cnt = lax.cond(jnp.max(n_top) > float(SEL_TOP),
                   lambda c: lax.fori_loop(0, n_causal, rank_step, zero_cnt), lambda c: c, cnt)
    bias_t = jnp.where(causal & (cnt < float(SEL_TOP)), 0.0, SEL_NEG)
    selb4 = jnp.concatenate([jnp.transpose(bias_t)] * NSA_REP, axis=0)

    lane = lax.broadcasted_iota(jnp.int32, (1, LANES), 1)
    own_lanes = (lane >= g * HEAD_DIM) & (lane < (g + 1) * HEAD_DIM)

    @pl.when(i == 0)
    def _():
        feat = lax.broadcasted_iota(jnp.int32, (LANES, 1), 0)
        own_rows = (feat >= g * HEAD_DIM) & (feat < (g + 1) * HEAD_DIM)

        def chunk(n, best):
            kf = ks_ref[0, :, pl.ds(pl.multiple_of(n * tk, tk), tk)].astype(F32)
            kf = jnp.where(own_rows, kf, 0.0)
            return jnp.maximum(best, jnp.max(jnp.sum(kf * kf, axis=0, keepdims=True)))
        knorm_ref[0] = jnp.sqrt(lax.fori_loop(0, seq // tk, chunk, jnp.float32(0.0)))

    q4f = q4.astype(F32)
    shift = jnp.sqrt(jnp.sum(q4f * q4f, axis=-1, keepdims=True)) * (knorm_ref[0] * NORM_SLACK)
    shift = (shift * (1.0 + 2.0 ** -7)).astype(BF16).astype(F32)
    tight = 2.0 * jnp.max(shift) <= NSA_TIGHT
    qs = jnp.concatenate([q4, jnp.where(tight, selb4 - shift, selb4).astype(BF16)], axis=1)
    half = tk // 2
    diag0 = pl.multiple_of((q0 // half) * half, half)
    n_big = diag0 // tk
    odd_half = diag0 - n_big * tk > 0
    half0 = pl.multiple_of(n_big * tk, tk)

    def scores(k0, size, masked):
        kk = jnp.concatenate([ks_ref[0, :, pl.ds(k0, size)], oh_ref[:, pl.ds(k0, size)]], axis=0)
        vt = vs_ref[pl.ds(k0, size), :]
        vt = jnp.where(own_lanes, vt, jnp.ones_like(vt))
        s = _dot(qs, kk)
        if masked:
            kpos = k0 + lax.broadcasted_iota(jnp.int32, (rows, size), 1)
            tpos = q0 + (lax.broadcasted_iota(jnp.int32, (rows, size), 0) & (QB - 1))
            s = jnp.where(kpos <= tpos, s, -jnp.inf)
        return s, vt

    def sweep(step, carry):
        carry = lax.fori_loop(0, n_big, lambda j, cr: step(pl.multiple_of(j * tk, tk), tk, False, cr), carry)
        carry = lax.cond(odd_half, lambda cr: step(half0, half, False, cr), lambda cr: cr, carry)
        return step(diag0, half, True, carry)

    def fixed_shift_sweep(acc):
        def step(k0, size, masked, acc):
            s, vt = scores(k0, size, masked)
            return acc + _dot(jnp.exp2(s).astype(BF16), vt)
        return sweep(step, acc)

    def running_max_sweep(acc):
        def step(k0, size, masked, carry):
            m, acc = carry
            s, vt = scores(k0, size, masked)
            m_new = jnp.maximum(m, jnp.max(s, axis=-1, keepdims=True))
            return m_new, jnp.exp2(m - m_new) * acc + _dot(jnp.exp2(s - m_new).astype(BF16), vt)
        return sweep(step, (jnp.full((rows, 1), -jnp.inf, F32), acc))[1]

    acc_s = lax.cond(tight, fixed_shift_sweep, running_max_sweep, jnp.zeros((rows, LANES), F32))
    l_s = jnp.where(g == 0, acc_s[:, HEAD_DIM:HEAD_DIM + 1], acc_s[:, 0:1])
    o_slc = acc_s / l_s

    w0 = pl.multiple_of(jnp.maximum(q0 - WINDOW, 0), QB)
    sw = _dot_nt(q4, kw_ref[pl.ds(w0, wlen), :])
    kpos = w0 + lax.broadcasted_iota(jnp.int32, (rows, wlen), 1)
    tpos = q0 + (lax.broadcasted_iota(jnp.int32, (rows, wlen), 0) & (QB - 1))
    p_win = _softmax_rows(jnp.where((kpos <= tpos) & (kpos > tpos - WINDOW), sw, -jnp.inf))
    o_win = _dot(p_win.astype(BF16), vw_ref[pl.ds(w0, wlen), :])

    gates = g_ref[...]
    orow = lax.broadcasted_iota(jnp.int32, (LANES, NSA_REP * HEAD_DIM), 0) - g * HEAD_DIM
    ocol = lax.broadcasted_iota(jnp.int32, (LANES, NSA_REP * HEAD_DIM), 1)
    o_group = (orow >= 0) & (orow < HEAD_DIM)
    out = jnp.zeros((QB, NSA_REP * HEAD_DIM), F32)
    for hh in range(NSA_REP):
        sl = slice(hh * QB, (hh + 1) * QB)
        gate = lambda br: gates[:, br * NSA_REP + hh:br * NSA_REP + hh + 1]
        o_h = gate(0) * o_cmp[sl] + gate(1) * o_slc[sl] + gate(2) * o_win[sl]
        place = jnp.where(o_group & (ocol - hh * HEAD_DIM == orow), 1.0, 0.0).astype(BF16)
        out = out + _dot(o_h.astype(BF16), place)
    o_ref[...] = out.astype(BF16)


def _nsa_consts(seq):
    nc = seq // CMP_STRIDE
    c_start = jnp.arange(nc) * CMP_STRIDE
    sel_start = jnp.arange(SEL_LANES) * SEL_LEN
    real = (jnp.arange(SEL_LANES) < seq // SEL_LEN)[:, None] & (jnp.arange(nc) < (seq - CMP_LEN) // CMP_STRIDE + 1)[None, :]
    overlap = (c_start[None, :] < sel_start[:, None] + SEL_LEN) & (c_start[None, :] + CMP_LEN > sel_start[:, None])
    mt = jnp.transpose(overlap & real).astype(BF16)
    oh = (jnp.arange(seq)[None, :] // SEL_LEN == jnp.arange(SEL_LANES)[:, None]).astype(BF16)
    return mt, oh


def _nsa_attention(nq, kc, vc, ksl, vsl, kwn, vwn, gates, mt, oh, batch, seq, tk):
    t = nq.shape[0]
    nqb = seq // QB
    nc = seq // CMP_STRIDE
    wlen = WINDOW + QB
    gw = NSA_REP * HEAD_DIM
    qspec = pl.BlockSpec((QB, gw), lambda b, g, i: (b * nqb + i, g))
    cspec = pl.BlockSpec((1, nc, LANES), lambda b, g, i: (b, 0, 0))
    kvspec = pl.BlockSpec((seq, LANES), lambda b, g, i: (b, 0))
    gspec = pl.BlockSpec((QB, LANES), lambda b, g, i: (b * nqb + i, g))
    return pl.pallas_call(
        functools.partial(_nsa_kernel, nc=nc, tk=tk, wlen=wlen, seq=seq),
        grid=(batch, NSA_KV_GROUPS, nqb),
        in_specs=[qspec, cspec, cspec, pl.BlockSpec((1, LANES, seq), lambda b, g, i: (b, 0, 0)),
                  kvspec, kvspec, kvspec, gspec,
                  pl.BlockSpec((nc, SEL_LANES), lambda b, g, i: (0, 0)),
                  pl.BlockSpec((SEL_LANES, seq), lambda b, g, i: (0, 0))],
        out_specs=qspec,
        out_shape=jax.ShapeDtypeStruct((t, NSA_QW), BF16),
        scratch_shapes=[pltpu.VMEM((SEL_LANES, QB), F32), pltpu.SMEM((1,), F32)],
        compiler_params=_params(("parallel", "parallel", "arbitrary")),
        name="nsa_attention",
    )(nq, kc, vc, jnp.swapaxes(ksl.reshape(batch, seq, LANES), 1, 2), vsl, kwn, vwn, gates, mt, oh)


N_CUM_PARTS = 3


def _odd_proj_kernel(x_ref, g_ref, w_ref, bf_ref, place_ref, q_ref, k_ref, v_ref, stats_ref, carry_ref,
                     *, tm, nsb):
    i = pl.program_id(0)
    hb = _rmsnorm(x_ref[...], g_ref[...]).astype(BF16)
    f = _dot(hb, w_ref[:, 3 * FOX_W:3 * FOX_W + LANES]) + bf_ref[...]
    log_f = jnp.minimum(f, 0.0) - jnp.log1p(jnp.exp(-jnp.abs(f)))
    r = lax.broadcasted_iota(jnp.int32, (tm, tm), 0)
    c = lax.broadcasted_iota(jnp.int32, (tm, tm), 1)
    tri = jnp.where(c <= r, 1.0, 0.0).astype(BF16)
    local = _dot(tri, jnp.concatenate(_split3(log_f), axis=1))
    local = local[:, 0:LANES] + local[:, LANES:2 * LANES] + local[:, 2 * LANES:3 * LANES]

    @pl.when(i % nsb == 0)
    def _():
        carry_ref[...] = jnp.zeros_like(carry_ref)

    cum = local + carry_ref[0:1, :]
    carry_ref[0:1, :] = cum[tm - 1:tm, :]
    lane = lax.broadcasted_iota(jnp.int32, (1, LANES), 1)
    neg_cum = cum * (-LOG2E)
    hi, mid, lo = (part.astype(F32) for part in _split3(neg_cum))
    k_sq = jnp.zeros((1, LANES), F32)
    head_of_lane = jnp.where(lax.broadcasted_iota(jnp.int32, (LANES, LANES), 0) // HEAD_DIM
                             == lax.broadcasted_iota(jnp.int32, (LANES, LANES), 1), 1.0, 0.0).astype(BF16)
    packed = jnp.where(lane < FOX_HEADS, hi,
                       jnp.where(lane < 2 * FOX_HEADS, pltpu.roll(mid, FOX_HEADS, 1),
                                 pltpu.roll(lo, 2 * FOX_HEADS, 1))).astype(BF16)
    low = lane < HEAD_DIM
    q_one = jnp.where((lane >= HEAD_DIM) & (lane < HEAD_DIM + N_CUM_PARTS), 1.0, 0.0)
    v_one = jnp.where(lane >= HEAD_DIM, 1.0, 0.0)
    k_one = jnp.where((lane >= HEAD_DIM + N_CUM_PARTS) & (lane < HEAD_DIM + N_CUM_PARTS + N_SHIFT_PARTS), 1.0, 0.0)
    yq_all = _dot(hb, w_ref[:, 0:FOX_W]) * (QK_SCALE * LOG2E)
    yk_all = _dot(hb, w_ref[:, FOX_W:2 * FOX_W])
    yv_all = _dot(hb, w_ref[:, 2 * FOX_W:3 * FOX_W])
    extras_all = _dot(packed, place_ref[...])
    for p in range(FOX_HEADS // 2):
        pair = slice(p * LANES, (p + 1) * LANES)
        yq, yk, yv = yq_all[:, pair], yk_all[:, pair], yv_all[:, pair]
        pair_sq = jnp.max(_dot((yk * yk * (1.0 + 2.0 ** -7)).astype(BF16), head_of_lane), axis=0, keepdims=True)
        for h in range(2):
            hs = slice((2 * p + h) * LANES, (2 * p + h + 1) * LANES)
            head = (lambda y: y) if h == 0 else (lambda y: pltpu.roll(y, HEAD_DIM, 1))
            kh = jnp.where(low, head(yk), 0.0)
            k_sq = jnp.where(lane == 2 * p + h, pair_sq[:, h:h + 1], k_sq)
            q_ref[:, hs] = (jnp.where(low, head(yq), 0.0) + q_one).astype(BF16)
            k_ref[:, hs] = (kh + extras_all[:, hs] + k_one).astype(BF16)
            v_ref[:, hs] = (jnp.where(low, head(yv), 0.0) + v_one).astype(BF16)
    stats_ref[...] = jnp.zeros(stats_ref.shape, F32)
    stats_ref[0:1, :] = k_sq
    stats_ref[1:2, :] = neg_cum[0:1, :]


def _cum_placement():
    src = jnp.arange(LANES)[:, None]
    dst = jnp.arange(FOX_HEADS * LANES)[None, :]
    n, h = src // FOX_HEADS, src % FOX_HEADS
    return ((n < N_CUM_PARTS) & (dst == h * LANES + HEAD_DIM + n)).astype(BF16)


def _odd_proj(xf, g, w, bf, seq, tm):
    t, d = xf.shape
    nsb = seq // tm
    n = w.shape[1]
    wide = FOX_HEADS * LANES
    row = lambda width: pl.BlockSpec((tm, width), lambda i: (i, 0))
    return pl.pallas_call(
        functools.partial(_odd_proj_kernel, tm=tm, nsb=nsb),
        grid=(t // tm,),
        in_specs=[row(d), pl.BlockSpec((1, d), lambda i: (0, 0)),
                  pl.BlockSpec((d, n), lambda i: (0, 0)), pl.BlockSpec((1, LANES), lambda i: (0, 0)),
                  pl.BlockSpec((LANES, wide), lambda i: (0, 0))],
        out_specs=[row(wide)] * 3 + [pl.BlockSpec((8, LANES), lambda i: (i, 0))],
        out_shape=[jax.ShapeDtypeStruct((t, wide), BF16)] * 3 + [jax.ShapeDtypeStruct((t // tm * 8, LANES), F32)],
        scratch_shapes=[pltpu.VMEM((8, LANES), F32)],
        compiler_params=_params(("arbitrary",)),
        name="odd_proj",
    )(xf, g, w, bf, _cum_placement())


FOX_DEAD = -160.0
FOX_TIGHT = 64.0
N_SHIFT_PARTS = 2


def _fox_kernel(stats_ref, q_ref, k_ref, v_ref, o_ref, knorm_ref, ncum_ref, *, tq, tk, sub, dsub, seq, stats_rows):
    b, pair, i = pl.program_id(0), pl.program_id(1), pl.program_id(2)
    q0 = pl.multiple_of(i * tq, tq)
    n_full = q0 // tk
    nsub = tq // sub
    chains = [(h, r) for h in range(2) for r in range(nsub)]
    qs = [q_ref[r * sub:(r + 1) * sub, h * LANES:(h + 1) * LANES] for h, r in chains]
    lane = lax.broadcasted_iota(jnp.int32, (1, LANES), 1)
    feat = lane < HEAD_DIM
    n_stats, per = seq // stats_rows, tk // stats_rows

    def row_norms(x):
        xf = jnp.where(feat, x.astype(F32), 0.0)
        return jnp.sqrt(jnp.sum(xf * xf, axis=-1, keepdims=True))

    @pl.when(i == 0)
    def _():
        for h in range(2):
            def tile_stats(n, best, h=h):
                best = jnp.maximum(best, stats_ref[b * n_stats + n, 2 * pair + h])

                @pl.when(n % per == 0)
                def _():
                    ncum_ref[h, n // per] = stats_ref[b * n_stats + n, FOX_HEADS + 2 * pair + h]
                return best
            knorm_ref[h] = jnp.sqrt(lax.fori_loop(0, n_stats, tile_stats, jnp.float32(0.0)))

    k_norm = [knorm_ref[h] * (NORM_SLACK * (1.0 + 2.0 ** -8)) for h in range(2)]
    q_norm = [row_norms(q) for q in qs]
    qk_bound = [functools.reduce(jnp.maximum, [jnp.max(q_norm[h * nsub + r]) for r in range(nsub)]) * k_norm[h]
                for h in range(2)]
    tight = 2.0 * jnp.maximum(qk_bound[0], qk_bound[1]) <= FOX_TIGHT
    cum_q0 = [ncum_ref[h, n_full] for h in range(2)]

    def tiles(jt, h):
        k0 = pl.multiple_of(jt * tk, tk)
        hs = slice(h * LANES, (h + 1) * LANES)
        return k_ref[pl.ds(k0, tk), hs], v_ref[pl.ds(k0, tk), hs]

    shifts, accs, q_shifted = [], [], []
    for c, ((h, r), q) in enumerate(zip(chains, qs)):
        pieces = []
        for d in range(sub // dsub):
            row0 = r * sub + d * dsub
            nk = row0 + dsub
            kt = k_ref[pl.ds(q0, nk), h * LANES:(h + 1) * LANES]
            vt = v_ref[pl.ds(q0, nk), h * LANES:(h + 1) * LANES]
            qd = q[d * dsub:(d + 1) * dsub]
            s = _dot_nt(qd, kt)
            kpos = lax.broadcasted_iota(jnp.int32, (dsub, nk), 1)
            tpos = row0 + lax.broadcasted_iota(jnp.int32, (dsub, nk), 0)
            s = jnp.where(kpos <= tpos, s, -jnp.inf)
            m_diag = jnp.max(s, axis=-1, keepdims=True)
            bound = q_norm[c][d * dsub:(d + 1) * dsub] * k_norm[h] + cum_q0[h]
            shift = jnp.where(tight, jnp.maximum(m_diag, bound), m_diag)
            hi = shift.astype(BF16)
            rest = shift - hi.astype(F32)
            lo = (rest + jnp.abs(rest) * (2.0 ** -7)).astype(BF16)
            shift = hi.astype(F32) + lo.astype(F32)
            acc = _dot(jnp.exp2(s - shift).astype(BF16), vt)
            qsh = jnp.where(lane == HEAD_DIM + N_CUM_PARTS, -hi,
                            jnp.where(lane == HEAD_DIM + N_CUM_PARTS + 1, -lo, qd))
            pieces.append((shift, acc, qsh))
        shifts.append(jnp.concatenate([p[0] for p in pieces], axis=0))
        accs.append(jnp.concatenate([p[1] for p in pieces], axis=0))
        q_shifted.append(jnp.concatenate([p[2] for p in pieces], axis=0))

    def may_matter(jt):
        return jnp.maximum(ncum_ref[0, jt + 1] - cum_q0[0], ncum_ref[1, jt + 1] - cum_q0[1]) >= FOX_DEAD

    def fixed_shift_sweep(accs):
        def sweep_keys(k0, size, accs):
            out = []
            for (h, r), q, acc in zip(chains, q_shifted, accs):
                hs = slice(h * LANES, (h + 1) * LANES)
                p = jnp.exp2(_dot_nt(q, k_ref[pl.ds(k0, size), hs])).astype(BF16)
                out.append(acc + _dot(p, v_ref[pl.ds(k0, size), hs]))
            return tuple(out)

        def body(carry):
            end, accs = carry
            return end - 2, sweep_keys(pl.multiple_of((end - 2) * tk, tk), 2 * tk, accs)

        end, accs = lax.while_loop(lambda cr: (cr[0] >= 2) & may_matter(jnp.maximum(cr[0] - 1, 0)), body,
                                   (n_full, accs))
        return lax.cond((end == 1) & may_matter(0), lambda a: sweep_keys(0, tk, a), lambda a: a, accs)

    def running_max_sweep(accs):
        def alive(jt, ms):
            go = None
            for h in range(2):
                m_min = functools.reduce(jnp.minimum, [jnp.min(ms[h * nsub + r]) for r in range(nsub)])
                live = qk_bound[h] + ncum_ref[h, jt + 1] - m_min >= FOX_DEAD
                go = live if go is None else (go | live)
            return go

        def body(carry):
            jt, _, ms, accs = carry
            new_m, new_acc = [], []
            for (h, r), q, m, acc in zip(chains, qs, ms, accs):
                kt, vt = tiles(jt, h)
                s = _dot_nt(q, kt)
                m_new = jnp.maximum(m, jnp.max(s, axis=-1, keepdims=True))
                new_acc.append(jnp.exp2(m - m_new) * acc + _dot(jnp.exp2(s - m_new).astype(BF16), vt))
                new_m.append(m_new)
            new_m = tuple(new_m)
            return jt - 1, alive(jnp.maximum(jt - 1, 0), new_m), new_m, tuple(new_acc)

        ms = tuple(shifts)
        return lax.while_loop(lambda cr: (cr[0] >= 0) & cr[1], body,
                              (n_full - 1, alive(jnp.maximum(n_full - 1, 0), ms), ms, accs))[3]

    accs = lax.cond(tight, fixed_shift_sweep, running_max_sweep, tuple(accs))
    heads = []
    for h in range(2):
        acc = jnp.concatenate([accs[h * nsub + r] for r in range(nsub)], axis=0)
        heads.append(acc * (1.0 / acc[:, HEAD_DIM:HEAD_DIM + 1]))
    o_ref[...] = jnp.where(feat, heads[0], pltpu.roll(heads[1], HEAD_DIM, 1)).astype(BF16)


def _fox_attention(q, k, v, stats, batch, seq, tq, tk, sub, dsub):
    t = q.shape[0]
    nq = seq // tq
    kv = pl.BlockSpec((seq, 2 * LANES), lambda b, p, i: (b, p))
    stats_rows = 8 * t // stats.shape[0]
    assert tk % stats_rows == 0
    tile_stats = stats.reshape(t // stats_rows, 8, LANES)
    tile_stats = jnp.concatenate([tile_stats[:, 0, :FOX_HEADS], tile_stats[:, 1, :FOX_HEADS]], axis=1)
    return pl.pallas_call(
        functools.partial(_fox_kernel, tq=tq, tk=tk, sub=sub, dsub=dsub, seq=seq, stats_rows=stats_rows),
        grid=(batch, FOX_HEADS // 2, nq),
        in_specs=[pl.BlockSpec(memory_space=pltpu.SMEM),
                  pl.BlockSpec((tq, 2 * LANES), lambda b, p, i: (b * nq + i, p)), kv, kv],
        out_specs=pl.BlockSpec((tq, LANES), lambda b, p, i: (b * nq + i, p)),
        out_shape=jax.ShapeDtypeStruct((t, FOX_W), BF16),
        scratch_shapes=[pltpu.SMEM((2,), F32), pltpu.SMEM((2, seq // tk), F32)],
        compiler_params=_params(("parallel", "parallel", "arbitrary")),
        name="fox_attention",
    )(tile_stats, q, k, v)


FFN_CHUNKS = (2816,)
assert sum(FFN_CHUNKS) == D_FF
HALO = 8


def _ffn_kernel(*refs, tm, nsb, final, n_attn):
    attn_refs, wattn_ref, refs = refs[:n_attn], refs[n_attn], refs[n_attn + 1:]
    if final:
        x_ref, g_ref, win_ref, cw_ref, cb_ref, wout_ref, fn_ref, o_ref, a_scr, halo_scr = refs
    else:
        x_ref, g_ref, win_ref, cw_ref, cb_ref, wout_ref, o_ref, a_scr, halo_scr = refs
    i = pl.program_id(0)

    @pl.when(i % nsb == 0)
    def _():
        halo_scr[...] = jnp.zeros((HALO, D_FF), F32)

    x = x_ref[...]
    off = 0
    for a_ref in attn_refs:
        width = a_ref.shape[1]
        x = x + _dot(a_ref[...], wattn_ref[off:off + width, :])
        off += width
    hb = _rmsnorm(x, g_ref[...]).astype(BF16)
    acc = jnp.zeros((tm, D_MODEL), F32)
    start = 0
    for width in FFN_CHUNKS:
        sl = slice(start, start + width)
        a = _dot(hb, win_ref[:, sl])
        b = _dot(hb, win_ref[:, D_FF + start:D_FF + start + width])
        start += width
        a_scr[0:HALO, 0:width] = halo_scr[:, sl]
        a_scr[HALO:HALO + tm, 0:width] = a
        halo_scr[:, sl] = a[tm - HALO:tm, :]
        conv = (cw_ref[0:1, sl] * a_scr[HALO - 2:HALO - 2 + tm, 0:width]
                + cw_ref[1:2, sl] * a_scr[HALO - 1:HALO - 1 + tm, 0:width]
                + cw_ref[2:3, sl] * a + cb_ref[:, sl])
        gated = conv * (1.0 / (1.0 + jnp.exp(-conv))) * b
        acc = acc + _dot(gated.astype(BF16), wout_ref[sl, :])
    y = x + acc
    if final:
        y = _rmsnorm(y, fn_ref[...])
    o_ref[...] = y


def _mixer_out_ffn(xf, attn_outs, w_attn, g, w_in, conv_w, conv_b, w_out, final_norm, seq, tm):
    t, d = xf.shape
    nsb = seq // tm
    row = pl.BlockSpec((tm, d), lambda i: (i, 0))
    const = lambda shape: pl.BlockSpec(shape, lambda i: (0, 0), pipeline_mode=pl.Buffered(1))
    small = lambda shape: pl.BlockSpec(shape, lambda i: (0, 0))
    final = final_norm is not None
    in_specs = [pl.BlockSpec((tm, a.shape[1]), lambda i: (i, 0)) for a in attn_outs] + [const(w_attn.shape)]
    in_specs += [row, small((1, d)), const(w_in.shape), small(conv_w.shape), small((1, D_FF)), const(w_out.shape)]
    args = list(attn_outs) + [w_attn, xf, g, w_in, conv_w, conv_b, w_out]
    if final:
        in_specs.append(small((1, d)))
        args.append(final_norm)
    return pl.pallas_call(
        functools.partial(_ffn_kernel, tm=tm, nsb=nsb, final=final, n_attn=len(attn_outs)),
        grid=(t // tm,),
        in_specs=in_specs,
        out_specs=row,
        out_shape=jax.ShapeDtypeStruct((t, d), F32),
        scratch_shapes=[pltpu.VMEM((tm + HALO, max(FFN_CHUNKS)), F32), pltpu.VMEM((HALO, D_FF), F32)],
        compiler_params=_params(("arbitrary",)),
        name="conv_glu_ffn",
    )(*args)


def kernel(x, attn_norm, ffn_norm, ev_w_in, ev_cmp_pos_k, ev_cmp_pos_v, ev_cmp_w_k, ev_cmp_w_v, ev_w_out,
           od_w_in, od_b_f, od_w_out, ffn_w_in, ffn_conv_w, ffn_conv_b, ffn_w_out, final_norm):
    batch, seq, d = x.shape
    t = batch * seq
    depth = attn_norm.shape[0]
    tm, ffn_tm = min(512, seq), min(512, seq)
    sb_tile, sb_nsub = min(256, seq), 2
    fox_tq, fox_tk, fox_sub, fox_dsub = min(512, seq), min(512, seq), 512, 512
    sel_tk = min(1024, seq)

    xf = x.reshape(t, d)
    tabs = _rope_tables(jnp.arange(seq))
    ctabs = _rope_tables(jnp.arange(seq // CMP_STRIDE) * CMP_STRIDE + (CMP_LEN - 1))
    mt, oh = _nsa_consts(seq)

    for layer in range(depth):
        g_attn = attn_norm[layer].reshape(1, d)
        if layer % 2 == 0:
            e = layer // 2
            (sbq, sbk, sbv, nq, kc_raw, vc_raw, ksl, vsl, kwn, vwn, gates) = _even_proj(
                xf, g_attn, _prep_even_w(ev_w_in[e]), tabs, seq, tm)
            pek, wk = _prep_cmp(ev_cmp_pos_k[e], ev_cmp_w_k[e])
            pev, wv = _prep_cmp(ev_cmp_pos_v[e], ev_cmp_w_v[e])
            kc, vc = _compress(kc_raw, vc_raw, pek, pev, wk, wv, ctabs, batch, seq)
            o_sb = _sb_attention(sbq, sbk, sbv, batch, seq, sb_tile, sb_nsub)
            o_nsa = _nsa_attention(nq, kc, vc, ksl, vsl, kwn, vwn, gates, mt, oh, batch, seq, sel_tk)
            attn_outs, w_attn = [o_sb, o_nsa], ev_w_out[e].astype(BF16)
        else:
            o = layer // 2
            w = jnp.pad(od_w_in[o], ((0, 0), (0, LANES - FOX_HEADS))).astype(BF16)
            bf = jnp.pad(od_b_f[o], (0, LANES - FOX_HEADS)).reshape(1, LANES)
            q, k, v, stats = _odd_proj(xf, g_attn, w, bf, seq, tm)
            o_fox = _fox_attention(q, k, v, stats, batch, seq, fox_tq, fox_tk, fox_sub, fox_dsub)
            attn_outs, w_attn = [o_fox], od_w_out[o].astype(BF16)
        last = layer == depth - 1
        xf = _mixer_out_ffn(xf, attn_outs, w_attn, ffn_norm[layer].reshape(1, d), ffn_w_in[layer].astype(BF16),
                            ffn_conv_w[layer], ffn_conv_b[layer].reshape(1, D_FF), ffn_w_out[layer].astype(BF16),
                            final_norm.reshape(1, d) if last else None, seq, ffn_tm)
    return xf.reshape(batch, seq, d)
```

```python
import functools
import math

import jax
import jax.numpy as jnp
from jax import lax
from jax.experimental import pallas as pl
from jax.experimental.pallas import tpu as pltpu

F32, BF16 = jnp.float32, jnp.bfloat16

D_MODEL = 1024
HEAD_DIM = 64
N_HEADS = D_MODEL // HEAD_DIM
SB_HEADS = N_HEADS // 2
NSA_HEADS = N_HEADS - SB_HEADS
NSA_KV_GROUPS = 2
NSA_REP = NSA_HEADS // NSA_KV_GROUPS
FOX_HEADS = N_HEADS
CMP_LEN = 32
CMP_STRIDE = 16
SEL_LEN = 64
SEL_TOP = 16
WINDOW = 512
N_BRANCH = 3
ROPE_THETA = 500000.0
ROT_DIM = HEAD_DIM // 4
D_FF = 2816
CONV_WIDTH = 3
NORM_EPS = 1e-6
SB_W = SB_HEADS * HEAD_DIM
NSA_QW = NSA_HEADS * HEAD_DIM
NSA_KVW = NSA_KV_GROUPS * HEAD_DIM
FOX_W = FOX_HEADS * HEAD_DIM

LANES = 128
SEL_LANES = 128
QK_SCALE = HEAD_DIM ** -0.5
LOG2E = math.log2(math.e)
SEL_NEG = -(2.0 ** 30)
NSA_TIGHT = 86.0
NORM_SLACK = 1.001
VMEM_LIMIT = 56 * 2 ** 20

_NT = (((1,), (1,)), ((), ()))


def _params(sem):
    return pltpu.CompilerParams(dimension_semantics=sem, vmem_limit_bytes=VMEM_LIMIT)


def _dot(a, b):
    return jnp.dot(a, b, preferred_element_type=F32)


def _dot_nt(a, b):
    return lax.dot_general(a, b, _NT, preferred_element_type=F32)


def _rmsnorm(x, g):
    ms = jnp.mean(x * x, axis=-1, keepdims=True)
    return (x * lax.rsqrt(ms + NORM_EPS)) * g


def _rope(y, c, s1, s2):
    return y * c + pltpu.roll(y, LANES - ROT_DIM // 2, 1) * s1 + pltpu.roll(y, ROT_DIM // 2, 1) * s2


def _split3(x):
    hi = x.astype(BF16)
    r1 = x - hi.astype(F32)
    mid = r1.astype(BF16)
    lo = (r1 - mid.astype(F32)).astype(BF16)
    return hi, mid, lo


def _rope_tables(pos):
    half = ROT_DIM // 2
    inv_freq = ROPE_THETA ** (-(jnp.arange(half, dtype=F32) * 2.0 / ROT_DIM))
    ang = pos.astype(F32)[:, None] * inv_freq[None, :]
    cos, sin = jnp.cos(ang), jnp.sin(ang)
    n = pos.shape[0]
    one = jnp.ones((n, HEAD_DIM - ROT_DIM), F32)
    zero = jnp.zeros((n, HEAD_DIM - ROT_DIM), F32)
    z8 = jnp.zeros((n, half), F32)
    c = jnp.concatenate([cos, cos, one], -1)
    s1 = jnp.concatenate([-sin, z8, zero], -1)
    s2 = jnp.concatenate([z8, sin, zero], -1)
    two = lambda t: jnp.concatenate([t, t], -1)
    return two(c), two(s1), two(s2)


def _even_proj_kernel(x_ref, g_ref, w_ref, c_ref, s1_ref, s2_ref,
                      sbq_ref, sbk_ref, sbv_ref, nq_ref, kc_ref, vc_ref,
                      ksl_ref, vsl_ref, kwn_ref, vwn_ref, gate_ref):
    hb = _rmsnorm(x_ref[...], g_ref[...]).astype(BF16)
    c, s1, s2 = c_ref[...], s1_ref[...], s2_ref[...]

    sb = _dot(hb, w_ref[:, 0:3 * SB_W])
    sbq_ref[...] = (sb[:, 0:SB_W] * QK_SCALE).astype(BF16)
    sbk_ref[...] = sb[:, SB_W:2 * SB_W].astype(BF16)
    sbv_ref[...] = sb[:, 2 * SB_W:3 * SB_W].astype(BF16)
    nsa = _dot(hb, w_ref[:, 3 * SB_W:])
    piece = lambda j: nsa[:, j * LANES:(j + 1) * LANES]
    n_q = NSA_QW // LANES
    for j in range(n_q):
        nq_ref[:, j * LANES:(j + 1) * LANES] = (_rope(piece(j), c, s1, s2) * (QK_SCALE * LOG2E)).astype(BF16)
    kc_ref[...] = piece(n_q)
    vc_ref[...] = piece(n_q + 1)
    ksl_ref[...] = _rope(piece(n_q + 2), c, s1, s2).astype(BF16)
    vsl_ref[...] = piece(n_q + 3).astype(BF16)
    kwn_ref[...] = _rope(piece(n_q + 4), c, s1, s2).astype(BF16)
    vwn_ref[...] = piece(n_q + 5).astype(BF16)
    gate_ref[...] = 1.0 / (1.0 + jnp.exp(-piece(n_q + 6)))


def _even_proj(xf, g, w, tabs, seq, tm):
    t, d = xf.shape
    nsb = seq // tm
    n = w.shape[1]
    row = lambda width: pl.BlockSpec((tm, width), lambda i: (i, 0))
    tab = pl.BlockSpec((tm, LANES), lambda i: (i % nsb, 0))
    out_shape = (
        [jax.ShapeDtypeStruct((t, SB_W), BF16)] * 3
        + [jax.ShapeDtypeStruct((t, NSA_QW), BF16)]
        + [jax.ShapeDtypeStruct((t, LANES), F32)] * 2
        + [jax.ShapeDtypeStruct((t, LANES), BF16)] * 4
        + [jax.ShapeDtypeStruct((t, LANES), F32)]
    )
    out_specs = [row(SB_W)] * 3 + [row(NSA_QW)] + [row(LANES)] * 7
    return pl.pallas_call(
        _even_proj_kernel,
        grid=(t // tm,),
        in_specs=[row(d), pl.BlockSpec((1, d), lambda i: (0, 0)),
                  pl.BlockSpec((d, n), lambda i: (0, 0)), tab, tab, tab],
        out_specs=out_specs,
        out_shape=out_shape,
        compiler_params=_params(("parallel",)),
        name="even_proj",
    )(xf, g, w, *tabs)


def _prep_even_w(w):
    return jnp.pad(w, ((0, 0), (0, LANES - NSA_HEADS * N_BRANCH))).astype(BF16)


SB_DEAD = -105.0


def _sb_kernel(q_ref, k_ref, v_ref, o_ref, *, tile, nsub):
    i = pl.program_id(2)
    lane = lax.broadcasted_iota(jnp.int32, (1, LANES), 1)
    r = lax.broadcasted_iota(jnp.int32, (tile, tile), 0)
    c = lax.broadcasted_iota(jnp.int32, (tile, tile), 1)
    later = jnp.where(r > c, 1.0, 0.0).astype(BF16)
    diag = c < r
    chains = [(h, sub) for h in range(2) for sub in range(nsub)]
    qs = []
    for h, sub in chains:
        q = q_ref[sub * tile:(sub + 1) * tile, :]
        qs.append(jnp.where((lane < HEAD_DIM) if h == 0 else (lane >= HEAD_DIM), q, jnp.zeros_like(q)))

    def step(jt, state, mode):
        k0 = pl.multiple_of(jt * tile, tile)
        kt = k_ref[pl.ds(k0, tile), :]
        vt = v_ref[pl.ds(k0, tile), :]
        out = []
        for c, (acc, keep_sum) in enumerate(state):
            if mode[c] is None:
                out.append((acc, keep_sum))
                continue
            z = _dot_nt(qs[c], kt)
            ls = jnp.minimum(z, 0.0) - jnp.log(1.0 + jnp.exp(-jnp.abs(z)))
            lk = ls - z
            if mode[c]:
                lk = jnp.where(diag, lk, 0.0)
            hi = lk.astype(BF16)
            lo = (lk - hi.astype(F32)).astype(BF16)
            both = _dot(jnp.concatenate([hi, lo], axis=0), later)
            after = both[0:tile] + both[tile:2 * tile]
            a = jnp.exp(ls + after + keep_sum)
            if mode[c]:
                a = jnp.where(diag, a, 0.0)
            out.append((acc + _dot(a.astype(BF16), vt), keep_sum + after[:, 0:1] + lk[:, 0:1]))
        return tuple(out)

    def alive(state):
        return functools.reduce(jnp.maximum, [jnp.max(keep_sum) for _, keep_sum in state]) >= SB_DEAD

    state = tuple((jnp.zeros((tile, LANES), F32), jnp.zeros((tile, 1), F32)) for _ in chains)
    for top in range(nsub - 1, -1, -1):
        mode = [None if sub < top else sub == top for _, sub in chains]
        state = step(nsub * i + top, state, mode)

    def body(carry):
        jt, _, state = carry
        state = step(jt, state, [False] * len(chains))
        return jt - 1, alive(state), state

    _, _, state = lax.while_loop(lambda cr: (cr[0] >= 0) & cr[1], body, (nsub * i - 1, alive(state), state))
    heads = [jnp.concatenate([state[h * nsub + sub][0] for sub in range(nsub)], axis=0) for h in range(2)]
    o_ref[...] = jnp.where(lane < HEAD_DIM, heads[0], heads[1]).astype(BF16)


def _sb_attention(q, k, v, batch, seq, tile, nsub):
    t, w = q.shape
    nq = seq // (tile * nsub)
    kv = pl.BlockSpec((seq, LANES), lambda b, p, i: (b, p))
    qo = pl.BlockSpec((tile * nsub, LANES), lambda b, p, i: (b * nq + i, p))
    return pl.pallas_call(
        functools.partial(_sb_kernel, tile=tile, nsub=nsub),
        grid=(batch, w // LANES, nq),
        in_specs=[qo, kv, kv],
        out_specs=qo,
        out_shape=jax.ShapeDtypeStruct((t, w), BF16),
        compiler_params=_params(("parallel", "parallel", "arbitrary")),
        name="sb_attention",
    )(q, k, v)


def _compress_kernel(xk_ref, xv_ref, pek_ref, pev_ref, wk_ref, wv_ref, c_ref, s1_ref, s2_ref,
                     kc_ref, vc_ref, *, nc):
    def comp(x_ref, pe_ref, w_ref):
        x = x_ref[0]
        top = _dot((x + pe_ref[0:1, :]).astype(BF16), w_ref[0])
        bot = _dot((x + pe_ref[1:2, :]).astype(BF16), w_ref[1])
        return top + pltpu.roll(bot, nc - 1, 0)

    kc = _rope(comp(xk_ref, pek_ref, wk_ref), c_ref[...], s1_ref[...], s2_ref[...])
    kc_ref[0] = kc.astype(BF16)
    vc_ref[0] = comp(xv_ref, pev_ref, wv_ref).astype(BF16)


def _prep_cmp(pe, w):
    half = CMP_LEN // 2
    w3 = w.reshape(CMP_LEN, HEAD_DIM, HEAD_DIM)
    eye = jnp.eye(NSA_KV_GROUPS, dtype=w.dtype)
    parts, pes = [], []
    for s in range(2):
        wh = jnp.einsum('ldo,gh->lgdho', w3[s * half:(s + 1) * half], eye)
        parts.append(wh.reshape(half * NSA_KVW, NSA_KVW))
        pes.append(jnp.broadcast_to(pe[s * half:(s + 1) * half, None, :],
                                    (half, NSA_KV_GROUPS, HEAD_DIM)).reshape(1, half * NSA_KVW))
    return jnp.concatenate(pes, 0), jnp.stack(parts).astype(BF16)


def _compress(kc_raw, vc_raw, pek, pev, wk, wv, ctabs, batch, seq):
    nc = seq // CMP_STRIDE
    cw = CMP_STRIDE * NSA_KVW
    xk = kc_raw.reshape(batch, nc, cw)
    xv = vc_raw.reshape(batch, nc, cw)
    xs = pl.BlockSpec((1, nc, cw), lambda b: (b, 0, 0))
    pes = pl.BlockSpec((2, cw), lambda b: (0, 0))
    ws = pl.BlockSpec((2, cw, NSA_KVW), lambda b: (0, 0, 0))
    tab = pl.BlockSpec((nc, LANES), lambda b: (0, 0))
    out = pl.BlockSpec((1, nc, LANES), lambda b: (b, 0, 0))
    return pl.pallas_call(
        functools.partial(_compress_kernel, nc=nc),
        grid=(batch,),
        in_specs=[xs, xs, pes, pes, ws, ws, tab, tab, tab],
        out_specs=[out, out],
        out_shape=[jax.ShapeDtypeStruct((batch, nc, LANES), BF16)] * 2,
        compiler_params=_params(("parallel",)),
        name="nsa_compress",
    )(xk, xv, pek, pev, wk, wv, *ctabs)


QB = 128


def _softmax_rows(s):
    m = jnp.max(s, axis=-1, keepdims=True)
    m = jnp.where(m > -jnp.inf, m, 0.0)
    e = jnp.exp2(s - m)
    d = jnp.sum(e, axis=-1, keepdims=True)
    return e / jnp.where(d > 0, d, 1.0)


def _nsa_kernel(q_ref, kc_ref, vc_ref, ks_ref, vs_ref, kw_ref, vw_ref, g_ref, mt_ref, oh_ref,
                o_ref, score_ref, knorm_ref, *, nc, tk, wlen, seq):
    g = pl.program_id(1)
    i = pl.program_id(2)
    q0 = i * QB
    rows = NSA_REP * QB

    q2 = q_ref[...]
    pr = lax.broadcasted_iota(jnp.int32, (NSA_REP * HEAD_DIM, LANES), 0)
    pc = lax.broadcasted_iota(jnp.int32, (NSA_REP * HEAD_DIM, LANES), 1) - g * HEAD_DIM
    in_group = (pc >= 0) & (pc < HEAD_DIM)
    q4 = jnp.concatenate(
        [_dot(q2, jnp.where(in_group & (pr - hh * HEAD_DIM == pc), 1.0, 0.0).astype(BF16)).astype(BF16)
         for hh in range(NSA_REP)], axis=0)

    sc = _dot_nt(q4, kc_ref[0])
    cend = lax.broadcasted_iota(jnp.int32, (rows, nc), 1) * CMP_STRIDE + (CMP_LEN - 1)
    tpos_c = q0 + (lax.broadcasted_iota(jnp.int32, (rows, nc), 0) & (QB - 1))
    p_cmp = _softmax_rows(jnp.where(cend <= tpos_c, sc, -jnp.inf))
    o_cmp = _dot(p_cmp.astype(BF16), vc_ref[0])

    p_sum = p_cmp[0:QB] + p_cmp[QB:2 * QB] + p_cmp[2 * QB:3 * QB] + p_cmp[3 * QB:4 * QB]
    overlap = mt_ref[...]
    parts = _dot(jnp.concatenate(_split3(p_sum), axis=0), overlap)
    imp_t = jnp.transpose(parts[0:QB] + parts[QB:2 * QB] + parts[2 * QB:3 * QB])
    nidx = lax.broadcasted_iota(jnp.int32, (SEL_LANES, QB), 0)
    cur = (q0 + lax.broadcasted_iota(jnp.int32, (SEL_LANES, QB), 1)) // SEL_LEN
    causal = nidx <= cur
    forced = (nidx == 0) | (nidx == cur) | (nidx == cur - 1)
    score = jnp.where(causal, jnp.where(forced, jnp.inf, imp_t), -jnp.inf)
    score_ref[...] = score

    def larger_step(m, cnt):
        return cnt + jnp.where(score_ref[pl.ds(m, 1), :] > score, 1.0, 0.0)

    def larger_pair(m2, cnt):
        return larger_step(2 * m2 + 1, larger_step(2 * m2, cnt))

    def rank_step(m, cnt):
        rowm = score_ref[pl.ds(m, 1), :]
        ge = jnp.where(rowm >= score, 1.0, 0.0)
        gt = jnp.where(rowm > score, 1.0, 0.0)
        return cnt + jnp.where(nidx > m, ge, gt)

    n_causal = (q0 + QB) // SEL_LEN
    zero_cnt = jnp.zeros((SEL_LANES, QB), F32)
    cnt = lax.fori_loop(0, n_causal // 2, larger_pair, zero_cnt)
    n_top = jnp.sum(jnp.where(causal & (cnt < float(SEL_TOP)), 1.0, 0.0), axis=0, keepdims=True)
    cnt = lax.cond(jnp.max(n_top) > float(SEL_TOP),
                   lambda c: lax.fori_loop(0, n_causal, rank_step, zero_cnt), lambda c: c, cnt)
    bias_t = jnp.where(causal & (cnt < float(SEL_TOP)), 0.0, SEL_NEG)
    selb4 = jnp.concatenate([jnp.transpose(bias_t)] * NSA_REP, axis=0)

    lane = lax.broadcasted_iota(jnp.int32, (1, LANES), 1)
    own_lanes = (lane >= g * HEAD_DIM) & (lane < (g + 1) * HEAD_DIM)

    @pl.when(i == 0)
    def _():
        feat = lax.broadcasted_iota(jnp.int32, (LANES, 1), 0)
        own_rows = (feat >= g * HEAD_DIM) & (feat < (g + 1) * HEAD_DIM)

        def chunk(n, best):
            kf = ks_ref[0, :, pl.ds(pl.multiple_of(n * tk, tk), tk)].astype(F32)
            kf = jnp.where(own_rows, kf, 0.0)
            return jnp.maximum(best, jnp.max(jnp.sum(kf * kf, axis=0, keepdims=True)))
        knorm_ref[0] = jnp.sqrt(lax.fori_loop(0, seq // tk, chunk, jnp.float32(0.0)))

    q4f = q4.astype(F32)
    shift = jnp.sqrt(jnp.sum(q4f * q4f, axis=-1, keepdims=True)) * (knorm_ref[0] * NORM_SLACK)
    shift = (shift * (1.0 + 2.0 ** -7)).astype(BF16).astype(F32)
    tight = 2.0 * jnp.max(shift) <= NSA_TIGHT
    qs = jnp.concatenate([q4, jnp.where(tight, selb4 - shift, selb4).astype(BF16)], axis=1)
    half = tk // 2
    diag0 = pl.multiple_of((q0 // half) * half, half)
    n_big = diag0 // tk
    odd_half = diag0 - n_big * tk > 0
    half0 = pl.multiple_of(n_big * tk, tk)

    def scores(k0, size, masked):
        kk = jnp.concatenate([ks_ref[0, :, pl.ds(k0, size)], oh_ref[:, pl.ds(k0, size)]], axis=0)
        vt = vs_ref[pl.ds(k0, size), :]
        vt = jnp.where(own_lanes, vt, jnp.ones_like(vt))
        s = _dot(qs, kk)
        if masked:
            kpos = k0 + lax.broadcasted_iota(jnp.int32, (rows, size), 1)
            tpos = q0 + (lax.broadcasted_iota(jnp.int32, (rows, size), 0) & (QB - 1))
            s = jnp.where(kpos <= tpos, s, -jnp.inf)
        return s, vt

    def sweep(step, carry):
        carry = lax.fori_loop(0, n_big, lambda j, cr: step(pl.multiple_of(j * tk, tk), tk, False, cr), carry)
        carry = lax.cond(odd_half, lambda cr: step(half0, half, False, cr), lambda cr: cr, carry)
        return step(diag0, half, True, carry)

    def fixed_shift_sweep(acc):
        def step(k0, size, masked, acc):
            s, vt = scores(k0, size, masked)
            return acc + _dot(jnp.exp2(s).astype(BF16), vt)
        return sweep(step, acc)

    def running_max_sweep(acc):
        def step(k0, size, masked, carry):
            m, acc = carry
            s, vt = scores(k0, size, masked)
            m_new = jnp.maximum(m, jnp.max(s, axis=-1, keepdims=True))
            return m_new, jnp.exp2(m - m_new) * acc + _dot(jnp.exp2(s - m_new).astype(BF16), vt)
        return sweep(step, (jnp.full((rows, 1), -jnp.inf, F32), acc))[1]

    acc_s = lax.cond(tight, fixed_shift_sweep, running_max_sweep, jnp.zeros((rows, LANES), F32))
    l_s = jnp.where(g == 0, acc_s[:, HEAD_DIM:HEAD_DIM + 1], acc_s[:, 0:1])
    o_slc = acc_s / l_s

    w0 = pl.multiple_of(jnp.maximum(q0 - WINDOW, 0), QB)
    sw = _dot_nt(q4, kw_ref[pl.ds(w0, wlen), :])
    kpos = w0 + lax.broadcasted_iota(jnp.int32, (rows, wlen), 1)
    tpos = q0 + (lax.broadcasted_iota(jnp.int32, (rows, wlen), 0) & (QB - 1))
    p_win = _softmax_rows(jnp.where((kpos <= tpos) & (kpos > tpos - WINDOW), sw, -jnp.inf))
    o_win = _dot(p_win.astype(BF16), vw_ref[pl.ds(w0, wlen), :])

    gates = g_ref[...]
    orow = lax.broadcasted_iota(jnp.int32, (LANES, NSA_REP * HEAD_DIM), 0) - g * HEAD_DIM
    ocol = lax.broadcasted_iota(jnp.int32, (LANES, NSA_REP * HEAD_DIM), 1)
    o_group = (orow >= 0) & (orow < HEAD_DIM)
    out = jnp.zeros((QB, NSA_REP * HEAD_DIM), F32)
    for hh in range(NSA_REP):
        sl = slice(hh * QB, (hh + 1) * QB)
        def gate(br, hh=hh):
            lanes = [(grp * NSA_REP + hh) * N_BRANCH + br for grp in range(NSA_KV_GROUPS)]
            return jnp.where(g == 0, gates[:, lanes[0]:lanes[0] + 1], gates[:, lanes[1]:lanes[1] + 1])
        o_h = gate(0) * o_cmp[sl] + gate(1) * o_slc[sl] + gate(2) * o_win[sl]
        place = jnp.where(o_group & (ocol - hh * HEAD_DIM == orow), 1.0, 0.0).astype(BF16)
        out = out + _dot(o_h.astype(BF16), place)
    o_ref[...] = out.astype(BF16)


def _nsa_consts(seq):
    nc = seq // CMP_STRIDE
    c_start = jnp.arange(nc) * CMP_STRIDE
    sel_start = jnp.arange(SEL_LANES) * SEL_LEN
    real = (jnp.arange(SEL_LANES) < seq // SEL_LEN)[:, None] & (jnp.arange(nc) < (seq - CMP_LEN) // CMP_STRIDE + 1)[None, :]
    overlap = (c_start[None, :] < sel_start[:, None] + SEL_LEN) & (c_start[None, :] + CMP_LEN > sel_start[:, None])
    mt = jnp.transpose(overlap & real).astype(BF16)
    oh = (jnp.arange(seq)[None, :] // SEL_LEN == jnp.arange(SEL_LANES)[:, None]).astype(BF16)
    return mt, oh


def _nsa_attention(nq, kc, vc, ksl, vsl, kwn, vwn, gates, mt, oh, batch, seq, tk):
    t = nq.shape[0]
    nqb = seq // QB
    nc = seq // CMP_STRIDE
    wlen = WINDOW + QB
    gw = NSA_REP * HEAD_DIM
    qspec = pl.BlockSpec((QB, gw), lambda b, g, i: (b * nqb + i, g))
    cspec = pl.BlockSpec((1, nc, LANES), lambda b, g, i: (b, 0, 0))
    kvspec = pl.BlockSpec((seq, LANES), lambda b, g, i: (b, 0))
    gspec = pl.BlockSpec((QB, LANES), lambda b, g, i: (b * nqb + i, 0))
    return pl.pallas_call(
        functools.partial(_nsa_kernel, nc=nc, tk=tk, wlen=wlen, seq=seq),
        grid=(batch, NSA_KV_GROUPS, nqb),
        in_specs=[qspec, cspec, cspec, pl.BlockSpec((1, LANES, seq), lambda b, g, i: (b, 0, 0)),
                  kvspec, kvspec, kvspec, gspec,
                  pl.BlockSpec((nc, SEL_LANES), lambda b, g, i: (0, 0)),
                  pl.BlockSpec((SEL_LANES, seq), lambda b, g, i: (0, 0))],
        out_specs=qspec,
        out_shape=jax.ShapeDtypeStruct((t, NSA_QW), BF16),
        scratch_shapes=[pltpu.VMEM((SEL_LANES, QB), F32), pltpu.SMEM((1,), F32)],
        compiler_params=_params(("parallel", "parallel", "arbitrary")),
        name="nsa_attention",
    )(nq, kc, vc, jnp.swapaxes(ksl.reshape(batch, seq, LANES), 1, 2), vsl, kwn, vwn, gates, mt, oh)


N_CUM_PARTS = 3


def _odd_proj_kernel(x_ref, g_ref, w_ref, bf_ref, place_ref, q_ref, k_ref, v_ref, stats_ref, carry_ref,
                     *, tm, nsb):
    i = pl.program_id(0)
    hb = _rmsnorm(x_ref[...], g_ref[...]).astype(BF16)
    f = _dot(hb, w_ref[:, 3 * FOX_W:3 * FOX_W + LANES]) + bf_ref[...]
    log_f = jnp.minimum(f, 0.0) - jnp.log1p(jnp.exp(-jnp.abs(f)))
    r = lax.broadcasted_iota(jnp.int32, (tm, tm), 0)
    c = lax.broadcasted_iota(jnp.int32, (tm, tm), 1)
    tri = jnp.where(c <= r, 1.0, 0.0).astype(BF16)
    local = _dot(tri, jnp.concatenate(_split3(log_f), axis=1))
    local = local[:, 0:LANES] + local[:, LANES:2 * LANES] + local[:, 2 * LANES:3 * LANES]

    @pl.when(i % nsb == 0)
    def _():
        carry_ref[...] = jnp.zeros_like(carry_ref)

    cum = local + carry_ref[0:1, :]
    carry_ref[0:1, :] = cum[tm - 1:tm, :]
    lane = lax.broadcasted_iota(jnp.int32, (1, LANES), 1)
    neg_cum = cum * (-LOG2E)
    hi, mid, lo = (part.astype(F32) for part in _split3(neg_cum))
    head_of_col = jnp.where(lax.broadcasted_iota(jnp.int32, (FOX_W, LANES), 0) // HEAD_DIM
                            == lax.broadcasted_iota(jnp.int32, (FOX_W, LANES), 1), 1.0, 0.0).astype(BF16)
    packed = jnp.where(lane < FOX_HEADS, hi,
                       jnp.where(lane < 2 * FOX_HEADS, pltpu.roll(mid, FOX_HEADS, 1),
                                 pltpu.roll(lo, 2 * FOX_HEADS, 1))).astype(BF16)
    low = lane < HEAD_DIM
    q_one = jnp.where((lane >= HEAD_DIM) & (lane < HEAD_DIM + N_CUM_PARTS), 1.0, 0.0)
    v_one = jnp.where(lane >= HEAD_DIM, 1.0, 0.0)
    k_one = jnp.where((lane >= HEAD_DIM + N_CUM_PARTS) & (lane < HEAD_DIM + N_CUM_PARTS + N_SHIFT_PARTS), 1.0, 0.0)
    yq_all = _dot(hb, w_ref[:, 0:FOX_W]) * (QK_SCALE * LOG2E)
    yk_all = _dot(hb, w_ref[:, FOX_W:2 * FOX_W])
    yv_all = _dot(hb, w_ref[:, 2 * FOX_W:3 * FOX_W])
    extras_all = _dot(packed, place_ref[...])
    k_sq = jnp.max(_dot((yk_all * yk_all * (1.0 + 2.0 ** -7)).astype(BF16), head_of_col), axis=0, keepdims=True)
    for p in range(FOX_HEADS // 2):
        pair = slice(p * LANES, (p + 1) * LANES)
        yq, yk, yv = yq_all[:, pair], yk_all[:, pair], yv_all[:, pair]
        for h in range(2):
            hs = slice((2 * p + h) * LANES, (2 * p + h + 1) * LANES)
            head = (lambda y: y) if h == 0 else (lambda y: pltpu.roll(y, HEAD_DIM, 1))
            kh = jnp.where(low, head(yk), 0.0)
            q_ref[:, hs] = (jnp.where(low, head(yq), 0.0) + q_one).astype(BF16)
            k_ref[:, hs] = (kh + extras_all[:, hs] + k_one).astype(BF16)
            v_ref[:, hs] = (jnp.where(low, head(yv), 0.0) + v_one).astype(BF16)
    stats_ref[...] = jnp.zeros(stats_ref.shape, F32)
    stats_ref[0:1, :] = k_sq
    stats_ref[1:2, :] = neg_cum[0:1, :]


def _cum_placement():
    src = jnp.arange(LANES)[:, None]
    dst = jnp.arange(FOX_HEADS * LANES)[None, :]
    n, h = src // FOX_HEADS, src % FOX_HEADS
    return ((n < N_CUM_PARTS) & (dst == h * LANES + HEAD_DIM + n)).astype(BF16)


def _odd_proj(xf, g, w, bf, seq, tm):
    t, d = xf.shape
    nsb = seq // tm
    n = w.shape[1]
    wide = FOX_HEADS * LANES
    row = lambda width: pl.BlockSpec((tm, width), lambda i: (i, 0))
    return pl.pallas_call(
        functools.partial(_odd_proj_kernel, tm=tm, nsb=nsb),
        grid=(t // tm,),
        in_specs=[row(d), pl.BlockSpec((1, d), lambda i: (0, 0)),
                  pl.BlockSpec((d, n), lambda i: (0, 0)), pl.BlockSpec((1, LANES), lambda i: (0, 0)),
                  pl.BlockSpec((LANES, wide), lambda i: (0, 0))],
        out_specs=[row(wide)] * 3 + [pl.BlockSpec((8, LANES), lambda i: (i, 0))],
        out_shape=[jax.ShapeDtypeStruct((t, wide), BF16)] * 3 + [jax.ShapeDtypeStruct((t // tm * 8, LANES), F32)],
        scratch_shapes=[pltpu.VMEM((8, LANES), F32)],
        compiler_params=_params(("arbitrary",)),
        name="odd_proj",
    )(xf, g, w, bf, _cum_placement())


FOX_DEAD = -160.0
FOX_TIGHT = 64.0
N_SHIFT_PARTS = 2


def _fox_kernel(stats_ref, q_ref, k_ref, v_ref, o_ref, knorm_ref, ncum_ref, *, tq, tk, sub, dsub, seq, stats_rows):
    b, pair, i = pl.program_id(0), pl.program_id(1), pl.program_id(2)
    q0 = pl.multiple_of(i * tq, tq)
    n_full = q0 // tk
    nsub = tq // sub
    chains = [(h, r) for h in range(2) for r in range(nsub)]
    qs = [q_ref[r * sub:(r + 1) * sub, h * LANES:(h + 1) * LANES] for h, r in chains]
    lane = lax.broadcasted_iota(jnp.int32, (1, LANES), 1)
    feat = lane < HEAD_DIM
    n_stats, per = seq // stats_rows, tk // stats_rows

    def row_norms(x):
        xf = jnp.where(feat, x.astype(F32), 0.0)
        return jnp.sqrt(jnp.sum(xf * xf, axis=-1, keepdims=True))

    @pl.when(i == 0)
    def _():
        for h in range(2):
            def tile_stats(n, best, h=h):
                best = jnp.maximum(best, stats_ref[b * n_stats + n, 2 * pair + h])

                @pl.when(n % per == 0)
                def _():
                    ncum_ref[h, n // per] = stats_ref[b * n_stats + n, FOX_HEADS + 2 * pair + h]
                return best
            knorm_ref[h] = jnp.sqrt(lax.fori_loop(0, n_stats, tile_stats, jnp.float32(0.0)))

    k_norm = [knorm_ref[h] * (NORM_SLACK * (1.0 + 2.0 ** -8)) for h in range(2)]
    q_norm = [row_norms(q) for q in qs]
    qk_bound = [functools.reduce(jnp.maximum, [jnp.max(q_norm[h * nsub + r]) for r in range(nsub)]) * k_norm[h]
                for h in range(2)]
    tight = 2.0 * jnp.maximum(qk_bound[0], qk_bound[1]) <= FOX_TIGHT
    cum_q0 = [ncum_ref[h, n_full] for h in range(2)]

    def tiles(jt, h):
        k0 = pl.multiple_of(jt * tk, tk)
        hs = slice(h * LANES, (h + 1) * LANES)
        return k_ref[pl.ds(k0, tk), hs], v_ref[pl.ds(k0, tk), hs]

    shifts, accs, q_shifted = [], [], []
    for c, ((h, r), q) in enumerate(zip(chains, qs)):
        pieces = []
        for d in range(sub // dsub):
            row0 = r * sub + d * dsub
            nk = row0 + dsub
            kt = k_ref[pl.ds(q0, nk), h * LANES:(h + 1) * LANES]
            vt = v_ref[pl.ds(q0, nk), h * LANES:(h + 1) * LANES]
            qd = q[d * dsub:(d + 1) * dsub]
            s = _dot_nt(qd, kt)
            kpos = lax.broadcasted_iota(jnp.int32, (dsub, nk), 1)
            tpos = row0 + lax.broadcasted_iota(jnp.int32, (dsub, nk), 0)
            s = jnp.where(kpos <= tpos, s, -jnp.inf)
            m_diag = jnp.max(s, axis=-1, keepdims=True)
            bound = q_norm[c][d * dsub:(d + 1) * dsub] * k_norm[h] + cum_q0[h]
            shift = jnp.where(tight, jnp.maximum(m_diag, bound), m_diag)
            hi = shift.astype(BF16)
            rest = shift - hi.astype(F32)
            lo = (rest + jnp.abs(rest) * (2.0 ** -7)).astype(BF16)
            shift = hi.astype(F32) + lo.astype(F32)
            acc = _dot(jnp.exp2(s - shift).astype(BF16), vt)
            qsh = jnp.where(lane == HEAD_DIM + N_CUM_PARTS, -hi,
                            jnp.where(lane == HEAD_DIM + N_CUM_PARTS + 1, -lo, qd))
            pieces.append((shift, acc, qsh))
        shifts.append(jnp.concatenate([p[0] for p in pieces], axis=0))
        accs.append(jnp.concatenate([p[1] for p in pieces], axis=0))
        q_shifted.append(jnp.concatenate([p[2] for p in pieces], axis=0))

    def may_matter(jt):
        return jnp.maximum(ncum_ref[0, jt + 1] - cum_q0[0], ncum_ref[1, jt + 1] - cum_q0[1]) >= FOX_DEAD

    def fixed_shift_sweep(accs):
        def body(carry):
            jt, accs = carry
            out = []
            for (h, r), q, acc in zip(chains, q_shifted, accs):
                kt, vt = tiles(jt, h)
                out.append(acc + _dot(jnp.exp2(_dot_nt(q, kt)).astype(BF16), vt))
            return jt - 1, tuple(out)

        return lax.while_loop(lambda cr: (cr[0] >= 0) & may_matter(jnp.maximum(cr[0], 0)), body,
                              (n_full - 1, accs))[1]

    def running_max_sweep(accs):
        def alive(jt, ms):
            go = None
            for h in range(2):
                m_min = functools.reduce(jnp.minimum, [jnp.min(ms[h * nsub + r]) for r in range(nsub)])
                live = qk_bound[h] + ncum_ref[h, jt + 1] - m_min >= FOX_DEAD
                go = live if go is None else (go | live)
            return go

        def body(carry):
            jt, _, ms, accs = carry
            new_m, new_acc = [], []
            for (h, r), q, m, acc in zip(chains, qs, ms, accs):
                kt, vt = tiles(jt, h)
                s = _dot_nt(q, kt)
                m_new = jnp.maximum(m, jnp.max(s, axis=-1, keepdims=True))
                new_acc.append(jnp.exp2(m - m_new) * acc + _dot(jnp.exp2(s - m_new).astype(BF16), vt))
                new_m.append(m_new)
            new_m = tuple(new_m)
            return jt - 1, alive(jnp.maximum(jt - 1, 0), new_m), new_m, tuple(new_acc)

        ms = tuple(shifts)
        return lax.while_loop(lambda cr: (cr[0] >= 0) & cr[1], body,
                              (n_full - 1, alive(jnp.maximum(n_full - 1, 0), ms), ms, accs))[3]

    accs = lax.cond(tight, fixed_shift_sweep, running_max_sweep, tuple(accs))
    heads = []
    for h in range(2):
        acc = jnp.concatenate([accs[h * nsub + r] for r in range(nsub)], axis=0)
        heads.append(acc * (1.0 / acc[:, HEAD_DIM:HEAD_DIM + 1]))
    o_ref[...] = jnp.where(feat, heads[0], pltpu.roll(heads[1], HEAD_DIM, 1)).astype(BF16)


def _fox_attention(q, k, v, stats, batch, seq, tq, tk, sub, dsub):
    t = q.shape[0]
    nq = seq // tq
    kv = pl.BlockSpec((seq, 2 * LANES), lambda b, p, i: (b, p))
    stats_rows = 8 * t // stats.shape[0]
    assert tk % stats_rows == 0
    tile_stats = stats.reshape(t // stats_rows, 8, LANES)
    tile_stats = jnp.concatenate([tile_stats[:, 0, :FOX_HEADS], tile_stats[:, 1, :FOX_HEADS]], axis=1)
    return pl.pallas_call(
        functools.partial(_fox_kernel, tq=tq, tk=tk, sub=sub, dsub=dsub, seq=seq, stats_rows=stats_rows),
        grid=(batch, FOX_HEADS // 2, nq),
        in_specs=[pl.BlockSpec(memory_space=pltpu.SMEM),
                  pl.BlockSpec((tq, 2 * LANES), lambda b, p, i: (b * nq + i, p)), kv, kv],
        out_specs=pl.BlockSpec((tq, LANES), lambda b, p, i: (b * nq + i, p)),
        out_shape=jax.ShapeDtypeStruct((t, FOX_W), BF16),
        scratch_shapes=[pltpu.SMEM((2,), F32), pltpu.SMEM((2, seq // tk), F32)],
        compiler_params=_params(("parallel", "parallel", "arbitrary")),
        name="fox_attention",
    )(tile_stats, q, k, v)


FFN_CHUNKS = (2816,)
assert sum(FFN_CHUNKS) == D_FF
HALO = 8


def _ffn_kernel(*refs, tm, nsb, final, n_attn):
    attn_refs, wattn_ref, refs = refs[:n_attn], refs[n_attn], refs[n_attn + 1:]
    if final:
        x_ref, g_ref, win_ref, cw_ref, cb_ref, wout_ref, fn_ref, o_ref, a_scr, halo_scr = refs
    else:
        x_ref, g_ref, win_ref, cw_ref, cb_ref, wout_ref, o_ref, a_scr, halo_scr = refs
    i = pl.program_id(0)

    @pl.when(i % nsb == 0)
    def _():
        halo_scr[...] = jnp.zeros((HALO, D_FF), F32)

    x = x_ref[...]
    off = 0
    for a_ref in attn_refs:
        width = a_ref.shape[1]
        x = x + _dot(a_ref[...], wattn_ref[off:off + width, :])
        off += width
    hb = _rmsnorm(x, g_ref[...]).astype(BF16)
    acc = jnp.zeros((tm, D_MODEL), F32)
    start = 0
    for width in FFN_CHUNKS:
        sl = slice(start, start + width)
        a = _dot(hb, win_ref[:, sl])
        b = _dot(hb, win_ref[:, D_FF + start:D_FF + start + width])
        start += width
        a_scr[0:HALO, 0:width] = halo_scr[:, sl]
        a_scr[HALO:HALO + tm, 0:width] = a
        halo_scr[:, sl] = a[tm - HALO:tm, :]
        conv = (cw_ref[0:1, sl] * a_scr[HALO - 2:HALO - 2 + tm, 0:width]
                + cw_ref[1:2, sl] * a_scr[HALO - 1:HALO - 1 + tm, 0:width]
                + cw_ref[2:3, sl] * a + cb_ref[:, sl])
        gated = conv * (1.0 / (1.0 + jnp.exp(-conv))) * b
        acc = acc + _dot(gated.astype(BF16), wout_ref[sl, :])
    y = x + acc
    if final:
        y = _rmsnorm(y, fn_ref[...])
    o_ref[...] = y


def _mixer_out_ffn(xf, attn_outs, w_attn, g, w_in, conv_w, conv_b, w_out, final_norm, seq, tm):
    t, d = xf.shape
    nsb = seq // tm
    row = pl.BlockSpec((tm, d), lambda i: (i, 0))
    const = lambda shape: pl.BlockSpec(shape, lambda i: (0, 0), pipeline_mode=pl.Buffered(1))
    small = lambda shape: pl.BlockSpec(shape, lambda i: (0, 0))
    final = final_norm is not None
    in_specs = [pl.BlockSpec((tm, a.shape[1]), lambda i: (i, 0)) for a in attn_outs] + [const(w_attn.shape)]
    in_specs += [row, small((1, d)), const(w_in.shape), small(conv_w.shape), small((1, D_FF)), const(w_out.shape)]
    args = list(attn_outs) + [w_attn, xf, g, w_in, conv_w, conv_b, w_out]
    if final:
        in_specs.append(small((1, d)))
        args.append(final_norm)
    return pl.pallas_call(
        functools.partial(_ffn_kernel, tm=tm, nsb=nsb, final=final, n_attn=len(attn_outs)),
        grid=(t // tm,),
        in_specs=in_specs,
        out_specs=row,
        out_shape=jax.ShapeDtypeStruct((t, d), F32),
        scratch_shapes=[pltpu.VMEM((tm + HALO, max(FFN_CHUNKS)), F32), pltpu.VMEM((HALO, D_FF), F32)],
        compiler_params=_params(("arbitrary",)),
        name="conv_glu_ffn",
    )(*args)


def kernel(x, attn_norm, ffn_norm, ev_w_in, ev_cmp_pos_k, ev_cmp_pos_v, ev_cmp_w_k, ev_cmp_w_v, ev_w_out,
           od_w_in, od_b_f, od_w_out, ffn_w_in, ffn_conv_w, ffn_conv_b, ffn_w_out, final_norm):
    batch, seq, d = x.shape
    t = batch * seq
    depth = attn_norm.shape[0]
    tm, ffn_tm = min(512, seq), min(512, seq)
    sb_tile, sb_nsub = min(256, seq), 2
    fox_tq, fox_tk, fox_sub, fox_dsub = min(512, seq), min(512, seq), 512, 512
    sel_tk = min(1024, seq)

    xf = x.reshape(t, d)
    tabs = _rope_tables(jnp.arange(seq))
    ctabs = _rope_tables(jnp.arange(seq // CMP_STRIDE) * CMP_STRIDE + (CMP_LEN - 1))
    mt, oh = _nsa_consts(seq)

    for layer in range(depth):
        g_attn = attn_norm[layer].reshape(1, d)
        if layer % 2 == 0:
            e = layer // 2
            (sbq, sbk, sbv, nq, kc_raw, vc_raw, ksl, vsl, kwn, vwn, gates) = _even_proj(
                xf, g_attn, _prep_even_w(ev_w_in[e]), tabs, seq, tm)
            pek, wk = _prep_cmp(ev_cmp_pos_k[e], ev_cmp_w_k[e])
            pev, wv = _prep_cmp(ev_cmp_pos_v[e], ev_cmp_w_v[e])
            kc, vc = _compress(kc_raw, vc_raw, pek, pev, wk, wv, ctabs, batch, seq)
            o_sb = _sb_attention(sbq, sbk, sbv, batch, seq, sb_tile, sb_nsub)
            o_nsa = _nsa_attention(nq, kc, vc, ksl, vsl, kwn, vwn, gates, mt, oh, batch, seq, sel_tk)
            attn_outs, w_attn = [o_sb, o_nsa], ev_w_out[e].astype(BF16)
        else:
            o = layer // 2
            w = jnp.pad(od_w_in[o], ((0, 0), (0, LANES - FOX_HEADS))).astype(BF16)
            bf = jnp.pad(od_b_f[o], (0, LANES - FOX_HEADS)).reshape(1, LANES)
            q, k, v, stats = _odd_proj(xf, g_attn, w, bf, seq, tm)
            o_fox = _fox_attention(q, k, v, stats, batch, seq, fox_tq, fox_tk, fox_sub, fox_dsub)
            attn_outs, w_attn = [o_fox], od_w_out[o].astype(BF16)
        last = layer == depth - 1
        xf = _mixer_out_ffn(xf, attn_outs, w_attn, ffn_norm[layer].reshape(1, d), ffn_w_in[layer].astype(BF16),
                            ffn_conv_w[layer], ffn_conv_b[layer].reshape(1, D_FF), ffn_w_out[layer].astype(BF16),
                            final_norm.reshape(1, d) if last else None, seq, ffn_tm)
    return xf.reshape(batch, seq, d)
```

```python
import functools
import math

import jax
import jax.numpy as jnp
from jax import lax
from jax.experimental import pallas as pl
from jax.experimental.pallas import tpu as pltpu

F32, BF16 = jnp.float32, jnp.bfloat16

D_MODEL = 1024
HEAD_DIM = 64
N_HEADS = D_MODEL // HEAD_DIM
SB_HEADS = N_HEADS // 2
NSA_HEADS = N_HEADS - SB_HEADS
NSA_KV_GROUPS = 2
NSA_REP = NSA_HEADS // NSA_KV_GROUPS
FOX_HEADS = N_HEADS
CMP_LEN = 32
CMP_STRIDE = 16
SEL_LEN = 64
SEL_TOP = 16
WINDOW = 512
N_BRANCH = 3
ROPE_THETA = 500000.0
ROT_DIM = HEAD_DIM // 4
D_FF = 2816
CONV_WIDTH = 3
NORM_EPS = 1e-6
SB_W = SB_HEADS * HEAD_DIM
NSA_QW = NSA_HEADS * HEAD_DIM
NSA_KVW = NSA_KV_GROUPS * HEAD_DIM
FOX_W = FOX_HEADS * HEAD_DIM

LANES = 128
SEL_LANES = 128
QK_SCALE = HEAD_DIM ** -0.5
LOG2E = math.log2(math.e)
SEL_NEG = -(2.0 ** 30)
NSA_TIGHT = 86.0
NORM_SLACK = 1.001
VMEM_LIMIT = 56 * 2 ** 20

_NT = (((1,), (1,)), ((), ()))


def _params(sem):
    return pltpu.CompilerParams(dimension_semantics=sem, vmem_limit_bytes=VMEM_LIMIT)


def _dot(a, b):
    return jnp.dot(a, b, preferred_element_type=F32)


def _dot_nt(a, b):
    return lax.dot_general(a, b, _NT, preferred_element_type=F32)


def _rmsnorm(x, g):
    ms = jnp.mean(x * x, axis=-1, keepdims=True)
    return (x * lax.rsqrt(ms + NORM_EPS)) * g


def _rope(y, c, s1, s2):
    return y * c + pltpu.roll(y, LANES - ROT_DIM // 2, 1) * s1 + pltpu.roll(y, ROT_DIM // 2, 1) * s2


def _split3(x):
    hi = x.astype(BF16)
    r1 = x - hi.astype(F32)
    mid = r1.astype(BF16)
    lo = (r1 - mid.astype(F32)).astype(BF16)
    return hi, mid, lo


def _rope_tables(pos):
    half = ROT_DIM // 2
    inv_freq = ROPE_THETA ** (-(jnp.arange(half, dtype=F32) * 2.0 / ROT_DIM))
    ang = pos.astype(F32)[:, None] * inv_freq[None, :]
    cos, sin = jnp.cos(ang), jnp.sin(ang)
    n = pos.shape[0]
    one = jnp.ones((n, HEAD_DIM - ROT_DIM), F32)
    zero = jnp.zeros((n, HEAD_DIM - ROT_DIM), F32)
    z8 = jnp.zeros((n, half), F32)
    c = jnp.concatenate([cos, cos, one], -1)
    s1 = jnp.concatenate([-sin, z8, zero], -1)
    s2 = jnp.concatenate([z8, sin, zero], -1)
    two = lambda t: jnp.concatenate([t, t], -1)
    return two(c), two(s1), two(s2)


def _even_proj_kernel(x_ref, g_ref, w_ref, c_ref, s1_ref, s2_ref,
                      sbq_ref, sbk_ref, sbv_ref, nq_ref, kc_ref, vc_ref,
                      ksl_ref, vsl_ref, kwn_ref, vwn_ref, gate_ref):
    hb = _rmsnorm(x_ref[...], g_ref[...]).astype(BF16)
    c, s1, s2 = c_ref[...], s1_ref[...], s2_ref[...]

    sb = _dot(hb, w_ref[:, 0:3 * SB_W])
    sbq_ref[...] = (sb[:, 0:SB_W] * QK_SCALE).astype(BF16)
    sbk_ref[...] = sb[:, SB_W:2 * SB_W].astype(BF16)
    sbv_ref[...] = sb[:, 2 * SB_W:3 * SB_W].astype(BF16)
    nsa = _dot(hb, w_ref[:, 3 * SB_W:])
    piece = lambda j: nsa[:, j * LANES:(j + 1) * LANES]
    n_q = NSA_QW // LANES
    for j in range(n_q):
        nq_ref[:, j * LANES:(j + 1) * LANES] = (_rope(piece(j), c, s1, s2) * (QK_SCALE * LOG2E)).astype(BF16)
    kc_ref[...] = piece(n_q)
    vc_ref[...] = piece(n_q + 1)
    ksl_ref[...] = _rope(piece(n_q + 2), c, s1, s2).astype(BF16)
    vsl_ref[...] = piece(n_q + 3).astype(BF16)
    kwn_ref[...] = _rope(piece(n_q + 4), c, s1, s2).astype(BF16)
    vwn_ref[...] = piece(n_q + 5).astype(BF16)
    gate_ref[...] = 1.0 / (1.0 + jnp.exp(-nsa[:, (n_q + 6) * LANES:(n_q + 8) * LANES]))


def _even_proj(xf, g, w, tabs, seq, tm):
    t, d = xf.shape
    nsb = seq // tm
    n = w.shape[1]
    row = lambda width: pl.BlockSpec((tm, width), lambda i: (i, 0))
    tab = pl.BlockSpec((tm, LANES), lambda i: (i % nsb, 0))
    out_shape = (
        [jax.ShapeDtypeStruct((t, SB_W), BF16)] * 3
        + [jax.ShapeDtypeStruct((t, NSA_QW), BF16)]
        + [jax.ShapeDtypeStruct((t, LANES), F32)] * 2
        + [jax.ShapeDtypeStruct((t, LANES), BF16)] * 4
        + [jax.ShapeDtypeStruct((t, 2 * LANES), F32)]
    )
    out_specs = [row(SB_W)] * 3 + [row(NSA_QW)] + [row(LANES)] * 6 + [row(2 * LANES)]
    return pl.pallas_call(
        _even_proj_kernel,
        grid=(t // tm,),
        in_specs=[row(d), pl.BlockSpec((1, d), lambda i: (0, 0)),
                  pl.BlockSpec((d, n), lambda i: (0, 0)), tab, tab, tab],
        out_specs=out_specs,
        out_shape=out_shape,
        compiler_params=_params(("parallel",)),
        name="even_proj",
    )(xf, g, w, *tabs)


def _prep_even_w(w):
    d = w.shape[0]
    main = w[:, :3 * SB_W + NSA_QW + 6 * NSA_KVW]
    gates = w[:, 3 * SB_W + NSA_QW + 6 * NSA_KVW:].reshape(d, NSA_KV_GROUPS, NSA_REP, N_BRANCH)
    gates = jnp.transpose(gates, (0, 1, 3, 2)).reshape(d, NSA_KV_GROUPS, N_BRANCH * NSA_REP)
    gates = jnp.pad(gates, ((0, 0), (0, 0), (0, LANES - N_BRANCH * NSA_REP)))
    return jnp.concatenate([main, gates.reshape(d, NSA_KV_GROUPS * LANES)], axis=1).astype(BF16)


SB_DEAD = -105.0


def _sb_kernel(q_ref, k_ref, v_ref, o_ref, *, tile, nsub):
    i = pl.program_id(2)
    lane = lax.broadcasted_iota(jnp.int32, (1, LANES), 1)
    r = lax.broadcasted_iota(jnp.int32, (tile, tile), 0)
    c = lax.broadcasted_iota(jnp.int32, (tile, tile), 1)
    later = jnp.where(r > c, 1.0, 0.0).astype(BF16)
    diag = c < r
    chains = [(h, sub) for h in range(2) for sub in range(nsub)]
    qs = []
    for h, sub in chains:
        q = q_ref[sub * tile:(sub + 1) * tile, :]
        qs.append(jnp.where((lane < HEAD_DIM) if h == 0 else (lane >= HEAD_DIM), q, jnp.zeros_like(q)))

    def step(jt, state, mode):
        k0 = pl.multiple_of(jt * tile, tile)
        kt = k_ref[pl.ds(k0, tile), :]
        vt = v_ref[pl.ds(k0, tile), :]
        out = []
        for c, (acc, keep_sum) in enumerate(state):
            if mode[c] is None:
                out.append((acc, keep_sum))
                continue
            z = _dot_nt(qs[c], kt)
            ls = jnp.minimum(z, 0.0) - jnp.log(1.0 + jnp.exp(-jnp.abs(z)))
            lk = ls - z
            if mode[c]:
                lk = jnp.where(diag, lk, 0.0)
            hi = lk.astype(BF16)
            lo = (lk - hi.astype(F32)).astype(BF16)
            both = _dot(jnp.concatenate([hi, lo], axis=0), later)
            after = both[0:tile] + both[tile:2 * tile]
            a = jnp.exp(ls + after + keep_sum)
            if mode[c]:
                a = jnp.where(diag, a, 0.0)
            out.append((acc + _dot(a.astype(BF16), vt), keep_sum + after[:, 0:1] + lk[:, 0:1]))
        return tuple(out)

    def alive(state):
        return functools.reduce(jnp.maximum, [jnp.max(keep_sum) for _, keep_sum in state]) >= SB_DEAD

    state = tuple((jnp.zeros((tile, LANES), F32), jnp.zeros((tile, 1), F32)) for _ in chains)
    for top in range(nsub - 1, -1, -1):
        mode = [None if sub < top else sub == top for _, sub in chains]
        state = step(nsub * i + top, state, mode)

    def body(carry):
        jt, _, state = carry
        state = step(jt, state, [False] * len(chains))
        return jt - 1, alive(state), state

    _, _, state = lax.while_loop(lambda cr: (cr[0] >= 0) & cr[1], body, (nsub * i - 1, alive(state), state))
    heads = [jnp.concatenate([state[h * nsub + sub][0] for sub in range(nsub)], axis=0) for h in range(2)]
    o_ref[...] = jnp.where(lane < HEAD_DIM, heads[0], heads[1]).astype(BF16)


def _sb_attention(q, k, v, batch, seq, tile, nsub):
    t, w = q.shape
    nq = seq // (tile * nsub)
    kv = pl.BlockSpec((seq, LANES), lambda b, p, i: (b, p))
    qo = pl.BlockSpec((tile * nsub, LANES), lambda b, p, i: (b * nq + i, p))
    return pl.pallas_call(
        functools.partial(_sb_kernel, tile=tile, nsub=nsub),
        grid=(batch, w // LANES, nq),
        in_specs=[qo, kv, kv],
        out_specs=qo,
        out_shape=jax.ShapeDtypeStruct((t, w), BF16),
        compiler_params=_params(("parallel", "parallel", "arbitrary")),
        name="sb_attention",
    )(q, k, v)


def _compress_kernel(xk_ref, xv_ref, pek_ref, pev_ref, wk_ref, wv_ref, c_ref, s1_ref, s2_ref,
                     kc_ref, vc_ref, *, nc):
    def comp(x_ref, pe_ref, w_ref):
        x = x_ref[0]
        top = _dot((x + pe_ref[0:1, :]).astype(BF16), w_ref[0])
        bot = _dot((x + pe_ref[1:2, :]).astype(BF16), w_ref[1])
        return top + pltpu.roll(bot, nc - 1, 0)

    kc = _rope(comp(xk_ref, pek_ref, wk_ref), c_ref[...], s1_ref[...], s2_ref[...])
    kc_ref[0] = kc.astype(BF16)
    vc_ref[0] = comp(xv_ref, pev_ref, wv_ref).astype(BF16)


def _prep_cmp(pe, w):
    half = CMP_LEN // 2
    w3 = w.reshape(CMP_LEN, HEAD_DIM, HEAD_DIM)
    eye = jnp.eye(NSA_KV_GROUPS, dtype=w.dtype)
    parts, pes = [], []
    for s in range(2):
        wh = jnp.einsum('ldo,gh->lgdho', w3[s * half:(s + 1) * half], eye)
        parts.append(wh.reshape(half * NSA_KVW, NSA_KVW))
        pes.append(jnp.broadcast_to(pe[s * half:(s + 1) * half, None, :],
                                    (half, NSA_KV_GROUPS, HEAD_DIM)).reshape(1, half * NSA_KVW))
    return jnp.concatenate(pes, 0), jnp.stack(parts).astype(BF16)


def _compress(kc_raw, vc_raw, pek, pev, wk, wv, ctabs, batch, seq):
    nc = seq // CMP_STRIDE
    cw = CMP_STRIDE * NSA_KVW
    xk = kc_raw.reshape(batch, nc, cw)
    xv = vc_raw.reshape(batch, nc, cw)
    xs = pl.BlockSpec((1, nc, cw), lambda b: (b, 0, 0))
    pes = pl.BlockSpec((2, cw), lambda b: (0, 0))
    ws = pl.BlockSpec((2, cw, NSA_KVW), lambda b: (0, 0, 0))
    tab = pl.BlockSpec((nc, LANES), lambda b: (0, 0))
    out = pl.BlockSpec((1, nc, LANES), lambda b: (b, 0, 0))
    return pl.pallas_call(
        functools.partial(_compress_kernel, nc=nc),
        grid=(batch,),
        in_specs=[xs, xs, pes, pes, ws, ws, tab, tab, tab],
        out_specs=[out, out],
        out_shape=[jax.ShapeDtypeStruct((batch, nc, LANES), BF16)] * 2,
        compiler_params=_params(("parallel",)),
        name="nsa_compress",
    )(xk, xv, pek, pev, wk, wv, *ctabs)


QB = 128


def _softmax_rows(s):
    m = jnp.max(s, axis=-1, keepdims=True)
    m = jnp.where(m > -jnp.inf, m, 0.0)
    e = jnp.exp2(s - m)
    d = jnp.sum(e, axis=-1, keepdims=True)
    return e / jnp.where(d > 0, d, 1.0)


def _nsa_kernel(q_ref, kc_ref, vc_ref, ks_ref, vs_ref, kw_ref, vw_ref, g_ref, mt_ref, oh_ref,
                o_ref, score_ref, knorm_ref, *, nc, tk, wlen, seq):
    g = pl.program_id(1)
    i = pl.program_id(2)
    q0 = i * QB
    rows = NSA_REP * QB

    q2 = q_ref[...]
    pr = lax.broadcasted_iota(jnp.int32, (NSA_REP * HEAD_DIM, LANES), 0)
    pc = lax.broadcasted_iota(jnp.int32, (NSA_REP * HEAD_DIM, LANES), 1) - g * HEAD_DIM
    in_group = (pc >= 0) & (pc < HEAD_DIM)
    q4 = jnp.concatenate(
        [_dot(q2, jnp.where(in_group & (pr - hh * HEAD_DIM == pc), 1.0, 0.0).astype(BF16)).astype(BF16)
         for hh in range(NSA_REP)], axis=0)

    sc = _dot_nt(q4, kc_ref[0])
    cend = lax.broadcasted_iota(jnp.int32, (rows, nc), 1) * CMP_STRIDE + (CMP_LEN - 1)
    tpos_c = q0 + (lax.broadcasted_iota(jnp.int32, (rows, nc), 0) & (QB - 1))
    p_cmp = _softmax_rows(jnp.where(cend <= tpos_c, sc, -jnp.inf))
    o_cmp = _dot(p_cmp.astype(BF16), vc_ref[0])

    p_sum = p_cmp[0:QB] + p_cmp[QB:2 * QB] + p_cmp[2 * QB:3 * QB] + p_cmp[3 * QB:4 * QB]
    overlap = mt_ref[...]
    parts = _dot(jnp.concatenate(_split3(p_sum), axis=0), overlap)
    imp_t = jnp.transpose(parts[0:QB] + parts[QB:2 * QB] + parts[2 * QB:3 * QB])
    nidx = lax.broadcasted_iota(jnp.int32, (SEL_LANES, QB), 0)
    cur = (q0 + lax.broadcasted_iota(jnp.int32, (SEL_LANES, QB), 1)) // SEL_LEN
    causal = nidx <= cur
    forced = (nidx == 0) | (nidx == cur) | (nidx == cur - 1)
    score = jnp.where(causal, jnp.where(forced, jnp.inf, imp_t), -jnp.inf)
    score_ref[...] = score

    def larger_step(m, cnt):
        return cnt + jnp.where(score_ref[pl.ds(m, 1), :] > score, 1.0, 0.0)

    def larger_pair(m2, cnt):
        return larger_step(2 * m2 + 1, larger_step(2 * m2, cnt))

    def rank_step(m, cnt):
        rowm = score_ref[pl.ds(m, 1), :]
        ge = jnp.where(rowm >= score, 1.0, 0.0)
        gt = jnp.where(rowm > score, 1.0, 0.0)
        return cnt + jnp.where(nidx > m, ge, gt)

    n_causal = (q0 + QB) // SEL_LEN
    zero_cnt = jnp.zeros((SEL_LANES, QB), F32)
    cnt = lax.fori_loop(0, n_causal // 2, larger_pair, zero_cnt)
    n_top = jnp.sum(jnp.where(causal & (cnt < float(SEL_TOP)), 1.0, 0.0), axis=0, keepdims=True)
    cnt = lax.cond(jnp.max(n_top) > float(SEL_TOP),
                   lambda c: lax.fori_loop(0, n_causal, rank_step, zero_cnt), lambda c: c, cnt)
    bias_t = jnp.where(causal & (cnt < float(SEL_TOP)), 0.0, SEL_NEG)
    selb4 = jnp.concatenate([jnp.transpose(bias_t)] * NSA_REP, axis=0)

    lane = lax.broadcasted_iota(jnp.int32, (1, LANES), 1)
    own_lanes = (lane >= g * HEAD_DIM) & (lane < (g + 1) * HEAD_DIM)

    @pl.when(i == 0)
    def _():
        feat = lax.broadcasted_iota(jnp.int32, (LANES, 1), 0)
        own_rows = (feat >= g * HEAD_DIM) & (feat < (g + 1) * HEAD_DIM)

        def chunk(n, best):
            kf = ks_ref[0, :, pl.ds(pl.multiple_of(n * tk, tk), tk)].astype(F32)
            kf = jnp.where(own_rows, kf, 0.0)
            return jnp.maximum(best, jnp.max(jnp.sum(kf * kf, axis=0, keepdims=True)))
        knorm_ref[0] = jnp.sqrt(lax.fori_loop(0, seq // tk, chunk, jnp.float32(0.0)))

    q4f = q4.astype(F32)
    shift = jnp.sqrt(jnp.sum(q4f * q4f, axis=-1, keepdims=True)) * (knorm_ref[0] * NORM_SLACK)
    shift = (shift * (1.0 + 2.0 ** -7)).astype(BF16).astype(F32)
    tight = 2.0 * jnp.max(shift) <= NSA_TIGHT
    qs = jnp.concatenate([q4, jnp.where(tight, selb4 - shift, selb4).astype(BF16)], axis=1)
    half = tk // 2
    diag0 = pl.multiple_of((q0 // half) * half, half)
    n_big = diag0 // tk
    odd_half = diag0 - n_big * tk > 0
    half0 = pl.multiple_of(n_big * tk, tk)

    def scores(k0, size, masked):
        kk = jnp.concatenate([ks_ref[0, :, pl.ds(k0, size)], oh_ref[:, pl.ds(k0, size)]], axis=0)
        vt = vs_ref[pl.ds(k0, size), :]
        vt = jnp.where(own_lanes, vt, jnp.ones_like(vt))
        s = _dot(qs, kk)
        if masked:
            kpos = k0 + lax.broadcasted_iota(jnp.int32, (rows, size), 1)
            tpos = q0 + (lax.broadcasted_iota(jnp.int32, (rows, size), 0) & (QB - 1))
            s = jnp.where(kpos <= tpos, s, -jnp.inf)
        return s, vt

    def sweep(step, carry):
        carry = lax.fori_loop(0, n_big, lambda j, cr: step(pl.multiple_of(j * tk, tk), tk, False, cr), carry)
        carry = lax.cond(odd_half, lambda cr: step(half0, half, False, cr), lambda cr: cr, carry)
        return step(diag0, half, True, carry)

    def fixed_shift_sweep(acc):
        def step(k0, size, masked, acc):
            s, vt = scores(k0, size, masked)
            return acc + _dot(jnp.exp2(s).astype(BF16), vt)
        return sweep(step, acc)

    def running_max_sweep(acc):
        def step(k0, size, masked, carry):
            m, acc = carry
            s, vt = scores(k0, size, masked)
            m_new = jnp.maximum(m, jnp.max(s, axis=-1, keepdims=True))
            return m_new, jnp.exp2(m - m_new) * acc + _dot(jnp.exp2(s - m_new).astype(BF16), vt)
        return sweep(step, (jnp.full((rows, 1), -jnp.inf, F32), acc))[1]

    acc_s = lax.cond(tight, fixed_shift_sweep, running_max_sweep, jnp.zeros((rows, LANES), F32))
    l_s = jnp.where(g == 0, acc_s[:, HEAD_DIM:HEAD_DIM + 1], acc_s[:, 0:1])
    o_slc = acc_s / l_s

    w0 = pl.multiple_of(jnp.maximum(q0 - WINDOW, 0), QB)
    sw = _dot_nt(q4, kw_ref[pl.ds(w0, wlen), :])
    kpos = w0 + lax.broadcasted_iota(jnp.int32, (rows, wlen), 1)
    tpos = q0 + (lax.broadcasted_iota(jnp.int32, (rows, wlen), 0) & (QB - 1))
    p_win = _softmax_rows(jnp.where((kpos <= tpos) & (kpos > tpos - WINDOW), sw, -jnp.inf))
    o_win = _dot(p_win.astype(BF16), vw_ref[pl.ds(w0, wlen), :])

    gates = g_ref[...]
    orow = lax.broadcasted_iota(jnp.int32, (LANES, NSA_REP * HEAD_DIM), 0) - g * HEAD_DIM
    ocol = lax.broadcasted_iota(jnp.int32, (LANES, NSA_REP * HEAD_DIM), 1)
    o_group = (orow >= 0) & (orow < HEAD_DIM)
    out = jnp.zeros((QB, NSA_REP * HEAD_DIM), F32)
    for hh in range(NSA_REP):
        sl = slice(hh * QB, (hh + 1) * QB)
        gate = lambda br: gates[:, br * NSA_REP + hh:br * NSA_REP + hh + 1]
        o_h = gate(0) * o_cmp[sl] + gate(1) * o_slc[sl] + gate(2) * o_win[sl]
        place = jnp.where(o_group & (ocol - hh * HEAD_DIM == orow), 1.0, 0.0).astype(BF16)
        out = out + _dot(o_h.astype(BF16), place)
    o_ref[...] = out.astype(BF16)


def _nsa_consts(seq):
    nc = seq // CMP_STRIDE
    c_start = jnp.arange(nc) * CMP_STRIDE
    sel_start = jnp.arange(SEL_LANES) * SEL_LEN
    real = (jnp.arange(SEL_LANES) < seq // SEL_LEN)[:, None] & (jnp.arange(nc) < (seq - CMP_LEN) // CMP_STRIDE + 1)[None, :]
    overlap = (c_start[None, :] < sel_start[:, None] + SEL_LEN) & (c_start[None, :] + CMP_LEN > sel_start[:, None])
    mt = jnp.transpose(overlap & real).astype(BF16)
    oh = (jnp.arange(seq)[None, :] // SEL_LEN == jnp.arange(SEL_LANES)[:, None]).astype(BF16)
    return mt, oh


def _nsa_attention(nq, kc, vc, ksl, vsl, kwn, vwn, gates, mt, oh, batch, seq, tk):
    t = nq.shape[0]
    nqb = seq // QB
    nc = seq // CMP_STRIDE
    wlen = WINDOW + QB
    gw = NSA_REP * HEAD_DIM
    qspec = pl.BlockSpec((QB, gw), lambda b, g, i: (b * nqb + i, g))
    cspec = pl.BlockSpec((1, nc, LANES), lambda b, g, i: (b, 0, 0))
    kvspec = pl.BlockSpec((seq, LANES), lambda b, g, i: (b, 0))
    gspec = pl.BlockSpec((QB, LANES), lambda b, g, i: (b * nqb + i, g))
    return pl.pallas_call(
        functools.partial(_nsa_kernel, nc=nc, tk=tk, wlen=wlen, seq=seq),
        grid=(batch, NSA_KV_GROUPS, nqb),
        in_specs=[qspec, cspec, cspec, pl.BlockSpec((1, LANES, seq), lambda b, g, i: (b, 0, 0)),
                  kvspec, kvspec, kvspec, gspec,
                  pl.BlockSpec((nc, SEL_LANES), lambda b, g, i: (0, 0)),
                  pl.BlockSpec((SEL_LANES, seq), lambda b, g, i: (0, 0))],
        out_specs=qspec,
        out_shape=jax.ShapeDtypeStruct((t, NSA_QW), BF16),
        scratch_shapes=[pltpu.VMEM((SEL_LANES, QB), F32), pltpu.SMEM((1,), F32)],
        compiler_params=_params(("parallel", "parallel", "arbitrary")),
        name="nsa_attention",
    )(nq, kc, vc, jnp.swapaxes(ksl.reshape(batch, seq, LANES), 1, 2), vsl, kwn, vwn, gates, mt, oh)


N_CUM_PARTS = 3


def _odd_proj_kernel(x_ref, g_ref, w_ref, bf_ref, place_ref, q_ref, k_ref, v_ref, stats_ref, carry_ref,
                     *, tm, nsb):
    i = pl.program_id(0)
    hb = _rmsnorm(x_ref[...], g_ref[...]).astype(BF16)
    f = _dot(hb, w_ref[:, 3 * FOX_W:3 * FOX_W + LANES]) + bf_ref[...]
    log_f = jnp.minimum(f, 0.0) - jnp.log1p(jnp.exp(-jnp.abs(f)))
    r = lax.broadcasted_iota(jnp.int32, (tm, tm), 0)
    c = lax.broadcasted_iota(jnp.int32, (tm, tm), 1)
    tri = jnp.where(c <= r, 1.0, 0.0).astype(BF16)
    local = _dot(tri, jnp.concatenate(_split3(log_f), axis=1))
    local = local[:, 0:LANES] + local[:, LANES:2 * LANES] + local[:, 2 * LANES:3 * LANES]

    @pl.when(i % nsb == 0)
    def _():
        carry_ref[...] = jnp.zeros_like(carry_ref)

    cum = local + carry_ref[0:1, :]
    carry_ref[0:1, :] = cum[tm - 1:tm, :]
    lane = lax.broadcasted_iota(jnp.int32, (1, LANES), 1)
    neg_cum = cum * (-LOG2E)
    hi, mid, lo = (part.astype(F32) for part in _split3(neg_cum))
    k_sq = jnp.zeros((1, LANES), F32)
    head_of_lane = jnp.where(lax.broadcasted_iota(jnp.int32, (LANES, LANES), 0) // HEAD_DIM
                             == lax.broadcasted_iota(jnp.int32, (LANES, LANES), 1), 1.0, 0.0).astype(BF16)
    packed = jnp.where(lane < FOX_HEADS, hi,
                       jnp.where(lane < 2 * FOX_HEADS, pltpu.roll(mid, FOX_HEADS, 1),
                                 pltpu.roll(lo, 2 * FOX_HEADS, 1))).astype(BF16)
    low = lane < HEAD_DIM
    q_one = jnp.where((lane >= HEAD_DIM) & (lane < HEAD_DIM + N_CUM_PARTS), 1.0, 0.0)
    v_one = jnp.where(lane >= HEAD_DIM, 1.0, 0.0)
    k_one = jnp.where((lane >= HEAD_DIM + N_CUM_PARTS) & (lane < HEAD_DIM + N_CUM_PARTS + N_SHIFT_PARTS), 1.0, 0.0)
    yq_all = _dot(hb, w_ref[:, 0:FOX_W]) * (QK_SCALE * LOG2E)
    yk_all = _dot(hb, w_ref[:, FOX_W:2 * FOX_W])
    yv_all = _dot(hb, w_ref[:, 2 * FOX_W:3 * FOX_W])
    extras_all = _dot(packed, place_ref[...])
    for p in range(FOX_HEADS // 2):
        pair = slice(p * LANES, (p + 1) * LANES)
        yq, yk, yv = yq_all[:, pair], yk_all[:, pair], yv_all[:, pair]
        pair_sq = jnp.max(_dot((yk * yk * (1.0 + 2.0 ** -7)).astype(BF16), head_of_lane), axis=0, keepdims=True)
        for h in range(2):
            hs = slice((2 * p + h) * LANES, (2 * p + h + 1) * LANES)
            head = (lambda y: y) if h == 0 else (lambda y: pltpu.roll(y, HEAD_DIM, 1))
            kh = jnp.where(low, head(yk), 0.0)
            k_sq = jnp.where(lane == 2 * p + h, pair_sq[:, h:h + 1], k_sq)
            q_ref[:, hs] = (jnp.where(low, head(yq), 0.0) + q_one).astype(BF16)
            k_ref[:, hs] = (kh + extras_all[:, hs] + k_one).astype(BF16)
            v_ref[:, hs] = (jnp.where(low, head(yv), 0.0) + v_one).astype(BF16)
    stats_ref[...] = jnp.zeros(stats_ref.shape, F32)
    stats_ref[0:1, :] = k_sq
    stats_ref[1:2, :] = neg_cum[0:1, :]


def _cum_placement():
    src = jnp.arange(LANES)[:, None]
    dst = jnp.arange(FOX_HEADS * LANES)[None, :]
    n, h = src // FOX_HEADS, src % FOX_HEADS
    return ((n < N_CUM_PARTS) & (dst == h * LANES + HEAD_DIM + n)).astype(BF16)


def _odd_proj(xf, g, w, bf, seq, tm):
    t, d = xf.shape
    nsb = seq // tm
    n = w.shape[1]
    wide = FOX_HEADS * LANES
    row = lambda width: pl.BlockSpec((tm, width), lambda i: (i, 0))
    return pl.pallas_call(
        functools.partial(_odd_proj_kernel, tm=tm, nsb=nsb),
        grid=(t // tm,),
        in_specs=[row(d), pl.BlockSpec((1, d), lambda i: (0, 0)),
                  pl.BlockSpec((d, n), lambda i: (0, 0)), pl.BlockSpec((1, LANES), lambda i: (0, 0)),
                  pl.BlockSpec((LANES, wide), lambda i: (0, 0))],
        out_specs=[row(wide)] * 3 + [pl.BlockSpec((8, LANES), lambda i: (i, 0))],
        out_shape=[jax.ShapeDtypeStruct((t, wide), BF16)] * 3 + [jax.ShapeDtypeStruct((t // tm * 8, LANES), F32)],
        scratch_shapes=[pltpu.VMEM((8, LANES), F32)],
        compiler_params=_params(("arbitrary",)),
        name="odd_proj",
    )(xf, g, w, bf, _cum_placement())


FOX_DEAD = -160.0
FOX_TIGHT = 64.0
N_SHIFT_PARTS = 2


def _fox_kernel(stats_ref, q_ref, k_ref, v_ref, o_ref, knorm_ref, ncum_ref, *, tq, tk, sub, seq, stats_rows):
    b, pair, i = pl.program_id(0), pl.program_id(1), pl.program_id(2)
    q0 = pl.multiple_of(i * tq, tq)
    n_full = q0 // tk
    nsub = tq // sub
    chains = [(h, r) for h in range(2) for r in range(nsub)]
    qs = [q_ref[r * sub:(r + 1) * sub, h * LANES:(h + 1) * LANES] for h, r in chains]
    lane = lax.broadcasted_iota(jnp.int32, (1, LANES), 1)
    feat = lane < HEAD_DIM
    n_stats, per = seq // stats_rows, tk // stats_rows

    def row_norms(x):
        xf = jnp.where(feat, x.astype(F32), 0.0)
        return jnp.sqrt(jnp.sum(xf * xf, axis=-1, keepdims=True))

    @pl.when(i == 0)
    def _():
        for h in range(2):
            def tile_stats(n, best, h=h):
                best = jnp.maximum(best, stats_ref[b * n_stats + n, 2 * pair + h])

                @pl.when(n % per == 0)
                def _():
                    ncum_ref[h, n // per] = stats_ref[b * n_stats + n, FOX_HEADS + 2 * pair + h]
                return best
            knorm_ref[h] = jnp.sqrt(lax.fori_loop(0, n_stats, tile_stats, jnp.float32(0.0)))

    k_norm = [knorm_ref[h] * (NORM_SLACK * (1.0 + 2.0 ** -8)) for h in range(2)]
    q_norm = [row_norms(q) for q in qs]
    qk_bound = [functools.reduce(jnp.maximum, [jnp.max(q_norm[h * nsub + r]) for r in range(nsub)]) * k_norm[h]
                for h in range(2)]
    tight = 2.0 * jnp.maximum(qk_bound[0], qk_bound[1]) <= FOX_TIGHT
    cum_q0 = [ncum_ref[h, n_full] for h in range(2)]

    def tiles(jt, h):
        k0 = pl.multiple_of(jt * tk, tk)
        hs = slice(h * LANES, (h + 1) * LANES)
        return k_ref[pl.ds(k0, tk), hs], v_ref[pl.ds(k0, tk), hs]

    shifts, accs, q_shifted = [], [], []
    for c, ((h, r), q) in enumerate(zip(chains, qs)):
        nk = (r + 1) * sub
        kt = k_ref[pl.ds(q0, nk), h * LANES:(h + 1) * LANES]
        vt = v_ref[pl.ds(q0, nk), h * LANES:(h + 1) * LANES]
        s = _dot_nt(q, kt)
        kpos = lax.broadcasted_iota(jnp.int32, (sub, nk), 1)
        tpos = r * sub + lax.broadcasted_iota(jnp.int32, (sub, nk), 0)
        s = jnp.where(kpos <= tpos, s, -jnp.inf)
        m_diag = jnp.max(s, axis=-1, keepdims=True)
        shift = jnp.where(tight, jnp.maximum(m_diag, q_norm[c] * k_norm[h] + cum_q0[h]), m_diag)
        hi = shift.astype(BF16)
        rest = shift - hi.astype(F32)
        lo = (rest + jnp.abs(rest) * (2.0 ** -7)).astype(BF16)
        shift = hi.astype(F32) + lo.astype(F32)
        accs.append(_dot(jnp.exp2(s - shift).astype(BF16), vt))
        shifts.append(shift)
        q_shifted.append(jnp.where(lane == HEAD_DIM + N_CUM_PARTS, -hi,
                                   jnp.where(lane == HEAD_DIM + N_CUM_PARTS + 1, -lo, q)))

    def may_matter(jt):
        return jnp.maximum(ncum_ref[0, jt + 1] - cum_q0[0], ncum_ref[1, jt + 1] - cum_q0[1]) >= FOX_DEAD

    def fixed_shift_sweep(accs):
        def body(carry):
            jt, accs = carry
            out = []
            for (h, r), q, acc in zip(chains, q_shifted, accs):
                kt, vt = tiles(jt, h)
                out.append(acc + _dot(jnp.exp2(_dot_nt(q, kt)).astype(BF16), vt))
            return jt - 1, tuple(out)

        return lax.while_loop(lambda cr: (cr[0] >= 0) & may_matter(jnp.maximum(cr[0], 0)), body,
                              (n_full - 1, accs))[1]

    def running_max_sweep(accs):
        def alive(jt, ms):
            go = None
            for h in range(2):
                m_min = functools.reduce(jnp.minimum, [jnp.min(ms[h * nsub + r]) for r in range(nsub)])
                live = qk_bound[h] + ncum_ref[h, jt + 1] - m_min >= FOX_DEAD
                go = live if go is None else (go | live)
            return go

        def body(carry):
            jt, _, ms, accs = carry
            new_m, new_acc = [], []
            for (h, r), q, m, acc in zip(chains, qs, ms, accs):
                kt, vt = tiles(jt, h)
                s = _dot_nt(q, kt)
                m_new = jnp.maximum(m, jnp.max(s, axis=-1, keepdims=True))
                new_acc.append(jnp.exp2(m - m_new) * acc + _dot(jnp.exp2(s - m_new).astype(BF16), vt))
                new_m.append(m_new)
            new_m = tuple(new_m)
            return jt - 1, alive(jnp.maximum(jt - 1, 0), new_m), new_m, tuple(new_acc)

        ms = tuple(shifts)
        return lax.while_loop(lambda cr: (cr[0] >= 0) & cr[1], body,
                              (n_full - 1, alive(jnp.maximum(n_full - 1, 0), ms), ms, accs))[3]

    accs = lax.cond(tight, fixed_shift_sweep, running_max_sweep, tuple(accs))
    heads = []
    for h in range(2):
        acc = jnp.concatenate([accs[h * nsub + r] for r in range(nsub)], axis=0)
        heads.append(acc * (1.0 / acc[:, HEAD_DIM:HEAD_DIM + 1]))
    o_ref[...] = jnp.where(feat, heads[0], pltpu.roll(heads[1], HEAD_DIM, 1)).astype(BF16)


def _fox_attention(q, k, v, stats, batch, seq, tq, tk, sub):
    t = q.shape[0]
    nq = seq // tq
    kv = pl.BlockSpec((seq, 2 * LANES), lambda b, p, i: (b, p))
    stats_rows = 8 * t // stats.shape[0]
    assert tk % stats_rows == 0
    tile_stats = stats.reshape(t // stats_rows, 8, LANES)
    tile_stats = jnp.concatenate([tile_stats[:, 0, :FOX_HEADS], tile_stats[:, 1, :FOX_HEADS]], axis=1)
    return pl.pallas_call(
        functools.partial(_fox_kernel, tq=tq, tk=tk, sub=sub, seq=seq, stats_rows=stats_rows),
        grid=(batch, FOX_HEADS // 2, nq),
        in_specs=[pl.BlockSpec(memory_space=pltpu.SMEM),
                  pl.BlockSpec((tq, 2 * LANES), lambda b, p, i: (b * nq + i, p)), kv, kv],
        out_specs=pl.BlockSpec((tq, LANES), lambda b, p, i: (b * nq + i, p)),
        out_shape=jax.ShapeDtypeStruct((t, FOX_W), BF16),
        scratch_shapes=[pltpu.SMEM((2,), F32), pltpu.SMEM((2, seq // tk), F32)],
        compiler_params=_params(("parallel", "parallel", "arbitrary")),
        name="fox_attention",
    )(tile_stats, q, k, v)


HALO = 8


def _ffn_kernel(*refs, tm, nsb, final, n_attn):
    attn_refs, wattn_ref, refs = refs[:n_attn], refs[n_attn], refs[n_attn + 1:]
    if final:
        x_ref, g_ref, win_ref, cw_ref, cb_ref, wout_ref, fn_ref, o_ref, a_scr, halo_scr = refs
    else:
        x_ref, g_ref, win_ref, cw_ref, cb_ref, wout_ref, o_ref, a_scr, halo_scr = refs
    i = pl.program_id(0)

    @pl.when(i % nsb == 0)
    def _():
        halo_scr[...] = jnp.zeros((HALO, D_FF), F32)

    x = x_ref[...]
    off = 0
    for a_ref in attn_refs:
        width = a_ref.shape[1]
        x = x + _dot(a_ref[...], wattn_ref[off:off + width, :])
        off += width
    hb = _rmsnorm(x, g_ref[...]).astype(BF16)
    a = _dot(hb, win_ref[:, 0:D_FF])
    b = _dot(hb, win_ref[:, D_FF:2 * D_FF])
    a_scr[0:HALO, :] = halo_scr[...]
    a_scr[HALO:HALO + tm, :] = a
    halo_scr[...] = a[tm - HALO:tm, :]
    conv = (cw_ref[0:1, :] * a_scr[HALO - 2:HALO - 2 + tm, :] + cw_ref[1:2, :] * a_scr[HALO - 1:HALO - 1 + tm, :]
            + cw_ref[2:3, :] * a + cb_ref[...])
    gated = conv * (1.0 / (1.0 + jnp.exp(-conv))) * b
    y = x + _dot(gated.astype(BF16), wout_ref[...])
    if final:
        y = _rmsnorm(y, fn_ref[...])
    o_ref[...] = y


def _mixer_out_ffn(xf, attn_outs, w_attn, g, w_in, conv_w, conv_b, w_out, final_norm, seq, tm):
    t, d = xf.shape
    nsb = seq // tm
    row = pl.BlockSpec((tm, d), lambda i: (i, 0))
    const = lambda shape: pl.BlockSpec(shape, lambda i: (0, 0), pipeline_mode=pl.Buffered(1))
    small = lambda shape: pl.BlockSpec(shape, lambda i: (0, 0))
    final = final_norm is not None
    in_specs = [pl.BlockSpec((tm, a.shape[1]), lambda i: (i, 0)) for a in attn_outs] + [const(w_attn.shape)]
    in_specs += [row, small((1, d)), const(w_in.shape), small(conv_w.shape), small((1, D_FF)), const(w_out.shape)]
    args = list(attn_outs) + [w_attn, xf, g, w_in, conv_w, conv_b, w_out]
    if final:
        in_specs.append(small((1, d)))
        args.append(final_norm)
    return pl.pallas_call(
        functools.partial(_ffn_kernel, tm=tm, nsb=nsb, final=final, n_attn=len(attn_outs)),
        grid=(t // tm,),
        in_specs=in_specs,
        out_specs=row,
        out_shape=jax.ShapeDtypeStruct((t, d), F32),
        scratch_shapes=[pltpu.VMEM((tm + HALO, D_FF), F32), pltpu.VMEM((HALO, D_FF), F32)],
        compiler_params=_params(("arbitrary",)),
        name="conv_glu_ffn",
    )(*args)


def kernel(x, attn_norm, ffn_norm, ev_w_in, ev_cmp_pos_k, ev_cmp_pos_v, ev_cmp_w_k, ev_cmp_w_v, ev_w_out,
           od_w_in, od_b_f, od_w_out, ffn_w_in, ffn_conv_w, ffn_conv_b, ffn_w_out, final_norm):
    batch, seq, d = x.shape
    t = batch * seq
    depth = attn_norm.shape[0]
    tm, ffn_tm = min(512, seq), min(512, seq)
    sb_tile, sb_nsub = min(256, seq), 2
    fox_tq, fox_tk, fox_sub = min(512, seq), min(512, seq), 512
    sel_tk = min(1024, seq)

    xf = x.reshape(t, d)
    tabs = _rope_tables(jnp.arange(seq))
    ctabs = _rope_tables(jnp.arange(seq // CMP_STRIDE) * CMP_STRIDE + (CMP_LEN - 1))
    mt, oh = _nsa_consts(seq)

    for layer in range(depth):
        g_attn = attn_norm[layer].reshape(1, d)
        if layer % 2 == 0:
            e = layer // 2
            (sbq, sbk, sbv, nq, kc_raw, vc_raw, ksl, vsl, kwn, vwn, gates) = _even_proj(
                xf, g_attn, _prep_even_w(ev_w_in[e]), tabs, seq, tm)
            pek, wk = _prep_cmp(ev_cmp_pos_k[e], ev_cmp_w_k[e])
            pev, wv = _prep_cmp(ev_cmp_pos_v[e], ev_cmp_w_v[e])
            kc, vc = _compress(kc_raw, vc_raw, pek, pev, wk, wv, ctabs, batch, seq)
            o_sb = _sb_attention(sbq, sbk, sbv, batch, seq, sb_tile, sb_nsub)
            o_nsa = _nsa_attention(nq, kc, vc, ksl, vsl, kwn, vwn, gates, mt, oh, batch, seq, sel_tk)
            attn_outs, w_attn = [o_sb, o_nsa], ev_w_out[e].astype(BF16)
        else:
            o = layer // 2
            w = jnp.pad(od_w_in[o], ((0, 0), (0, LANES - FOX_HEADS))).astype(BF16)
            bf = jnp.pad(od_b_f[o], (0, LANES - FOX_HEADS)).reshape(1, LANES)
            q, k, v, stats = _odd_proj(xf, g_attn, w, bf, seq, tm)
            o_fox = _fox_attention(q, k, v, stats, batch, seq, fox_tq, fox_tk, fox_sub)
            attn_outs, w_attn = [o_fox], od_w_out[o].astype(BF16)
        last = layer == depth - 1
        xf = _mixer_out_ffn(xf, attn_outs, w_attn, ffn_norm[layer].reshape(1, d), ffn_w_in[layer].astype(BF16),
                            ffn_conv_w[layer], ffn_conv_b[layer].reshape(1, D_FF), ffn_w_out[layer].astype(BF16),
                            final_norm.reshape(1, d) if last else None, seq, ffn_tm)
    return xf.reshape(batch, seq, d)
```

```python
import functools
import math

import jax
import jax.numpy as jnp
from jax import lax
from jax.experimental import pallas as pl
from jax.experimental.pallas import tpu as pltpu

F32, BF16 = jnp.float32, jnp.bfloat16

D_MODEL = 1024
HEAD_DIM = 64
N_HEADS = D_MODEL // HEAD_DIM
SB_HEADS = N_HEADS // 2
NSA_HEADS = N_HEADS - SB_HEADS
NSA_KV_GROUPS = 2
NSA_REP = NSA_HEADS // NSA_KV_GROUPS
FOX_HEADS = N_HEADS
CMP_LEN = 32
CMP_STRIDE = 16
SEL_LEN = 64
SEL_TOP = 16
WINDOW = 512
N_BRANCH = 3
ROPE_THETA = 500000.0
ROT_DIM = HEAD_DIM // 4
D_FF = 2816
CONV_WIDTH = 3
NORM_EPS = 1e-6
SB_W = SB_HEADS * HEAD_DIM
NSA_QW = NSA_HEADS * HEAD_DIM
NSA_KVW = NSA_KV_GROUPS * HEAD_DIM
FOX_W = FOX_HEADS * HEAD_DIM

LANES = 128
SEL_LANES = 128
QK_SCALE = HEAD_DIM ** -0.5
LOG2E = math.log2(math.e)
SEL_NEG = -(2.0 ** 30)
NSA_TIGHT = 86.0
NORM_SLACK = 1.001
VMEM_LIMIT = 56 * 2 ** 20

_NT = (((1,), (1,)), ((), ()))


def _params(sem):
    return pltpu.CompilerParams(dimension_semantics=sem, vmem_limit_bytes=VMEM_LIMIT)


def _dot(a, b):
    return jnp.dot(a, b, preferred_element_type=F32)


def _dot_nt(a, b):
    return lax.dot_general(a, b, _NT, preferred_element_type=F32)


def _rmsnorm(x, g):
    ms = jnp.mean(x * x, axis=-1, keepdims=True)
    return (x * lax.rsqrt(ms + NORM_EPS)) * g


def _rope(y, c, s1, s2):
    return y * c + pltpu.roll(y, LANES - ROT_DIM // 2, 1) * s1 + pltpu.roll(y, ROT_DIM // 2, 1) * s2


def _split3(x):
    hi = x.astype(BF16)
    r1 = x - hi.astype(F32)
    mid = r1.astype(BF16)
    lo = (r1 - mid.astype(F32)).astype(BF16)
    return hi, mid, lo


def _rope_tables(pos):
    half = ROT_DIM // 2
    inv_freq = ROPE_THETA ** (-(jnp.arange(half, dtype=F32) * 2.0 / ROT_DIM))
    ang = pos.astype(F32)[:, None] * inv_freq[None, :]
    cos, sin = jnp.cos(ang), jnp.sin(ang)
    n = pos.shape[0]
    one = jnp.ones((n, HEAD_DIM - ROT_DIM), F32)
    zero = jnp.zeros((n, HEAD_DIM - ROT_DIM), F32)
    z8 = jnp.zeros((n, half), F32)
    c = jnp.concatenate([cos, cos, one], -1)
    s1 = jnp.concatenate([-sin, z8, zero], -1)
    s2 = jnp.concatenate([z8, sin, zero], -1)
    two = lambda t: jnp.concatenate([t, t], -1)
    return two(c), two(s1), two(s2)


def _even_proj_kernel(x_ref, g_ref, w_ref, c_ref, s1_ref, s2_ref,
                      sbq_ref, sbk_ref, sbv_ref, nq_ref, kc_ref, vc_ref,
                      ksl_ref, vsl_ref, kwn_ref, vwn_ref, gate_ref):
    hb = _rmsnorm(x_ref[...], g_ref[...]).astype(BF16)
    c, s1, s2 = c_ref[...], s1_ref[...], s2_ref[...]

    sb = _dot(hb, w_ref[:, 0:3 * SB_W])
    sbq_ref[...] = (sb[:, 0:SB_W] * QK_SCALE).astype(BF16)
    sbk_ref[...] = sb[:, SB_W:2 * SB_W].astype(BF16)
    sbv_ref[...] = sb[:, 2 * SB_W:3 * SB_W].astype(BF16)
    nsa = _dot(hb, w_ref[:, 3 * SB_W:])
    piece = lambda j: nsa[:, j * LANES:(j + 1) * LANES]
    n_q = NSA_QW // LANES
    for j in range(n_q):
        nq_ref[:, j * LANES:(j + 1) * LANES] = (_rope(piece(j), c, s1, s2) * (QK_SCALE * LOG2E)).astype(BF16)
    kc_ref[...] = piece(n_q)
    vc_ref[...] = piece(n_q + 1)
    ksl_ref[...] = _rope(piece(n_q + 2), c, s1, s2).astype(BF16)
    vsl_ref[...] = piece(n_q + 3).astype(BF16)
    kwn_ref[...] = _rope(piece(n_q + 4), c, s1, s2).astype(BF16)
    vwn_ref[...] = piece(n_q + 5).astype(BF16)
    gate_ref[...] = 1.0 / (1.0 + jnp.exp(-nsa[:, (n_q + 6) * LANES:(n_q + 8) * LANES]))


def _even_proj(xf, g, w, tabs, seq, tm):
    t, d = xf.shape
    nsb = seq // tm
    n = w.shape[1]
    row = lambda width: pl.BlockSpec((tm, width), lambda i: (i, 0))
    tab = pl.BlockSpec((tm, LANES), lambda i: (i % nsb, 0))
    out_shape = (
        [jax.ShapeDtypeStruct((t, SB_W), BF16)] * 3
        + [jax.ShapeDtypeStruct((t, NSA_QW), BF16)]
        + [jax.ShapeDtypeStruct((t, LANES), F32)] * 2
        + [jax.ShapeDtypeStruct((t, LANES), BF16)] * 4
        + [jax.ShapeDtypeStruct((t, 2 * LANES), F32)]
    )
    out_specs = [row(SB_W)] * 3 + [row(NSA_QW)] + [row(LANES)] * 6 + [row(2 * LANES)]
    return pl.pallas_call(
        _even_proj_kernel,
        grid=(t // tm,),
        in_specs=[row(d), pl.BlockSpec((1, d), lambda i: (0, 0)),
                  pl.BlockSpec((d, n), lambda i: (0, 0)), tab, tab, tab],
        out_specs=out_specs,
        out_shape=out_shape,
        compiler_params=_params(("parallel",)),
        name="even_proj",
    )(xf, g, w, *tabs)


def _prep_even_w(w):
    d = w.shape[0]
    main = w[:, :3 * SB_W + NSA_QW + 6 * NSA_KVW]
    gates = w[:, 3 * SB_W + NSA_QW + 6 * NSA_KVW:].reshape(d, NSA_KV_GROUPS, NSA_REP, N_BRANCH)
    gates = jnp.transpose(gates, (0, 1, 3, 2)).reshape(d, NSA_KV_GROUPS, N_BRANCH * NSA_REP)
    gates = jnp.pad(gates, ((0, 0), (0, 0), (0, LANES - N_BRANCH * NSA_REP)))
    return jnp.concatenate([main, gates.reshape(d, NSA_KV_GROUPS * LANES)], axis=1).astype(BF16)


SB_DEAD = -105.0


def _sb_kernel(q_ref, k_ref, v_ref, o_ref, *, tile, nsub):
    i = pl.program_id(2)
    lane = lax.broadcasted_iota(jnp.int32, (1, LANES), 1)
    r = lax.broadcasted_iota(jnp.int32, (tile, tile), 0)
    c = lax.broadcasted_iota(jnp.int32, (tile, tile), 1)
    later = jnp.where(r > c, 1.0, 0.0).astype(BF16)
    diag = c < r
    chains = [(h, sub) for h in range(2) for sub in range(nsub)]
    qs = []
    for h, sub in chains:
        q = q_ref[sub * tile:(sub + 1) * tile, :]
        qs.append(jnp.where((lane < HEAD_DIM) if h == 0 else (lane >= HEAD_DIM), q, jnp.zeros_like(q)))

    def step(jt, state, mode):
        k0 = pl.multiple_of(jt * tile, tile)
        kt = k_ref[pl.ds(k0, tile), :]
        vt = v_ref[pl.ds(k0, tile), :]
        out = []
        for c, (acc, keep_sum) in enumerate(state):
            if mode[c] is None:
                out.append((acc, keep_sum))
                continue
            z = _dot_nt(qs[c], kt)
            ls = jnp.minimum(z, 0.0) - jnp.log(1.0 + jnp.exp(-jnp.abs(z)))
            lk = ls - z
            if mode[c]:
                lk = jnp.where(diag, lk, 0.0)
            hi = lk.astype(BF16)
            lo = (lk - hi.astype(F32)).astype(BF16)
            both = _dot(jnp.concatenate([hi, lo], axis=0), later)
            after = both[0:tile] + both[tile:2 * tile]
            a = jnp.exp(ls + after + keep_sum)
            if mode[c]:
                a = jnp.where(diag, a, 0.0)
            out.append((acc + _dot(a.astype(BF16), vt), keep_sum + after[:, 0:1] + lk[:, 0:1]))
        return tuple(out)

    def alive(state):
        return functools.reduce(jnp.maximum, [jnp.max(keep_sum) for _, keep_sum in state]) >= SB_DEAD

    state = tuple((jnp.zeros((tile, LANES), F32), jnp.zeros((tile, 1), F32)) for _ in chains)
    for top in range(nsub - 1, -1, -1):
        mode = [None if sub < top else sub == top for _, sub in chains]
        state = step(nsub * i + top, state, mode)

    def body(carry):
        jt, _, state = carry
        state = step(jt, state, [False] * len(chains))
        return jt - 1, alive(state), state

    _, _, state = lax.while_loop(lambda cr: (cr[0] >= 0) & cr[1], body, (nsub * i - 1, alive(state), state))
    heads = [jnp.concatenate([state[h * nsub + sub][0] for sub in range(nsub)], axis=0) for h in range(2)]
    o_ref[...] = jnp.where(lane < HEAD_DIM, heads[0], heads[1]).astype(BF16)


def _sb_attention(q, k, v, batch, seq, tile, nsub):
    t, w = q.shape
    nq = seq // (tile * nsub)
    kv = pl.BlockSpec((seq, LANES), lambda b, p, i: (b, p))
    qo = pl.BlockSpec((tile * nsub, LANES), lambda b, p, i: (b * nq + i, p))
    return pl.pallas_call(
        functools.partial(_sb_kernel, tile=tile, nsub=nsub),
        grid=(batch, w // LANES, nq),
        in_specs=[qo, kv, kv],
        out_specs=qo,
        out_shape=jax.ShapeDtypeStruct((t, w), BF16),
        compiler_params=_params(("parallel", "parallel", "arbitrary")),
        name="sb_attention",
    )(q, k, v)


def _compress_kernel(xk_ref, xv_ref, pek_ref, pev_ref, wk_ref, wv_ref, c_ref, s1_ref, s2_ref,
                     kc_ref, vc_ref, *, nc):
    def comp(x_ref, pe_ref, w_ref):
        x = x_ref[0]
        top = _dot((x + pe_ref[0:1, :]).astype(BF16), w_ref[0])
        bot = _dot((x + pe_ref[1:2, :]).astype(BF16), w_ref[1])
        return top + pltpu.roll(bot, nc - 1, 0)

    kc = _rope(comp(xk_ref, pek_ref, wk_ref), c_ref[...], s1_ref[...], s2_ref[...])
    kc_ref[0] = kc.astype(BF16)
    vc_ref[0] = comp(xv_ref, pev_ref, wv_ref).astype(BF16)


def _prep_cmp(pe, w):
    half = CMP_LEN // 2
    w3 = w.reshape(CMP_LEN, HEAD_DIM, HEAD_DIM)
    eye = jnp.eye(NSA_KV_GROUPS, dtype=w.dtype)
    parts, pes = [], []
    for s in range(2):
        wh = jnp.einsum('ldo,gh->lgdho', w3[s * half:(s + 1) * half], eye)
        parts.append(wh.reshape(half * NSA_KVW, NSA_KVW))
        pes.append(jnp.broadcast_to(pe[s * half:(s + 1) * half, None, :],
                                    (half, NSA_KV_GROUPS, HEAD_DIM)).reshape(1, half * NSA_KVW))
    return jnp.concatenate(pes, 0), jnp.stack(parts).astype(BF16)


def _compress(kc_raw, vc_raw, pek, pev, wk, wv, ctabs, batch, seq):
    nc = seq // CMP_STRIDE
    cw = CMP_STRIDE * NSA_KVW
    xk = kc_raw.reshape(batch, nc, cw)
    xv = vc_raw.reshape(batch, nc, cw)
    xs = pl.BlockSpec((1, nc, cw), lambda b: (b, 0, 0))
    pes = pl.BlockSpec((2, cw), lambda b: (0, 0))
    ws = pl.BlockSpec((2, cw, NSA_KVW), lambda b: (0, 0, 0))
    tab = pl.BlockSpec((nc, LANES), lambda b: (0, 0))
    out = pl.BlockSpec((1, nc, LANES), lambda b: (b, 0, 0))
    return pl.pallas_call(
        functools.partial(_compress_kernel, nc=nc),
        grid=(batch,),
        in_specs=[xs, xs, pes, pes, ws, ws, tab, tab, tab],
        out_specs=[out, out],
        out_shape=[jax.ShapeDtypeStruct((batch, nc, LANES), BF16)] * 2,
        compiler_params=_params(("parallel",)),
        name="nsa_compress",
    )(xk, xv, pek, pev, wk, wv, *ctabs)


QB = 128


def _softmax_rows(s):
    m = jnp.max(s, axis=-1, keepdims=True)
    m = jnp.where(m > -jnp.inf, m, 0.0)
    e = jnp.exp2(s - m)
    d = jnp.sum(e, axis=-1, keepdims=True)
    return e / jnp.where(d > 0, d, 1.0)


def _nsa_kernel(q_ref, kc_ref, vc_ref, ks_ref, vs_ref, kw_ref, vw_ref, g_ref, mt_ref, oh_ref,
                o_ref, score_ref, knorm_ref, *, nc, tk, wlen, seq):
    g = pl.program_id(1)
    i = pl.program_id(2)
    q0 = i * QB
    rows = NSA_REP * QB

    q2 = q_ref[...]
    pr = lax.broadcasted_iota(jnp.int32, (NSA_REP * HEAD_DIM, LANES), 0)
    pc = lax.broadcasted_iota(jnp.int32, (NSA_REP * HEAD_DIM, LANES), 1) - g * HEAD_DIM
    in_group = (pc >= 0) & (pc < HEAD_DIM)
    q4 = jnp.concatenate(
        [_dot(q2, jnp.where(in_group & (pr - hh * HEAD_DIM == pc), 1.0, 0.0).astype(BF16)).astype(BF16)
         for hh in range(NSA_REP)], axis=0)

    sc = _dot_nt(q4, kc_ref[0])
    cend = lax.broadcasted_iota(jnp.int32, (rows, nc), 1) * CMP_STRIDE + (CMP_LEN - 1)
    tpos_c = q0 + (lax.broadcasted_iota(jnp.int32, (rows, nc), 0) & (QB - 1))
    p_cmp = _softmax_rows(jnp.where(cend <= tpos_c, sc, -jnp.inf))
    o_cmp = _dot(p_cmp.astype(BF16), vc_ref[0])

    p_sum = p_cmp[0:QB] + p_cmp[QB:2 * QB] + p_cmp[2 * QB:3 * QB] + p_cmp[3 * QB:4 * QB]
    overlap = mt_ref[...]
    parts = _dot(jnp.concatenate(_split3(p_sum), axis=0), overlap)
    imp_t = jnp.transpose(parts[0:QB] + parts[QB:2 * QB] + parts[2 * QB:3 * QB])
    nidx = lax.broadcasted_iota(jnp.int32, (SEL_LANES, QB), 0)
    cur = (q0 + lax.broadcasted_iota(jnp.int32, (SEL_LANES, QB), 1)) // SEL_LEN
    causal = nidx <= cur
    forced = (nidx == 0) | (nidx == cur) | (nidx == cur - 1)
    score = jnp.where(causal, jnp.where(forced, jnp.inf, imp_t), -jnp.inf)
    score_ref[...] = score

    def larger_step(m, cnt):
        return cnt + jnp.where(score_ref[pl.ds(m, 1), :] > score, 1.0, 0.0)

    def larger_quad(m4, cnt):
        for j in range(4):
            cnt = larger_step(4 * m4 + j, cnt)
        return cnt

    def rank_step(m, cnt):
        rowm = score_ref[pl.ds(m, 1), :]
        ge = jnp.where(rowm >= score, 1.0, 0.0)
        gt = jnp.where(rowm > score, 1.0, 0.0)
        return cnt + jnp.where(nidx > m, ge, gt)

    n_causal = (q0 + QB) // SEL_LEN
    zero_cnt = jnp.zeros((SEL_LANES, QB), F32)
    cnt = lax.fori_loop(0, (n_causal + 3) // 4, larger_quad, zero_cnt)
    n_top = jnp.sum(jnp.where(causal & (cnt < float(SEL_TOP)), 1.0, 0.0), axis=0, keepdims=True)
    cnt = lax.cond(jnp.max(n_top) > float(SEL_TOP),
                   lambda c: lax.fori_loop(0, n_causal, rank_step, zero_cnt), lambda c: c, cnt)
    bias_t = jnp.where(causal & (cnt < float(SEL_TOP)), 0.0, SEL_NEG)
    selb4 = jnp.concatenate([jnp.transpose(bias_t)] * NSA_REP, axis=0)

    lane = lax.broadcasted_iota(jnp.int32, (1, LANES), 1)
    own_lanes = (lane >= g * HEAD_DIM) & (lane < (g + 1) * HEAD_DIM)

    @pl.when(i == 0)
    def _():
        feat = lax.broadcasted_iota(jnp.int32, (LANES, 1), 0)
        own_rows = (feat >= g * HEAD_DIM) & (feat < (g + 1) * HEAD_DIM)

        def chunk(n, best):
            kf = ks_ref[0, :, pl.ds(pl.multiple_of(n * tk, tk), tk)].astype(F32)
            kf = jnp.where(own_rows, kf, 0.0)
            return jnp.maximum(best, jnp.max(jnp.sum(kf * kf, axis=0, keepdims=True)))
        knorm_ref[0] = jnp.sqrt(lax.fori_loop(0, seq // tk, chunk, jnp.float32(0.0)))

    q4f = q4.astype(F32)
    shift = jnp.sqrt(jnp.sum(q4f * q4f, axis=-1, keepdims=True)) * (knorm_ref[0] * NORM_SLACK)
    shift = (shift * (1.0 + 2.0 ** -7)).astype(BF16).astype(F32)
    tight = 2.0 * jnp.max(shift) <= NSA_TIGHT
    qs = jnp.concatenate([q4, jnp.where(tight, selb4 - shift, selb4).astype(BF16)], axis=1)
    half = tk // 2
    diag0 = pl.multiple_of((q0 // half) * half, half)
    n_big = diag0 // tk
    odd_half = diag0 - n_big * tk > 0
    half0 = pl.multiple_of(n_big * tk, tk)

    def scores(k0, size, masked):
        kk = jnp.concatenate([ks_ref[0, :, pl.ds(k0, size)], oh_ref[:, pl.ds(k0, size)]], axis=0)
        vt = vs_ref[pl.ds(k0, size), :]
        vt = jnp.where(own_lanes, vt, jnp.ones_like(vt))
        s = _dot(qs, kk)
        if masked:
            kpos = k0 + lax.broadcasted_iota(jnp.int32, (rows, size), 1)
            tpos = q0 + (lax.broadcasted_iota(jnp.int32, (rows, size), 0) & (QB - 1))
            s = jnp.where(kpos <= tpos, s, -jnp.inf)
        return s, vt

    def sweep(step, carry):
        carry = lax.fori_loop(0, n_big, lambda j, cr: step(pl.multiple_of(j * tk, tk), tk, False, cr), carry)
        carry = lax.cond(odd_half, lambda cr: step(half0, half, False, cr), lambda cr: cr, carry)
        return step(diag0, half, True, carry)

    def fixed_shift_sweep(acc):
        def step(k0, size, masked, acc):
            s, vt = scores(k0, size, masked)
            return acc + _dot(jnp.exp2(s).astype(BF16), vt)
        return sweep(step, acc)

    def running_max_sweep(acc):
        def step(k0, size, masked, carry):
            m, acc = carry
            s, vt = scores(k0, size, masked)
            m_new = jnp.maximum(m, jnp.max(s, axis=-1, keepdims=True))
            return m_new, jnp.exp2(m - m_new) * acc + _dot(jnp.exp2(s - m_new).astype(BF16), vt)
        return sweep(step, (jnp.full((rows, 1), -jnp.inf, F32), acc))[1]

    acc_s = lax.cond(tight, fixed_shift_sweep, running_max_sweep, jnp.zeros((rows, LANES), F32))
    l_s = jnp.where(g == 0, acc_s[:, HEAD_DIM:HEAD_DIM + 1], acc_s[:, 0:1])
    o_slc = acc_s / l_s

    w0 = pl.multiple_of(jnp.maximum(q0 - WINDOW, 0), QB)
    sw = _dot_nt(q4, kw_ref[pl.ds(w0, wlen), :])
    kpos = w0 + lax.broadcasted_iota(jnp.int32, (rows, wlen), 1)
    tpos = q0 + (lax.broadcasted_iota(jnp.int32, (rows, wlen), 0) & (QB - 1))
    sw = jnp.where((kpos <= tpos) & (kpos > tpos - WINDOW), sw, -jnp.inf)
    e_win = jnp.exp2(sw - jnp.max(sw, axis=-1, keepdims=True))
    o_win = _dot(e_win.astype(BF16), vw_ref[pl.ds(w0, wlen), :]) / jnp.sum(e_win, axis=-1, keepdims=True)

    gates = g_ref[...]
    orow = lax.broadcasted_iota(jnp.int32, (LANES, NSA_REP * HEAD_DIM), 0) - g * HEAD_DIM
    ocol = lax.broadcasted_iota(jnp.int32, (LANES, NSA_REP * HEAD_DIM), 1)
    o_group = (orow >= 0) & (orow < HEAD_DIM)
    out = jnp.zeros((QB, NSA_REP * HEAD_DIM), F32)
    for hh in range(NSA_REP):
        sl = slice(hh * QB, (hh + 1) * QB)
        gate = lambda br: gates[:, br * NSA_REP + hh:br * NSA_REP + hh + 1]
        o_h = gate(0) * o_cmp[sl] + gate(1) * o_slc[sl] + gate(2) * o_win[sl]
        place = jnp.where(o_group & (ocol - hh * HEAD_DIM == orow), 1.0, 0.0).astype(BF16)
        out = out + _dot(o_h.astype(BF16), place)
    o_ref[...] = out.astype(BF16)


def _nsa_consts(seq):
    nc = seq // CMP_STRIDE
    c_start = jnp.arange(nc) * CMP_STRIDE
    sel_start = jnp.arange(SEL_LANES) * SEL_LEN
    real = (jnp.arange(SEL_LANES) < seq // SEL_LEN)[:, None] & (jnp.arange(nc) < (seq - CMP_LEN) // CMP_STRIDE + 1)[None, :]
    overlap = (c_start[None, :] < sel_start[:, None] + SEL_LEN) & (c_start[None, :] + CMP_LEN > sel_start[:, None])
    mt = jnp.transpose(overlap & real).astype(BF16)
    oh = (jnp.arange(seq)[None, :] // SEL_LEN == jnp.arange(SEL_LANES)[:, None]).astype(BF16)
    return mt, oh


def _nsa_attention(nq, kc, vc, ksl, vsl, kwn, vwn, gates, mt, oh, batch, seq, tk):
    t = nq.shape[0]
    nqb = seq // QB
    nc = seq // CMP_STRIDE
    wlen = WINDOW + QB
    gw = NSA_REP * HEAD_DIM
    qspec = pl.BlockSpec((QB, gw), lambda b, g, i: (b * nqb + i, g))
    cspec = pl.BlockSpec((1, nc, LANES), lambda b, g, i: (b, 0, 0))
    kvspec = pl.BlockSpec((seq, LANES), lambda b, g, i: (b, 0))
    gspec = pl.BlockSpec((QB, LANES), lambda b, g, i: (b * nqb + i, g))
    return pl.pallas_call(
        functools.partial(_nsa_kernel, nc=nc, tk=tk, wlen=wlen, seq=seq),
        grid=(batch, NSA_KV_GROUPS, nqb),
        in_specs=[qspec, cspec, cspec, pl.BlockSpec((1, LANES, seq), lambda b, g, i: (b, 0, 0)),
                  kvspec, kvspec, kvspec, gspec,
                  pl.BlockSpec((nc, SEL_LANES), lambda b, g, i: (0, 0)),
                  pl.BlockSpec((SEL_LANES, seq), lambda b, g, i: (0, 0))],
        out_specs=qspec,
        out_shape=jax.ShapeDtypeStruct((t, NSA_QW), BF16),
        scratch_shapes=[pltpu.VMEM((SEL_LANES, QB), F32), pltpu.SMEM((1,), F32)],
        compiler_params=_params(("parallel", "parallel", "arbitrary")),
        name="nsa_attention",
    )(nq, kc, vc, jnp.swapaxes(ksl.reshape(batch, seq, LANES), 1, 2), vsl, kwn, vwn, gates, mt, oh)


N_CUM_PARTS = 3


def _odd_proj_kernel(x_ref, g_ref, w_ref, bf_ref, place_ref, q_ref, k_ref, v_ref, stats_ref, carry_ref,
                     *, tm, nsb):
    i = pl.program_id(0)
    hb = _rmsnorm(x_ref[...], g_ref[...]).astype(BF16)
    f = _dot(hb, w_ref[:, 3 * FOX_W:3 * FOX_W + LANES]) + bf_ref[...]
    log_f = jnp.minimum(f, 0.0) - jnp.log1p(jnp.exp(-jnp.abs(f)))
    r = lax.broadcasted_iota(jnp.int32, (tm, tm), 0)
    c = lax.broadcasted_iota(jnp.int32, (tm, tm), 1)
    tri = jnp.where(c <= r, 1.0, 0.0).astype(BF16)
    local = _dot(tri, jnp.concatenate(_split3(log_f), axis=1))
    local = local[:, 0:LANES] + local[:, LANES:2 * LANES] + local[:, 2 * LANES:3 * LANES]

    @pl.when(i % nsb == 0)
    def _():
        carry_ref[...] = jnp.zeros_like(carry_ref)

    cum = local + carry_ref[0:1, :]
    carry_ref[0:1, :] = cum[tm - 1:tm, :]
    lane = lax.broadcasted_iota(jnp.int32, (1, LANES), 1)
    neg_cum = cum * (-LOG2E)
    hi, mid, lo = (part.astype(F32) for part in _split3(neg_cum))
    k_sq = jnp.zeros((1, LANES), F32)
    head_of_lane = jnp.where(lax.broadcasted_iota(jnp.int32, (LANES, LANES), 0) // HEAD_DIM
                             == lax.broadcasted_iota(jnp.int32, (LANES, LANES), 1), 1.0, 0.0).astype(BF16)
    packed = jnp.where(lane < FOX_HEADS, hi,
                       jnp.where(lane < 2 * FOX_HEADS, pltpu.roll(mid, FOX_HEADS, 1),
                                 pltpu.roll(lo, 2 * FOX_HEADS, 1))).astype(BF16)
    low = lane < HEAD_DIM
    q_one = jnp.where((lane >= HEAD_DIM) & (lane < HEAD_DIM + N_CUM_PARTS), 1.0, 0.0)
    v_one = jnp.where(lane >= HEAD_DIM, 1.0, 0.0)
    k_one = jnp.where((lane >= HEAD_DIM + N_CUM_PARTS) & (lane < HEAD_DIM + N_CUM_PARTS + N_SHIFT_PARTS), 1.0, 0.0)
    yq_all = _dot(hb, w_ref[:, 0:FOX_W]) * (QK_SCALE * LOG2E)
    yk_all = _dot(hb, w_ref[:, FOX_W:2 * FOX_W])
    yv_all = _dot(hb, w_ref[:, 2 * FOX_W:3 * FOX_W])
    extras_all = _dot(packed, place_ref[...])
    for p in range(FOX_HEADS // 2):
        pair = slice(p * LANES, (p + 1) * LANES)
        yq, yk, yv = yq_all[:, pair], yk_all[:, pair], yv_all[:, pair]
        pair_sq = jnp.max(_dot((yk * yk * (1.0 + 2.0 ** -7)).astype(BF16), head_of_lane), axis=0, keepdims=True)
        for h in range(2):
            hs = slice((2 * p + h) * LANES, (2 * p + h + 1) * LANES)
            head = (lambda y: y) if h == 0 else (lambda y: pltpu.roll(y, HEAD_DIM, 1))
            kh = jnp.where(low, head(yk), 0.0)
            k_sq = jnp.where(lane == 2 * p + h, pair_sq[:, h:h + 1], k_sq)
            q_ref[:, hs] = (jnp.where(low, head(yq), 0.0) + q_one).astype(BF16)
            k_ref[:, hs] = (kh + extras_all[:, hs] + k_one).astype(BF16)
            v_ref[:, hs] = (jnp.where(low, head(yv), 0.0) + v_one).astype(BF16)
    stats_ref[...] = jnp.zeros(stats_ref.shape, F32)
    stats_ref[0:1, :] = k_sq
    stats_ref[1:2, :] = neg_cum[0:1, :]


def _cum_placement():
    src = jnp.arange(LANES)[:, None]
    dst = jnp.arange(FOX_HEADS * LANES)[None, :]
    n, h = src // FOX_HEADS, src % FOX_HEADS
    return ((n < N_CUM_PARTS) & (dst == h * LANES + HEAD_DIM + n)).astype(BF16)


def _odd_proj(xf, g, w, bf, seq, tm):
    t, d = xf.shape
    nsb = seq // tm
    n = w.shape[1]
    wide = FOX_HEADS * LANES
    row = lambda width: pl.BlockSpec((tm, width), lambda i: (i, 0))
    return pl.pallas_call(
        functools.partial(_odd_proj_kernel, tm=tm, nsb=nsb),
        grid=(t // tm,),
        in_specs=[row(d), pl.BlockSpec((1, d), lambda i: (0, 0)),
                  pl.BlockSpec((d, n), lambda i: (0, 0)), pl.BlockSpec((1, LANES), lambda i: (0, 0)),
                  pl.BlockSpec((LANES, wide), lambda i: (0, 0))],
        out_specs=[row(wide)] * 3 + [pl.BlockSpec((8, LANES), lambda i: (i, 0))],
        out_shape=[jax.ShapeDtypeStruct((t, wide), BF16)] * 3 + [jax.ShapeDtypeStruct((t // tm * 8, LANES), F32)],
        scratch_shapes=[pltpu.VMEM((8, LANES), F32)],
        compiler_params=_params(("arbitrary",)),
        name="odd_proj",
    )(xf, g, w, bf, _cum_placement())


FOX_DEAD = -160.0
FOX_TIGHT = 64.0
N_SHIFT_PARTS = 2


def _fox_kernel(stats_ref, q_ref, k_ref, v_ref, o_ref, knorm_ref, ncum_ref, *, tq, tk, sub, seq, stats_rows):
    b, pair, i = pl.program_id(0), pl.program_id(1), pl.program_id(2)
    q0 = pl.multiple_of(i * tq, tq)
    n_full = q0 // tk
    nsub = tq // sub
    chains = [(h, r) for h in range(2) for r in range(nsub)]
    qs = [q_ref[r * sub:(r + 1) * sub, h * LANES:(h + 1) * LANES] for h, r in chains]
    lane = lax.broadcasted_iota(jnp.int32, (1, LANES), 1)
    feat = lane < HEAD_DIM
    n_stats, per = seq // stats_rows, tk // stats_rows

    def row_norms(x):
        xf = jnp.where(feat, x.astype(F32), 0.0)
        return jnp.sqrt(jnp.sum(xf * xf, axis=-1, keepdims=True))

    @pl.when(i == 0)
    def _():
        for h in range(2):
            def tile_stats(n, best, h=h):
                best = jnp.maximum(best, stats_ref[b * n_stats + n, 2 * pair + h])

                @pl.when(n % per == 0)
                def _():
                    ncum_ref[h, n // per] = stats_ref[b * n_stats + n, FOX_HEADS + 2 * pair + h]
                return best
            knorm_ref[h] = jnp.sqrt(lax.fori_loop(0, n_stats, tile_stats, jnp.float32(0.0)))

    k_norm = [knorm_ref[h] * (NORM_SLACK * (1.0 + 2.0 ** -8)) for h in range(2)]
    q_norm = [row_norms(q) for q in qs]
    qk_bound = [functools.reduce(jnp.maximum, [jnp.max(q_norm[h * nsub + r]) for r in range(nsub)]) * k_norm[h]
                for h in range(2)]
    tight = 2.0 * jnp.maximum(qk_bound[0], qk_bound[1]) <= FOX_TIGHT
    cum_q0 = [ncum_ref[h, n_full] for h in range(2)]

    def tiles(jt, h):
        k0 = pl.multiple_of(jt * tk, tk)
        hs = slice(h * LANES, (h + 1) * LANES)
        return k_ref[pl.ds(k0, tk), hs], v_ref[pl.ds(k0, tk), hs]

    shifts, accs, q_shifted = [], [], []
    for c, ((h, r), q) in enumerate(zip(chains, qs)):
        nk = (r + 1) * sub
        kt = k_ref[pl.ds(q0, nk), h * LANES:(h + 1) * LANES]
        vt = v_ref[pl.ds(q0, nk), h * LANES:(h + 1) * LANES]
        s = _dot_nt(q, kt)
        kpos = lax.broadcasted_iota(jnp.int32, (sub, nk), 1)
        tpos = r * sub + lax.broadcasted_iota(jnp.int32, (sub, nk), 0)
        s = jnp.where(kpos <= tpos, s, -jnp.inf)
        m_diag = jnp.max(s, axis=-1, keepdims=True)
        shift = jnp.where(tight, jnp.maximum(m_diag, q_norm[c] * k_norm[h] + cum_q0[h]), m_diag)
        hi = shift.astype(BF16)
        rest = shift - hi.astype(F32)
        lo = (rest + jnp.abs(rest) * (2.0 ** -7)).astype(BF16)
        shift = hi.astype(F32) + lo.astype(F32)
        accs.append(_dot(jnp.exp2(s - shift).astype(BF16), vt))
        shifts.append(shift)
        q_shifted.append(jnp.where(lane == HEAD_DIM + N_CUM_PARTS, -hi,
                                   jnp.where(lane == HEAD_DIM + N_CUM_PARTS + 1, -lo, q)))

    def may_matter(jt):
        return jnp.maximum(ncum_ref[0, jt + 1] - cum_q0[0], ncum_ref[1, jt + 1] - cum_q0[1]) >= FOX_DEAD

    def fixed_shift_sweep(accs):
        def body(carry):
            jt, accs = carry
            out = []
            for (h, r), q, acc in zip(chains, q_shifted, accs):
                kt, vt = tiles(jt, h)
                out.append(acc + _dot(jnp.exp2(_dot_nt(q, kt)).astype(BF16), vt))
            return jt - 1, tuple(out)

        return lax.while_loop(lambda cr: (cr[0] >= 0) & may_matter(jnp.maximum(cr[0], 0)), body,
                              (n_full - 1, accs))[1]

    def running_max_sweep(accs):
        def alive(jt, ms):
            go = None
            for h in range(2):
                m_min = functools.reduce(jnp.minimum, [jnp.min(ms[h * nsub + r]) for r in range(nsub)])
                live = qk_bound[h] + ncum_ref[h, jt + 1] - m_min >= FOX_DEAD
                go = live if go is None else (go | live)
            return go

        def body(carry):
            jt, _, ms, accs = carry
            new_m, new_acc = [], []
            for (h, r), q, m, acc in zip(chains, qs, ms, accs):
                kt, vt = tiles(jt, h)
                s = _dot_nt(q, kt)
                m_new = jnp.maximum(m, jnp.max(s, axis=-1, keepdims=True))
                new_acc.append(jnp.exp2(m - m_new) * acc + _dot(jnp.exp2(s - m_new).astype(BF16), vt))
                new_m.append(m_new)
            new_m = tuple(new_m)
            return jt - 1, alive(jnp.maximum(jt - 1, 0), new_m), new_m, tuple(new_acc)

        ms = tuple(shifts)
        return lax.while_loop(lambda cr: (cr[0] >= 0) & cr[1], body,
                              (n_full - 1, alive(jnp.maximum(n_full - 1, 0), ms), ms, accs))[3]

    accs = lax.cond(tight, fixed_shift_sweep, running_max_sweep, tuple(accs))
    heads = []
    for h in range(2):
        acc = jnp.concatenate([accs[h * nsub + r] for r in range(nsub)], axis=0)
        heads.append(acc * (1.0 / acc[:, HEAD_DIM:HEAD_DIM + 1]))
    o_ref[...] = jnp.where(feat, heads[0], pltpu.roll(heads[1], HEAD_DIM, 1)).astype(BF16)


def _fox_attention(q, k, v, stats, batch, seq, tq, tk, sub):
    t = q.shape[0]
    nq = seq // tq
    kv = pl.BlockSpec((seq, 2 * LANES), lambda b, p, i: (b, p))
    stats_rows = 8 * t // stats.shape[0]
    assert tk % stats_rows == 0
    tile_stats = stats.reshape(t // stats_rows, 8, LANES)
    tile_stats = jnp.concatenate([tile_stats[:, 0, :FOX_HEADS], tile_stats[:, 1, :FOX_HEADS]], axis=1)
    return pl.pallas_call(
        functools.partial(_fox_kernel, tq=tq, tk=tk, sub=sub, seq=seq, stats_rows=stats_rows),
        grid=(batch, FOX_HEADS // 2, nq),
        in_specs=[pl.BlockSpec(memory_space=pltpu.SMEM),
                  pl.BlockSpec((tq, 2 * LANES), lambda b, p, i: (b * nq + i, p)), kv, kv],
        out_specs=pl.BlockSpec((tq, LANES), lambda b, p, i: (b * nq + i, p)),
        out_shape=jax.ShapeDtypeStruct((t, FOX_W), BF16),
        scratch_shapes=[pltpu.SMEM((2,), F32), pltpu.SMEM((2, seq // tk), F32)],
        compiler_params=_params(("parallel", "parallel", "arbitrary")),
        name="fox_attention",
    )(tile_stats, q, k, v)


HALO = 8


def _ffn_kernel(*refs, tm, nsb, final, n_attn):
    attn_refs, wattn_ref, refs = refs[:n_attn], refs[n_attn], refs[n_attn + 1:]
    if final:
        x_ref, g_ref, win_ref, cw_ref, cb_ref, wout_ref, fn_ref, o_ref, a_scr, halo_scr = refs
    else:
        x_ref, g_ref, win_ref, cw_ref, cb_ref, wout_ref, o_ref, a_scr, halo_scr = refs
    i = pl.program_id(0)

    @pl.when(i % nsb == 0)
    def _():
        halo_scr[...] = jnp.zeros((HALO, D_FF), F32)

    x = x_ref[...]
    off = 0
    for a_ref in attn_refs:
        width = a_ref.shape[1]
        x = x + _dot(a_ref[...], wattn_ref[off:off + width, :])
        off += width
    hb = _rmsnorm(x, g_ref[...]).astype(BF16)
    a = _dot(hb, win_ref[:, 0:D_FF])
    b = _dot(hb, win_ref[:, D_FF:2 * D_FF])
    a_scr[0:HALO, :] = halo_scr[...]
    a_scr[HALO:HALO + tm, :] = a
    halo_scr[...] = a[tm - HALO:tm, :]
    conv = (cw_ref[0:1, :] * a_scr[HALO - 2:HALO - 2 + tm, :] + cw_ref[1:2, :] * a_scr[HALO - 1:HALO - 1 + tm, :]
            + cw_ref[2:3, :] * a + cb_ref[...])
    gated = conv * (1.0 / (1.0 + jnp.exp(-conv))) * b
    y = x + _dot(gated.astype(BF16), wout_ref[...])
    if final:
        y = _rmsnorm(y, fn_ref[...])
    o_ref[...] = y


def _mixer_out_ffn(xf, attn_outs, w_attn, g, w_in, conv_w, conv_b, w_out, final_norm, seq, tm):
    t, d = xf.shape
    nsb = seq // tm
    row = pl.BlockSpec((tm, d), lambda i: (i, 0))
    const = lambda shape: pl.BlockSpec(shape, lambda i: (0, 0), pipeline_mode=pl.Buffered(1))
    small = lambda shape: pl.BlockSpec(shape, lambda i: (0, 0))
    final = final_norm is not None
    in_specs = [pl.BlockSpec((tm, a.shape[1]), lambda i: (i, 0)) for a in attn_outs] + [const(w_attn.shape)]
    in_specs += [row, small((1, d)), const(w_in.shape), small(conv_w.shape), small((1, D_FF)), const(w_out.shape)]
    args = list(attn_outs) + [w_attn, xf, g, w_in, conv_w, conv_b, w_out]
    if final:
        in_specs.append(small((1, d)))
        args.append(final_norm)
    return pl.pallas_call(
        functools.partial(_ffn_kernel, tm=tm, nsb=nsb, final=final, n_attn=len(attn_outs)),
        grid=(t // tm,),
        in_specs=in_specs,
        out_specs=row,
        out_shape=jax.ShapeDtypeStruct((t, d), F32),
        scratch_shapes=[pltpu.VMEM((tm + HALO, D_FF), F32), pltpu.VMEM((HALO, D_FF), F32)],
        compiler_params=_params(("arbitrary",)),
        name="conv_glu_ffn",
    )(*args)


def kernel(x, attn_norm, ffn_norm, ev_w_in, ev_cmp_pos_k, ev_cmp_pos_v, ev_cmp_w_k, ev_cmp_w_v, ev_w_out,
           od_w_in, od_b_f, od_w_out, ffn_w_in, ffn_conv_w, ffn_conv_b, ffn_w_out, final_norm):
    batch, seq, d = x.shape
    t = batch * seq
    depth = attn_norm.shape[0]
    tm, ffn_tm = min(512, seq), min(512, seq)
    sb_tile, sb_nsub = min(256, seq), 2
    fox_tq, fox_tk, fox_sub = min(512, seq), min(512, seq), 512
    sel_tk = min(1024, seq)

    xf = x.reshape(t, d)
    tabs = _rope_tables(jnp.arange(seq))
    ctabs = _rope_tables(jnp.arange(seq // CMP_STRIDE) * CMP_STRIDE + (CMP_LEN - 1))
    mt, oh = _nsa_consts(seq)

    for layer in range(depth):
        g_attn = attn_norm[layer].reshape(1, d)
        if layer % 2 == 0:
            e = layer // 2
            (sbq, sbk, sbv, nq, kc_raw, vc_raw, ksl, vsl, kwn, vwn, gates) = _even_proj(
                xf, g_attn, _prep_even_w(ev_w_in[e]), tabs, seq, tm)
            pek, wk = _prep_cmp(ev_cmp_pos_k[e], ev_cmp_w_k[e])
            pev, wv = _prep_cmp(ev_cmp_pos_v[e], ev_cmp_w_v[e])
            kc, vc = _compress(kc_raw, vc_raw, pek, pev, wk, wv, ctabs, batch, seq)
            o_sb = _sb_attention(sbq, sbk, sbv, batch, seq, sb_tile, sb_nsub)
            o_nsa = _nsa_attention(nq, kc, vc, ksl, vsl, kwn, vwn, gates, mt, oh, batch, seq, sel_tk)
            attn_outs, w_attn = [o_sb, o_nsa], ev_w_out[e].astype(BF16)
        else:
            o = layer // 2
            w = jnp.pad(od_w_in[o], ((0, 0), (0, LANES - FOX_HEADS))).astype(BF16)
            bf = jnp.pad(od_b_f[o], (0, LANES - FOX_HEADS)).reshape(1, LANES)
            q, k, v, stats = _odd_proj(xf, g_attn, w, bf, seq, tm)
            o_fox = _fox_attention(q, k, v, stats, batch, seq, fox_tq, fox_tk, fox_sub)
            attn_outs, w_attn = [o_fox], od_w_out[o].astype(BF16)
        last = layer == depth - 1
        xf = _mixer_out_ffn(xf, attn_outs, w_attn, ffn_norm[layer].reshape(1, d), ffn_w_in[layer].astype(BF16),
                            ffn_conv_w[layer], ffn_conv_b[layer].reshape(1, D_FF), ffn_w_out[layer].astype(BF16),
                            final_norm.reshape(1, d) if last else None, seq, ffn_tm)
    return xf.reshape(batch, seq, d)
```
